```python
import math
import jax, jax.numpy as jnp
from jax import lax
import numpy as np


D_MODEL = 2048
BATCH = 2
SEQ = 16384
DEPTH = 2

N_A_LAYERS = DEPTH // 2
N_B_LAYERS = DEPTH - N_A_LAYERS
ALPHA = (2 * DEPTH) ** 0.25
BETA = (8 * DEPTH) ** -0.25
LN_EPS = 1e-5
RW_HEAD_DIM = 64
RW_HEADS = D_MODEL // RW_HEAD_DIM
RW_DECAY_RANK = 96
RW_A_RANK = 96
RW_GATE_RANK = 256
RW_GN_EPS = 64e-5
NSA_HEAD_DIM = 128
NSA_HEADS = D_MODEL // NSA_HEAD_DIM
NSA_KV_GROUPS = 2
NSA_HPG = NSA_HEADS // NSA_KV_GROUPS
CMP_STRIDE = 16
CMP_LEN = 2 * CMP_STRIDE
CMP_HIDDEN = 2 * NSA_HEAD_DIM
SEL_BLOCK = 64
N_SEL = 16
WINDOW = 512
Q_BLOCK = 128
N_EXPERTS = 32
TOP_K = 4
D_EXPERT = D_MODEL
SWIGLU_LIMIT = 7.0
SWIGLU_ALPHA = 1.702
ROW_BLOCK = 256

kernel_name = 'hybrid_rwkv7_nsa_yoco_moe'


def layer_norm(z, g, b):
    zf = z.astype(jnp.float32)
    mu = jnp.mean(zf, -1, keepdims=True)
    var = jnp.mean(jnp.square(zf - mu), -1, keepdims=True)
    return ((zf - mu) * lax.rsqrt(var + LN_EPS) * g + b).astype(z.dtype)


def masked_softmax(s, mask):
    s = jnp.where(mask, s, -jnp.inf)
    m = jnp.max(s, -1, keepdims=True)
    m = jnp.where(jnp.isfinite(m), m, 0.0)
    e = jnp.exp(s - m)
    d = jnp.sum(e, -1, keepdims=True)
    return e / jnp.where(d > 0, d, 1.0)


def wkv7_scan(r, w, k, v, a, b):
    B, T, H, N = r.shape

    def step(S, inp):
        r_t, w_t, k_t, v_t, a_t, b_t = inp
        sa = jnp.einsum('bhij,bhj->bhi', S, a_t)
        S = S * w_t[:, :, None, :] + sa[..., None] * b_t[:, :, None, :] + v_t[..., None] * k_t[:, :, None, :]
        return S, jnp.einsum('bhij,bhj->bhi', S, r_t)

    S0 = jnp.zeros((B, H, N, N), jnp.float32)
    seq = tuple(jnp.moveaxis(z, 1, 0) for z in (r, w, k, v, a, b))
    _, y = lax.scan(step, S0, seq, unroll=4)
    return jnp.moveaxis(y, 0, 1)


def rwkv7_time_mix(x, mu, w_rkv, w0, w1, w2, a0, a1, a2, g1, g2, k_k, k_a, r_k, lnx_g, lnx_b, w_o):
    B, T, D = x.shape
    H, N = RW_HEADS, RW_HEAD_DIM
    xx = jnp.pad(x[:, :-1], ((0, 0), (1, 0), (0, 0))) - x
    xr, xw, xk, xv, xa, xg = [x + xx * mu[i] for i in range(6)]
    r, k, v = jnp.einsum('pbtd,pde->pbte', jnp.stack([xr, xk, xv]), w_rkv)
    w_log = -jax.nn.softplus(-(w0 + jnp.tanh(xw @ w1) @ w2).astype(jnp.float32)) - 0.5
    decay = jnp.exp(-jnp.exp(w_log))
    a = jax.nn.sigmoid((a0 + (xa @ a1) @ a2).astype(jnp.float32))
    g = jax.nn.sigmoid(xg @ g1) @ g2
    heads = lambda z: z.reshape(B, T, H, N).astype(jnp.float32)
    r_h, k_h, v_h, a_h, w_h = heads(r), heads(k), heads(v), heads(a), heads(decay)
    kk = k_h * k_k.reshape(H, N)
    kk = kk / jnp.maximum(jnp.linalg.norm(kk, axis=-1, keepdims=True), 1e-12)
    k_h = k_h * (1.0 + (a_h - 1.0) * k_a.reshape(H, N))
    y = wkv7_scan(r_h, w_h, k_h, v_h, -kk, kk * a_h)
    ym = jnp.mean(y, -1, keepdims=True)
    yv = jnp.mean(jnp.square(y - ym), -1, keepdims=True)
    y = ((y - ym) * lax.rsqrt(yv + RW_GN_EPS)).reshape(B, T, D) * lnx_g + lnx_b
    bonus = jnp.sum(r_h * k_h * r_k, -1, keepdims=True) * v_h
    y = y + bonus.reshape(B, T, D)
    return (y * g).astype(x.dtype) @ w_o


def nsa_shared_kv(s, w_kv, cmp_pe, cmp_w1, cmp_b1, cmp_w2, cmp_b2):
    B, T, D = s.shape
    G, DK = NSA_KV_GROUPS, NSA_HEAD_DIM
    kv = (s @ w_kv).reshape(B, T, 6, G, DK)
    kc, vc, ks, vs, kw, vw = [kv[:, :, i] for i in range(6)]
    n_c = T // CMP_STRIDE - 1

    def compress(z, i):
        ch = z.reshape(B, T // CMP_STRIDE, CMP_STRIDE, G, DK)
        blk = jnp.concatenate([ch[:, :-1], ch[:, 1:]], axis=2)
        blk = blk + cmp_pe[i][:, None, :]
        flat = blk.transpose(0, 1, 3, 2, 4).reshape(B, n_c, G, CMP_LEN * DK)
        hdn = jax.nn.gelu(flat @ cmp_w1[i] + cmp_b1[i])
        return hdn @ cmp_w2[i] + cmp_b2[i]

    k_cmp, v_cmp = compress(kc, 0), compress(vc, 1)
    n_s = T // SEL_BLOCK
    k_sel = ks.reshape(B, n_s, SEL_BLOCK, G, DK).transpose(0, 3, 1, 2, 4)
    v_sel = vs.reshape(B, n_s, SEL_BLOCK, G, DK).transpose(0, 3, 1, 2, 4)
    k_win = jnp.pad(kw, ((0, 0), (WINDOW, 0), (0, 0), (0, 0)))
    v_win = jnp.pad(vw, ((0, 0), (WINDOW, 0), (0, 0), (0, 0)))
    return k_cmp, v_cmp, k_sel, v_sel, k_win, v_win


def nsa_attention(x, shared, w_in, b_gate, w_o):
    k_cmp, v_cmp, k_sel, v_sel, k_win, v_win = shared
    B, T, D = x.shape
    G, HPG, DK = NSA_KV_GROUPS, NSA_HPG, NSA_HEAD_DIM
    H = NSA_HEADS
    nqb = T // Q_BLOCK
    scale = NSA_HEAD_DIM ** -0.5
    proj = x @ w_in
    q = jnp.moveaxis(proj[..., :H * DK].reshape(B, nqb, Q_BLOCK, G, HPG, DK), 1, 0)
    gates = jax.nn.sigmoid((proj[..., H * DK:] + b_gate).astype(jnp.float32))
    gates = jnp.moveaxis(gates.reshape(B, nqb, Q_BLOCK, 3, G, HPG), 1, 0)
    n_c = k_cmp.shape[1]
    n_s = k_sel.shape[2]
    n_sel = min(N_SEL, n_s)
    c_end = jnp.arange(n_c) * CMP_STRIDE + CMP_LEN - 1
    c_lo = jnp.arange(n_c) * CMP_STRIDE
    s_lo = jnp.arange(n_s) * SEL_BLOCK
    overlap = ((c_lo[:, None] < s_lo[None, :] + SEL_BLOCK) & (c_lo[:, None] + CMP_LEN > s_lo[None, :])).astype(jnp.float32)
    sel_ids = jnp.arange(n_s)
    bi = jnp.arange(B)[:, None, None, None]
    gi = jnp.arange(G)[None, :, None, None]

    def block_fn(args):
        qb, qblk, gblk = args
        t = qb * Q_BLOCK + jnp.arange(Q_BLOCK)
        s_c = jnp.einsum('bqghd,bcgd->bghqc', qblk, k_cmp, preferred_element_type=jnp.float32) * scale
        p_c = masked_softmax(s_c, c_end[None, :] <= t[:, None])
        o_c = jnp.einsum('bghqc,bcgd->bqghd', p_c, v_cmp.astype(jnp.float32))
        imp = jnp.einsum('bghqc,cs->bgqs', p_c, overlap)
        valid = s_lo[None, :] <= t[:, None]
        cur = (t // SEL_BLOCK)[:, None]
        forced = (sel_ids[None, :] == 0) | (sel_ids[None, :] == cur) | (sel_ids[None, :] == cur - 1)
        score = jnp.where(valid, jnp.where(forced, jnp.inf, imp), -jnp.inf)
        idx = lax.top_k(score, n_sel)[1]
        kg = k_sel[bi, gi, idx]
        vg = v_sel[bi, gi, idx]
        kpos = idx[..., None] * SEL_BLOCK + jnp.arange(SEL_BLOCK)
        mask_s = (kpos <= t[None, None, :, None, None]).reshape(B, G, 1, Q_BLOCK, n_sel * SEL_BLOCK)
        s_s = jnp.einsum('bqghd,bgqskd->bghqsk', qblk, kg, preferred_element_type=jnp.float32) * scale
        p_s = masked_softmax(s_s.reshape(B, G, HPG, Q_BLOCK, n_sel * SEL_BLOCK), mask_s)
        o_s = jnp.einsum('bghqk,bgqkd->bqghd', p_s, vg.reshape(B, G, Q_BLOCK, n_sel * SEL_BLOCK, DK).astype(jnp.float32))
        kw = lax.dynamic_slice_in_dim(k_win, qb * Q_BLOCK, WINDOW + Q_BLOCK, axis=1)
        vw = lax.dynamic_slice_in_dim(v_win, qb * Q_BLOCK, WINDOW + Q_BLOCK, axis=1)
        kpos_w = qb * Q_BLOCK - WINDOW + jnp.arange(WINDOW + Q_BLOCK)
        mask_w = (kpos_w[None, :] <= t[:, None]) & (kpos_w[None, :] > t[:, None] - WINDOW) & (kpos_w[None, :] >= 0)
        s_w = jnp.einsum('bqghd,bkgd->bghqk', qblk, kw, preferred_element_type=jnp.float32) * scale
        p_w = masked_softmax(s_w, mask_w)
        o_w = jnp.einsum('bghqk,bkgd->bqghd', p_w, vw.astype(jnp.float32))
        o = gblk[:, :, 0, ..., None] * o_c + gblk[:, :, 1, ..., None] * o_s + gblk[:, :, 2, ..., None] * o_w
        return o.reshape(B, Q_BLOCK, H * DK).astype(x.dtype)

    out = lax.map(block_fn, (jnp.arange(nqb), q, gates))
    out = jnp.moveaxis(out, 0, 1).reshape(B, T, H * DK)
    return out @ w_o


def moe_ffn(h, router_w, router_b, w_gu, b_gu, w_down, b_down):
    B, T, D = h.shape
    x = h.reshape(-1, D)
    n = x.shape[0]
    logits = (x @ router_w + router_b).astype(jnp.float32)
    top_v, top_i = lax.top_k(logits, TOP_K)
    gates = jax.nn.softmax(top_v, axis=-1)
    flat_e = top_i.reshape(-1)
    flat_tok = jnp.repeat(jnp.arange(n), TOP_K)
    order = jnp.argsort(flat_e, stable=True)
    se = flat_e[order]
    counts = jnp.bincount(flat_e, length=N_EXPERTS)
    padded = ((counts + ROW_BLOCK - 1) // ROW_BLOCK) * ROW_BLOCK
    pad_end = jnp.cumsum(padded)
    pad_start = pad_end - padded
    start = jnp.cumsum(counts) - counts
    dest = pad_start[se] + (jnp.arange(n * TOP_K) - start[se])
    n_rows = ((n * TOP_K + ROW_BLOCK - 1) // ROW_BLOCK) * ROW_BLOCK + N_EXPERTS * ROW_BLOCK
    n_blk = n_rows // ROW_BLOCK
    tok_sorted = flat_tok[order]
    xs = jnp.zeros((n_rows, D), x.dtype).at[dest].set(x[tok_sorted])
    blk_e = jnp.minimum(jnp.searchsorted(pad_end, jnp.arange(n_blk) * ROW_BLOCK, side='right'), N_EXPERTS - 1)

    def expert_block(args):
        xb, e = args
        gu = xb @ w_gu[e] + b_gu[e]
        gate = jnp.minimum(gu[:, :D_EXPERT], SWIGLU_LIMIT)
        up = jnp.clip(gu[:, D_EXPERT:], -SWIGLU_LIMIT, SWIGLU_LIMIT)
        act = (up + 1.0) * (gate * jax.nn.sigmoid(gate * SWIGLU_ALPHA))
        return act @ w_down[e] + b_down[e]

    ys = lax.map(expert_block, (xs.reshape(n_blk, ROW_BLOCK, D), blk_e)).reshape(n_rows, D)
    w_sorted = gates.reshape(-1)[order]
    out = jnp.zeros((n, D), ys.dtype).at[tok_sorted].add(ys[dest] * w_sorted[:, None].astype(ys.dtype))
    return out.reshape(B, T, D)


def setup_inputs(seed: int = 0) -> dict:
    key = jax.random.key(seed)
    ks = iter(jax.random.split(key, 40))

    def nrm(shape, scale):
        return jax.random.normal(next(ks), shape, jnp.float32) * scale

    D = D_MODEL
    NA, NB = N_A_LAYERS, N_B_LAYERS
    HD = NSA_HEADS * NSA_HEAD_DIM
    G, DK = NSA_KV_GROUPS, NSA_HEAD_DIM
    E, F = N_EXPERTS, D_EXPERT
    inp = {}
    inp['x'] = nrm((BATCH, SEQ, D), 1.0)
    inp['ln_g'] = 1.0 + nrm((DEPTH, 2, D), 0.02)
    inp['ln_b'] = nrm((DEPTH, 2, D), 0.02)
    inp['rw_mu'] = jax.random.uniform(next(ks), (NA, 6, D), jnp.float32)
    inp['rw_w_rkv'] = nrm((NA, 3, D, D), D ** -0.5)
    inp['rw_w0'] = -1.0 + nrm((NA, D), 0.5)
    inp['rw_w1'] = nrm((NA, D, RW_DECAY_RANK), D ** -0.5)
    inp['rw_w2'] = nrm((NA, RW_DECAY_RANK, D), 0.5 * RW_DECAY_RANK ** -0.5)
    inp['rw_a0'] = nrm((NA, D), 0.5)
    inp['rw_a1'] = nrm((NA, D, RW_A_RANK), D ** -0.5)
    inp['rw_a2'] = nrm((NA, RW_A_RANK, D), 0.5 * RW_A_RANK ** -0.5)
    inp['rw_g1'] = nrm((NA, D, RW_GATE_RANK), D ** -0.5)
    inp['rw_g2'] = nrm((NA, RW_GATE_RANK, D), RW_GATE_RANK ** -0.5)
    inp['rw_k_k'] = 0.85 + nrm((NA, D), 0.05)
    inp['rw_k_a'] = 1.0 + nrm((NA, D), 0.05)
    inp['rw_r_k'] = nrm((NA, RW_HEADS, RW_HEAD_DIM), 0.1)
    inp['rw_lnx_g'] = 1.0 + nrm((NA, D), 0.02)
    inp['rw_lnx_b'] = nrm((NA, D), 0.02)
    inp['rw_w_o'] = nrm((NA, D, D), BETA * D ** -0.5)
    inp['nsa_w_kv'] = nrm((D, 6 * G * DK), D ** -0.5)
    inp['nsa_cmp_pe'] = nrm((2, CMP_LEN, DK), 0.02)
    inp['nsa_cmp_w1'] = nrm((2, CMP_LEN * DK, CMP_HIDDEN), (CMP_LEN * DK) ** -0.5)
    inp['nsa_cmp_b1'] = nrm((2, CMP_HIDDEN), 0.02)
    inp['nsa_cmp_w2'] = nrm((2, CMP_HIDDEN, DK), CMP_HIDDEN ** -0.5)
    inp['nsa_cmp_b2'] = nrm((2, DK), 0.02)
    inp['nsa_w_in'] = nrm((NB, D, HD + 3 * NSA_HEADS), D ** -0.5)
    inp['nsa_b_gate'] = nrm((NB, 3 * NSA_HEADS), 0.02)
    inp['nsa_w_o'] = nrm((NB, HD, D), BETA * HD ** -0.5)
    inp['moe_router_w'] = nrm((DEPTH, D, E), D ** -0.5)
    inp['moe_router_b'] = nrm((DEPTH, E), 0.01)
    inp['moe_w_gu'] = nrm((DEPTH, E, D, 2 * F), D ** -0.5)
    inp['moe_b_gu'] = nrm((DEPTH, E, 2 * F), 0.02)
    inp['moe_w_down'] = nrm((DEPTH, E, F, D), BETA * F ** -0.5)
    inp['moe_b_down'] = nrm((DEPTH, E, D), 0.02)
    return inp


def reference(x, ln_g, ln_b, rw_mu, rw_w_rkv, rw_w0, rw_w1, rw_w2, rw_a0, rw_a1, rw_a2, rw_g1, rw_g2,
              rw_k_k, rw_k_a, rw_r_k, rw_lnx_g, rw_lnx_b, rw_w_o, nsa_w_kv, nsa_cmp_pe, nsa_cmp_w1,
              nsa_cmp_b1, nsa_cmp_w2, nsa_cmp_b2, nsa_w_in, nsa_b_gate, nsa_w_o, moe_router_w,
              moe_router_b, moe_w_gu, moe_b_gu, moe_w_down, moe_b_down):
    shared = None
    for layer in range(DEPTH):
        if layer < N_A_LAYERS:
            i = layer
            mix = rwkv7_time_mix(x, rw_mu[i], rw_w_rkv[i], rw_w0[i], rw_w1[i], rw_w2[i], rw_a0[i], rw_a1[i],
                                 rw_a2[i], rw_g1[i], rw_g2[i], rw_k_k[i], rw_k_a[i], rw_r_k[i],
                                 rw_lnx_g[i], rw_lnx_b[i], rw_w_o[i])
        else:
            if layer == N_A_LAYERS:
                shared = nsa_shared_kv(x, nsa_w_kv, nsa_cmp_pe, nsa_cmp_w1, nsa_cmp_b1, nsa_cmp_w2, nsa_cmp_b2)
            j = layer - N_A_LAYERS
            mix = nsa_attention(x, shared, nsa_w_in[j], nsa_b_gate[j], nsa_w_o[j])
        x = layer_norm(ALPHA * x + mix, ln_g[layer, 0], ln_b[layer, 0])
        ffn = moe_ffn(x, moe_router_w[layer], moe_router_b[layer], moe_w_gu[layer], moe_b_gu[layer],
                      moe_w_down[layer], moe_b_down[layer])
        x = layer_norm(ALPHA * x + ffn, ln_g[layer, 1], ln_b[layer, 1])
    return x
```

```python
import functools
import math

import numpy as np
import jax
import jax.numpy as jnp
from jax import lax
from jax.experimental import pallas as pl
from jax.experimental.pallas import tpu as pltpu

F32 = jnp.float32
BF16 = jnp.bfloat16

V7X_VMEM_LIMIT_BYTES = 56 * 1024 * 1024
LANES = 128

LN_EPS = 1e-5
RW_HEAD_DIM = 64
RW_GN_EPS = 64e-5
RW_CHUNK = 64
RW_CHUNKS_PER_STEP = 4
NSA_HEAD_DIM = 128
NSA_KV_GROUPS = 2
CMP_STRIDE = 16
CMP_LEN = 32
SEL_BLOCK = 64
N_SEL = 16
WINDOW = 512
Q_BLOCK = 128
SEL_KEY_TILE = 512
N_EXPERTS = 32
TOP_K = 4
SWIGLU_LIMIT = 7.0
SWIGLU_ALPHA = 1.702
MOE_ROW_BLOCK = 512
MOE_F_TILE = 512
NEG_BIG = -1e30


def _params(*sem):
    return pltpu.CompilerParams(dimension_semantics=sem, vmem_limit_bytes=V7X_VMEM_LIMIT_BYTES)


def _bdot(a, b):
    return jnp.dot(a.astype(BF16), b.astype(BF16), preferred_element_type=F32)


def _bdot_nt(a, b):
    return lax.dot_general(a.astype(BF16), b.astype(BF16), (((1,), (1,)), ((), ())),
                           preferred_element_type=F32)


def _dot_const_split(c, x):
    hi = x.astype(BF16)
    lo = (x - hi.astype(F32)).astype(BF16)
    return (jnp.dot(c, hi, preferred_element_type=F32) + jnp.dot(c, lo, preferred_element_type=F32))


def _dot_split_const(x, c):
    hi = x.astype(BF16)
    lo = (x - hi.astype(F32)).astype(BF16)
    return (jnp.dot(hi, c, preferred_element_type=F32) + jnp.dot(lo, c, preferred_element_type=F32))


def _mm_body(a_ref, w_ref, b_ref, o_ref):
    acc = jnp.dot(a_ref[...], w_ref[...], preferred_element_type=F32) + b_ref[...]
    o_ref[...] = acc.astype(o_ref.dtype)


def _matmul(a, w, bias=None, out_dtype=F32, tm=512, tn=1024):
    m, k = a.shape
    n = w.shape[1]
    tm = min(tm, m)
    tn = min(tn, n)
    assert m % tm == 0 and n % tn == 0, (m, n, tm, tn)
    if bias is None:
        bias = jnp.zeros((1, n), F32)
    return pl.pallas_call(
        _mm_body,
        grid=(n // tn, m // tm),
        in_specs=[pl.BlockSpec((tm, k), lambda j, i: (i, 0)),
                  pl.BlockSpec((k, tn), lambda j, i: (0, j)),
                  pl.BlockSpec((1, tn), lambda j, i: (0, j))],
        out_specs=pl.BlockSpec((tm, tn), lambda j, i: (i, j)),
        out_shape=jax.ShapeDtypeStruct((m, n), out_dtype),
        compiler_params=_params("parallel", "parallel"),
        name="dense_matmul",
    )(a, w, bias.reshape(1, n).astype(F32))


def _add_ln_body(x_ref, m_ref, g_ref, b_ref, o_ref, ob_ref, *, alpha):
    z = alpha * x_ref[...] + m_ref[...]
    mu = jnp.mean(z, -1, keepdims=True)
    zc = z - mu
    var = jnp.mean(zc * zc, -1, keepdims=True)
    y = zc * lax.rsqrt(var + LN_EPS) * g_ref[...] + b_ref[...]
    o_ref[...] = y
    ob_ref[...] = y.astype(BF16)


def _add_ln(x, mix, g, b, alpha, tm=256):
    n, d = x.shape
    row = pl.BlockSpec((tm, d), lambda i: (i, 0))
    vec = pl.BlockSpec((1, d), lambda i: (0, 0))
    return pl.pallas_call(
        functools.partial(_add_ln_body, alpha=alpha),
        grid=(n // tm,),
        in_specs=[row, row, vec, vec],
        out_specs=[row, row],
        out_shape=[jax.ShapeDtypeStruct((n, d), F32), jax.ShapeDtypeStruct((n, d), BF16)],
        compiler_params=_params("parallel"),
        name="add_layer_norm",
    )(x, mix, g.reshape(1, d), b.reshape(1, d))


def _rw_mix_body(x_ref, last_ref, mu_ref, *o_refs):
    x = x_ref[...]
    prev = pltpu.roll(x, shift=1, axis=0)
    row = lax.broadcasted_iota(jnp.int32, x.shape, 0)
    prev = jnp.where(row == 0, last_ref[0], prev)
    xx = prev - x
    for i, o_ref in enumerate(o_refs):
        o_ref[...] = (x + xx * mu_ref[i:i + 1, :]).astype(BF16)


def _rw_mix(x2, batch, mu, tm=256):
    n, d = x2.shape
    t = n // batch
    nt = t // tm
    last = x2.reshape(batch, nt, tm, d)[:, :, tm - 1, :]
    last = jnp.concatenate([jnp.zeros((batch, 1, d), F32), last[:, :-1]], axis=1).reshape(batch * nt, 1, d)
    row = pl.BlockSpec((tm, d), lambda i: (i, 0))
    return pl.pallas_call(
        _rw_mix_body,
        grid=(n // tm,),
        in_specs=[row, pl.BlockSpec((1, 1, d), lambda i: (i, 0, 0)), pl.BlockSpec((6, d), lambda i: (0, 0))],
        out_specs=[row] * 6,
        out_shape=[jax.ShapeDtypeStruct((n, d), BF16)] * 6,
        compiler_params=_params("parallel"),
        name="rwkv_token_shift",
    )(x2, last, mu)


def _rw_lowrank_body(xw_ref, xa_ref, xg_ref, w1_ref, w2_ref, w0_ref, a1_ref, a2_ref, a0_ref, g1_ref, g2_ref,
                     lw_ref, a_ref, g_ref):
    z = w0_ref[...] + _bdot(jnp.tanh(jnp.dot(xw_ref[...], w1_ref[...], preferred_element_type=F32)), w2_ref[...])
    w_log = jnp.minimum(z, 0.0) - jnp.log(1.0 + jnp.exp(-jnp.abs(z))) - 0.5
    lw_ref[...] = -jnp.exp(w_log)
    za = a0_ref[...] + _bdot(jnp.dot(xa_ref[...], a1_ref[...], preferred_element_type=F32), a2_ref[...])
    a_ref[...] = jax.nn.sigmoid(za)
    hg = jax.nn.sigmoid(jnp.dot(xg_ref[...], g1_ref[...], preferred_element_type=F32))
    g_ref[...] = _bdot(hg, g2_ref[...])


def _pad_rank(w_in, w_out):
    r = w_in.shape[1]
    rp = -(-r // LANES) * LANES
    return (jnp.pad(w_in, ((0, 0), (0, rp - r))).astype(BF16), jnp.pad(w_out, ((0, rp - r), (0, 0))).astype(BF16))


def _rw_lowrank(xw, xa, xg, w0, w1, w2, a0, a1, a2, g1, g2, tm=256):
    n, d = xw.shape
    w1p, w2p = _pad_rank(w1, w2)
    a1p, a2p = _pad_rank(a1, a2)
    g1p, g2p = _pad_rank(g1, g2)
    row = pl.BlockSpec((tm, d), lambda i: (i, 0))
    full = lambda arr: pl.BlockSpec(arr.shape, lambda i: (0, 0))
    w0r, a0r = w0.reshape(1, d), a0.reshape(1, d)
    return pl.pallas_call(
        _rw_lowrank_body,
        grid=(n // tm,),
        in_specs=[row, row, row, full(w1p), full(w2p), full(w0r), full(a1p), full(a2p), full(a0r), full(g1p), full(g2p)],
        out_specs=[row] * 3,
        out_shape=[jax.ShapeDtypeStruct((n, d), F32)] * 3,
        compiler_params=_params("parallel"),
        name="rwkv_lowrank",
    )(xw, xa, xg, w1p, w2p, w0r, a1p, a2p, a0r, g1p, g2p)


def _wkv_body(r_ref, k_ref, v_ref, lw_ref, a_ref, g_ref, kk_ref, ka_ref, rk_ref, lng_ref, lnb_ref,
              o_ref, s_ref, rhat_ref, y0_ref, p_ref, q_ref, *, nchunk):
    L = RW_CHUNK
    H2 = 2 * L

    @pl.when(pl.program_id(2) == 0)
    def _():
        s_ref[...] = jnp.zeros_like(s_ref)

    lane = lax.broadcasted_iota(jnp.int32, (1, LANES), 1)
    mask0 = (lane < RW_HEAD_DIM).astype(F32)
    mask1 = 1.0 - mask0
    ri = lax.broadcasted_iota(jnp.int32, (H2, H2), 0)
    ci = lax.broadcasted_iota(jnp.int32, (H2, H2), 1)
    same_head = (ri // L) == (ci // L)
    strict = (same_head & (ci < ri)).astype(F32)
    incl = (same_head & (ci <= ri)).astype(F32)
    diag16 = ((ri // 16) == (ci // 16)).astype(F32)
    eye = (ri == ci).astype(F32)
    head_ones = same_head.astype(BF16)
    tl = lax.broadcasted_iota(jnp.int32, (L, L), 0)
    sl = lax.broadcasted_iota(jnp.int32, (L, L), 1)
    tri_incl = (sl <= tl).astype(BF16)

    def stack(x):
        return jnp.concatenate([x * mask0, x * mask1], axis=0)

    k_k = kk_ref[...]
    k_a = ka_ref[...]
    r_k = rk_ref[...]

    for c in range(nchunk):
        rows = pl.ds(c * L, L)
        r = r_ref[rows, :]
        k = k_ref[rows, :]
        v = v_ref[rows, :]
        lw = lw_ref[rows, :]
        ag = a_ref[rows, :]
        kk = k * k_k
        ss = _dot_split_const(kk * kk, head_ones)
        kk = kk / jnp.maximum(jnp.sqrt(ss), 1e-12)
        kmod = k * (1.0 + (ag - 1.0) * k_a)
        av = -kk
        bv = kk * ag
        cl = _dot_const_split(tri_incl, lw)
        cl_last = cl[L - 1:L, :]
        e_pos = jnp.exp(cl)
        e_neg = jnp.exp(-cl)
        e_end = jnp.exp(cl_last - cl)
        at = av * jnp.exp(cl - lw)
        rt = r * e_pos
        bt = bv * e_neg
        kt = kmod * e_neg
        at_s, rt_s, v_s = stack(at), stack(rt), stack(v)
        gmat = _bdot_nt(jnp.concatenate([at_s, rt_s], axis=0), jnp.concatenate([stack(bt), stack(kt)], axis=0))
        a_ab = gmat[:H2, :H2] * strict
        a_ak = gmat[:H2, H2:] * strict
        a_rb = gmat[H2:, :H2] * incl
        a_rk = gmat[H2:, H2:] * incl
        dblk = a_ab * diag16
        off = a_ab - dblk
        d2 = _bdot(dblk, dblk)
        d4 = _bdot(d2, d2)
        d8 = _bdot(d4, d4)
        dinv = _bdot(_bdot(_bdot(eye + dblk, eye + d2), eye + d4), eye + d8)
        e1 = _bdot(dinv, off)
        e2 = _bdot(e1, e1)
        minv = _bdot(_bdot(eye + e1, eye + e2), dinv)
        x_ak = _bdot(a_ak, v_s)
        zu = _bdot(minv, jnp.concatenate([at_s, x_ak], axis=1))
        w2 = _bdot(a_rb, zu)
        rhat_ref[c] = rt_s + w2[:, :LANES]
        y0_ref[c] = w2[:, LANES:] + _bdot(a_rk, v_s)
        pq = _bdot(stack(bv * e_end).T, zu)
        p_ref[c] = eye * jnp.exp(cl_last) + pq[:, :LANES]
        q_ref[c] = pq[:, LANES:] + _bdot(stack(kmod * e_end).T, v_s)

    for c in range(nchunk):
        rows = pl.ds(c * L, L)
        s = s_ref[...]
        y_s = _bdot(rhat_ref[c], s) + y0_ref[c]
        s_ref[...] = _bdot(p_ref[c], s) + q_ref[c]
        y = y_s[:L] + y_s[L:]
        r = r_ref[rows, :]
        k = k_ref[rows, :]
        v = v_ref[rows, :]
        kmod = k * (1.0 + (a_ref[rows, :] - 1.0) * k_a)
        inv_n = 1.0 / RW_HEAD_DIM
        ym = _dot_split_const(y, head_ones) * inv_n
        yc = y - ym
        yv = _dot_split_const(yc * yc, head_ones) * inv_n
        yn = yc * lax.rsqrt(yv + RW_GN_EPS) * lng_ref[...] + lnb_ref[...]
        bonus = _dot_split_const(r * kmod * r_k, head_ones) * v
        o_ref[rows, :] = ((yn + bonus) * g_ref[rows, :]).astype(BF16)


def _wkv(r, k, v, lw, a, g, k_k, k_a, r_k, lnx_g, lnx_b, batch):
    n, d = r.shape
    t = n // batch
    nchunk = RW_CHUNKS_PER_STEP
    tb = RW_CHUNK * nchunk
    while t % tb:
        nchunk //= 2
        tb = RW_CHUNK * nchunk
    nt = t // tb
    row = pl.BlockSpec((tb, LANES), lambda b, hp, c: (b * nt + c, hp))
    vec = pl.BlockSpec((1, LANES), lambda b, hp, c: (0, hp))
    sq = pltpu.VMEM((nchunk, LANES, LANES), F32)
    vecs = [z.reshape(1, d) for z in (k_k, k_a, r_k, lnx_g, lnx_b)]
    return pl.pallas_call(
        functools.partial(_wkv_body, nchunk=nchunk),
        grid=(batch, d // LANES, nt),
        in_specs=[row] * 6 + [vec] * 5,
        out_specs=row,
        out_shape=jax.ShapeDtypeStruct((n, d), BF16),
        scratch_shapes=[pltpu.VMEM((LANES, LANES), F32), sq, sq, sq, sq],
        compiler_params=_params("parallel", "parallel", "arbitrary"),
        name="rwkv_chunked_scan",
    )(r, k, v, lw, a, g, *vecs)


def _rwkv_time_mix(x2, batch, mu, w_rkv, w0, w1, w2, a0, a1, a2, g1, g2, k_k, k_a, r_k, lnx_g, lnx_b, w_o):
    xr, xw, xk, xv, xa, xg = _rw_mix(x2, batch, mu)
    r = _matmul(xr, w_rkv[0].astype(BF16))
    k = _matmul(xk, w_rkv[1].astype(BF16))
    v = _matmul(xv, w_rkv[2].astype(BF16))
    lw, a, g = _rw_lowrank(xw, xa, xg, w0, w1, w2, a0, a1, a2, g1, g2)
    z = _wkv(r, k, v, lw, a, g, k_k, k_a, r_k.reshape(-1), lnx_g, lnx_b, batch)
    return _matmul(z, w_o.astype(BF16))


def _router_body(x_ref, w_ref, b_ref, idx_ref, gate_ref):
    logits = jnp.dot(x_ref[...], w_ref[...], preferred_element_type=F32) + b_ref[...]
    lane = lax.broadcasted_iota(jnp.int32, logits.shape, 1).astype(F32)
    cur = logits
    vals, idxs = [], []
    for _ in range(TOP_K):
        m = jnp.max(cur, axis=-1, keepdims=True)
        i = jnp.min(jnp.where(cur == m, lane, float(LANES)), axis=-1, keepdims=True)
        vals.append(m)
        idxs.append(i)
        cur = jnp.where(lane == i, -3e38, cur)
    es = [jnp.exp(vv - vals[0]) for vv in vals]
    den = es[0]
    for e in es[1:]:
        den = den + e
    idx_out = jnp.zeros(logits.shape, F32)
    gate_out = jnp.zeros(logits.shape, F32)
    for kk in range(TOP_K):
        idx_out = jnp.where(lane == kk, idxs[kk], idx_out)
        gate_out = jnp.where(lane == kk, es[kk] / den, gate_out)
    idx_ref[...] = idx_out.astype(jnp.int32)
    gate_ref[...] = gate_out


def _router(xb, router_w, router_b, tm=512):
    n, d = xb.shape
    e = router_w.shape[1]
    wp = jnp.pad(router_w, ((0, 0), (0, LANES - e))).astype(BF16)
    bp = jnp.concatenate([router_b.astype(F32), jnp.full((LANES - e,), NEG_BIG, F32)]).reshape(1, LANES)
    row = pl.BlockSpec((tm, LANES), lambda i: (i, 0))
    idx, gate = pl.pallas_call(
        _router_body,
        grid=(n // tm,),
        in_specs=[pl.BlockSpec((tm, d), lambda i: (i, 0)), pl.BlockSpec((d, LANES), lambda i: (0, 0)),
                  pl.BlockSpec((1, LANES), lambda i: (0, 0))],
        out_specs=[row, row],
        out_shape=[jax.ShapeDtypeStruct((n, LANES), jnp.int32), jax.ShapeDtypeStruct((n, LANES), F32)],
        compiler_params=_params("parallel"),
        name="moe_router",
    )(xb, wp, bp)
    return idx[:, :TOP_K], gate[:, :TOP_K]


def _moe_expert_body(be_ref, nu_ref, x_ref, wg_ref, wu_ref, bg_ref, bu_ref, wd_ref, bd_ref, o_ref, acc_ref):
    i = pl.program_id(0)
    f = pl.program_id(1)
    last_f = pl.num_programs(1) - 1
    used = i < nu_ref[0]

    @pl.when(used)
    def _():
        x = x_ref[...]
        gate = jnp.dot(x, wg_ref[...], preferred_element_type=F32) + bg_ref[...]
        up = jnp.dot(x, wu_ref[...], preferred_element_type=F32) + bu_ref[...]
        gate = jnp.minimum(gate, SWIGLU_LIMIT)
        up = jnp.clip(up, -SWIGLU_LIMIT, SWIGLU_LIMIT)
        act = (up + 1.0) * (gate * jax.nn.sigmoid(gate * SWIGLU_ALPHA))
        part = jnp.dot(act.astype(BF16), wd_ref[...], preferred_element_type=F32)

        @pl.when(f == 0)
        def _():
            acc_ref[...] = part + bd_ref[...]

        @pl.when(f > 0)
        def _():
            acc_ref[...] += part

        @pl.when(f == last_f)
        def _():
            o_ref[...] = acc_ref[...]

    @pl.when(jnp.logical_not(used) & (f == last_f))
    def _():
        o_ref[...] = jnp.zeros_like(o_ref)


def _moe_experts(xs, blk_e, n_used, w_gu, b_gu, w_down, b_down):
    n_rows, d = xs.shape
    ne, _, f2 = w_gu.shape
    fdim = f2 // 2
    tm, tf = MOE_ROW_BLOCK, MOE_F_TILE
    nf = fdim // tf
    n_blk = n_rows // tm

    def xrow(i, f, be, nu):
        return (jnp.minimum(i, nu[0] - 1), 0)

    grid_spec = pltpu.PrefetchScalarGridSpec(
        num_scalar_prefetch=2,
        grid=(n_blk, nf),
        in_specs=[
            pl.BlockSpec((tm, d), xrow),
            pl.BlockSpec((None, d, tf), lambda i, f, be, nu: (be[i], 0, f)),
            pl.BlockSpec((None, d, tf), lambda i, f, be, nu: (be[i], 0, nf + f)),
            pl.BlockSpec((None, 1, tf), lambda i, f, be, nu: (be[i], 0, f)),
            pl.BlockSpec((None, 1, tf), lambda i, f, be, nu: (be[i], 0, nf + f)),
            pl.BlockSpec((None, tf, d), lambda i, f, be, nu: (be[i], f, 0)),
            pl.BlockSpec((None, 1, d), lambda i, f, be, nu: (be[i], 0, 0)),
        ],
        out_specs=pl.BlockSpec((tm, d), lambda i, f, be, nu: (i, 0)),
        scratch_shapes=[pltpu.VMEM((tm, d), F32)],
    )
    return pl.pallas_call(
        _moe_expert_body,
        grid_spec=grid_spec,
        out_shape=jax.ShapeDtypeStruct((n_rows, d), F32),
        compiler_params=_params("arbitrary", "arbitrary"),
        name="moe_experts",
    )(blk_e, n_used, xs, w_gu, w_gu, b_gu.reshape(ne, 1, f2), b_gu.reshape(ne, 1, f2), w_down,
      b_down.reshape(ne, 1, d))


def _moe_ffn(xb, router_w, router_b, w_gu, b_gu, w_down, b_down):
    n, d = xb.shape
    tm = MOE_ROW_BLOCK
    top_i, gates = _router(xb, router_w, router_b)
    flat_e = top_i.reshape(-1)
    onehot = (flat_e[:, None] == jnp.arange(N_EXPERTS, dtype=jnp.int32)[None, :]).astype(jnp.int32)
    csum = jnp.cumsum(onehot, axis=0)
    rank = jnp.take_along_axis(csum, flat_e[:, None], axis=1)[:, 0] - 1
    counts = csum[-1]
    padded = ((counts + tm - 1) // tm) * tm
    pad_end = jnp.cumsum(padded)
    pad_start = pad_end - padded
    dest = pad_start[flat_e] + rank
    n_rows = n * TOP_K + N_EXPERTS * tm
    n_blk = n_rows // tm
    blk_e = jnp.minimum(jnp.searchsorted(pad_end, jnp.arange(n_blk, dtype=jnp.int32) * tm, side='right'),
                        N_EXPERTS - 1).astype(jnp.int32)
    n_used = (pad_end[-1] // tm).astype(jnp.int32).reshape(1)
    flat_tok = jnp.arange(n * TOP_K, dtype=jnp.int32) // TOP_K
    row_tok = jnp.zeros((n_rows,), jnp.int32).at[dest].set(flat_tok)
    xs = jnp.take(xb, row_tok, axis=0)
    ys = _moe_experts(xs, blk_e, n_used, w_gu.astype(BF16), b_gu, w_down.astype(BF16), b_down)
    y4 = jnp.take(ys, dest, axis=0).reshape(n, TOP_K, d)
    return jnp.sum(y4 * gates[:, :, None], axis=1)


def _cmp_mlp_body(x_ref, pe_ref, w1_ref, b1_ref, w2_ref, b2_ref, o_ref):
    h = _bdot(x_ref[...] + pe_ref[...], w1_ref[...]) + b1_ref[...]
    h = jax.nn.gelu(h)
    o_ref[...] = (_bdot(h, w2_ref[...]) + b2_ref[...]).astype(o_ref.dtype)


def _cmp_mlp(flat, pe, w1, b1, w2, b2, tm=256):
    m, kd = flat.shape
    hid = w1.shape[1]
    dk = w2.shape[1]
    full = lambda shp: pl.BlockSpec(shp, lambda i: (0, 0))
    return pl.pallas_call(
        _cmp_mlp_body,
        grid=(m // tm,),
        in_specs=[pl.BlockSpec((tm, kd), lambda i: (i, 0)), full((1, kd)), full((kd, hid)), full((1, hid)),
                  full((hid, dk)), full((1, dk))],
        out_specs=pl.BlockSpec((tm, dk), lambda i: (i, 0)),
        out_shape=jax.ShapeDtypeStruct((m, dk), BF16),
        compiler_params=_params("parallel"),
        name="nsa_compress_mlp",
    )(flat, pe.reshape(1, kd), w1.astype(BF16), b1.reshape(1, hid), w2.astype(BF16), b2.reshape(1, dk))


def _group_rows(q_ref, hpg):
    dk = NSA_HEAD_DIM
    return jnp.concatenate([q_ref[:, h * dk:(h + 1) * dk] for h in range(hpg)], axis=0)


def _softmax_rows(s, mask):
    s = jnp.where(mask, s, NEG_BIG)
    m = jnp.max(s, axis=-1, keepdims=True)
    e = jnp.where(mask, jnp.exp(s - m), 0.0)
    den = jnp.sum(e, axis=-1, keepdims=True)
    return e / jnp.where(den > 0, den, 1.0)


def _nsa_cmp_body(q_ref, kc_ref, vc_ref, ov_ref, o_ref, sel_ref, *, hpg, n_c, n_sel, scale):
    qb = pl.program_id(2)
    t0 = qb * Q_BLOCK
    ncp = kc_ref.shape[0]
    n_s = ov_ref.shape[1]
    q8 = _group_rows(q_ref, hpg)
    s = _bdot_nt(q8, kc_ref[...]) * scale
    rid = lax.broadcasted_iota(jnp.int32, s.shape, 0)
    cid = lax.broadcasted_iota(jnp.int32, s.shape, 1)
    t = t0 + (rid % Q_BLOCK)
    valid = (cid * CMP_STRIDE + (CMP_LEN - 1) <= t) & (cid < n_c)
    p = _softmax_rows(s, valid)
    pb = p.astype(BF16)
    o = jnp.dot(pb, vc_ref[...], preferred_element_type=F32)
    for h in range(hpg):
        o_ref[:, h * NSA_HEAD_DIM:(h + 1) * NSA_HEAD_DIM] = o[h * Q_BLOCK:(h + 1) * Q_BLOCK]
    imp_h = jnp.dot(pb, ov_ref[...], preferred_element_type=F32)
    imp = imp_h[:Q_BLOCK]
    for h in range(1, hpg):
        imp = imp + imp_h[h * Q_BLOCK:(h + 1) * Q_BLOCK]
    tq = t0 + lax.broadcasted_iota(jnp.int32, (Q_BLOCK, n_s), 0)
    sid_i = lax.broadcasted_iota(jnp.int32, (Q_BLOCK, n_s), 1)
    cur = tq // SEL_BLOCK
    forced = (sid_i == 0) | (sid_i == cur) | (sid_i == cur - 1)
    score = jnp.where(sid_i * SEL_BLOCK <= tq, jnp.where(forced, 1e30, imp), -1.0)
    sid = sid_i.astype(F32)
    sel = jnp.zeros((Q_BLOCK, n_s), F32)
    for _ in range(n_sel):
        m = jnp.max(score, axis=-1, keepdims=True)
        first = jnp.min(jnp.where(score == m, sid, float(n_s)), axis=-1, keepdims=True)
        hit = sid == first
        sel = jnp.where(hit, 1.0, sel)
        score = jnp.where(hit, -2.0, score)
    sel_ref[...] = sel.astype(BF16)


def _nsa_win_body(q_ref, *refs, hpg, nwb, scale):
    k_refs, v_refs, o_ref = refs[:nwb], refs[nwb:2 * nwb], refs[2 * nwb]
    qb = pl.program_id(2)
    t0 = qb * Q_BLOCK
    q8 = _group_rows(q_ref, hpg)
    kcat = jnp.concatenate([r[...] for r in k_refs], axis=0)
    vcat = jnp.concatenate([r[...] for r in v_refs], axis=0)
    s = _bdot_nt(q8, kcat) * scale
    rid = lax.broadcasted_iota(jnp.int32, s.shape, 0)
    cid = lax.broadcasted_iota(jnp.int32, s.shape, 1)
    t = t0 + (rid % Q_BLOCK)
    kpos = t0 - WINDOW + cid
    mask = (kpos <= t) & (kpos > t - WINDOW) & (kpos >= 0)
    p = _softmax_rows(s, mask)
    o = jnp.dot(p.astype(BF16), vcat, preferred_element_type=F32)
    for h in range(hpg):
        o_ref[:, h * NSA_HEAD_DIM:(h + 1) * NSA_HEAD_DIM] = o[h * Q_BLOCK:(h + 1) * Q_BLOCK]


def _nsa_sel_body(qb_ref, kb_ref, q_ref, k_ref, v_ref, sel_ref, ex_ref, o_ref, m_ref, l_ref, acc_ref, *, hpg, scale):
    step = pl.program_id(2)
    qb = qb_ref[step]
    kb = kb_ref[step]
    t0 = qb * Q_BLOCK
    tk = k_ref.shape[0]

    @pl.when(kb == 0)
    def _():
        m_ref[...] = jnp.full_like(m_ref, NEG_BIG)
        l_ref[...] = jnp.zeros_like(l_ref)
        acc_ref[...] = jnp.zeros_like(acc_ref)

    picked = jnp.dot(sel_ref[...], ex_ref[...], preferred_element_type=F32)
    tq = t0 + lax.broadcasted_iota(jnp.int32, (Q_BLOCK, tk), 0)
    kpos = kb * tk + lax.broadcasted_iota(jnp.int32, (Q_BLOCK, tk), 1)
    allowed = (picked > 0.5) & (kpos <= tq)
    allowed8 = jnp.concatenate([allowed.astype(F32)] * hpg, axis=0) > 0.5
    q8 = _group_rows(q_ref, hpg)
    s = jnp.where(allowed8, _bdot_nt(q8, k_ref[...]) * scale, NEG_BIG)
    m_old = m_ref[...]
    m_new = jnp.maximum(m_old, jnp.max(s, axis=-1, keepdims=True))
    p = jnp.where(allowed8, jnp.exp(s - m_new), 0.0)
    alpha = jnp.exp(m_old - m_new)
    l_ref[...] = alpha * l_ref[...] + jnp.sum(p, axis=-1, keepdims=True)
    acc_ref[...] = alpha * acc_ref[...] + jnp.dot(p.astype(BF16), v_ref[...], preferred_element_type=F32)
    m_ref[...] = m_new

    @pl.when(kb == (t0 + Q_BLOCK - 1) // tk)
    def _():
        den = l_ref[...]
        o = acc_ref[...] / jnp.where(den > 0, den, 1.0)
        for h in range(hpg):
            o_ref[:, h * NSA_HEAD_DIM:(h + 1) * NSA_HEAD_DIM] = o[h * Q_BLOCK:(h + 1) * Q_BLOCK]


def _nsa_attention(q, kvb, k_cmp, v_cmp, batch, n_c):
    n, hd = q.shape
    dk, g = NSA_HEAD_DIM, NSA_KV_GROUPS
    hpg = hd // dk // g
    t = n // batch
    nqb = t // Q_BLOCK
    n_s = t // SEL_BLOCK
    n_sel = min(N_SEL, n_s)
    ncp = k_cmp.shape[2]
    scale = dk ** -0.5
    gw = hpg * dk

    c_lo = np.arange(ncp) * CMP_STRIDE
    s_lo = np.arange(n_s) * SEL_BLOCK
    overlap = ((c_lo[:, None] < s_lo[None, :] + SEL_BLOCK) & (c_lo[:, None] + CMP_LEN > s_lo[None, :])
               & (np.arange(ncp)[:, None] < n_c))
    overlap = jnp.asarray(overlap, BF16)

    qspec = pl.BlockSpec((Q_BLOCK, gw), lambda b, gi, qb: (b * nqb + qb, gi))
    o_c, sel = pl.pallas_call(
        functools.partial(_nsa_cmp_body, hpg=hpg, n_c=n_c, n_sel=n_sel, scale=scale),
        grid=(batch, g, nqb),
        in_specs=[qspec,
                  pl.BlockSpec((None, None, ncp, dk), lambda b, gi, qb: (b, gi, 0, 0)),
                  pl.BlockSpec((None, None, ncp, dk), lambda b, gi, qb: (b, gi, 0, 0)),
                  pl.BlockSpec((ncp, n_s), lambda b, gi, qb: (0, 0))],
        out_specs=[qspec, pl.BlockSpec((None, None, Q_BLOCK, n_s), lambda b, gi, qb: (b, gi, qb, 0))],
        out_shape=[jax.ShapeDtypeStruct((n, hd), F32), jax.ShapeDtypeStruct((batch, g, t, n_s), BF16)],
        compiler_params=_params("parallel", "parallel", "parallel"),
        name="nsa_compressed_select",
    )(q, k_cmp, v_cmp, overlap)

    nwb = WINDOW // Q_BLOCK + 1
    kcol, vcol = 4 * g, 5 * g

    def kv_spec(col, j):
        return pl.BlockSpec((Q_BLOCK, dk),
                            lambda b, gi, qb: (b * nqb + jnp.maximum(qb - (nwb - 1) + j, 0), col + gi))

    o_w = pl.pallas_call(
        functools.partial(_nsa_win_body, hpg=hpg, nwb=nwb, scale=scale),
        grid=(batch, g, nqb),
        in_specs=[qspec] + [kv_spec(kcol, j) for j in range(nwb)] + [kv_spec(vcol, j) for j in range(nwb)],
        out_specs=qspec,
        out_shape=jax.ShapeDtypeStruct((n, hd), F32),
        compiler_params=_params("parallel", "parallel", "parallel"),
        name="nsa_window",
    )(q, *([kvb] * (2 * nwb)))

    tk = min(SEL_KEY_TILE, t)
    nkb = t // tk
    steps = [(qb, kb) for qb in range(nqb) for kb in range((qb * Q_BLOCK + Q_BLOCK - 1) // tk + 1)]
    qb_tab = jnp.asarray([s_[0] for s_ in steps], jnp.int32)
    kb_tab = jnp.asarray([s_[1] for s_ in steps], jnp.int32)
    expand = jnp.asarray(np.arange(n_s)[:, None] == (np.arange(t)[None, :] // SEL_BLOCK), BF16)
    kscol, vscol = 2 * g, 3 * g
    tkq = tk // Q_BLOCK
    grid_spec = pltpu.PrefetchScalarGridSpec(
        num_scalar_prefetch=2,
        grid=(batch, g, len(steps)),
        in_specs=[
            pl.BlockSpec((Q_BLOCK, gw), lambda b, gi, s_, qt, kt: (b * nqb + qt[s_], gi)),
            pl.BlockSpec((tk, dk), lambda b, gi, s_, qt, kt: (b * nkb + kt[s_], kscol + gi)),
            pl.BlockSpec((tk, dk), lambda b, gi, s_, qt, kt: (b * nkb + kt[s_], vscol + gi)),
            pl.BlockSpec((None, None, Q_BLOCK, n_s), lambda b, gi, s_, qt, kt: (b, gi, qt[s_], 0)),
            pl.BlockSpec((n_s, tk), lambda b, gi, s_, qt, kt: (0, kt[s_])),
        ],
        out_specs=pl.BlockSpec((Q_BLOCK, gw), lambda b, gi, s_, qt, kt: (b * nqb + qt[s_], gi)),
        scratch_shapes=[pltpu.VMEM((hpg * Q_BLOCK, 1), F32), pltpu.VMEM((hpg * Q_BLOCK, 1), F32),
                        pltpu.VMEM((hpg * Q_BLOCK, dk), F32)],
    )
    del tkq
    o_s = pl.pallas_call(
        functools.partial(_nsa_sel_body, hpg=hpg, scale=scale),
        grid_spec=grid_spec,
        out_shape=jax.ShapeDtypeStruct((n, hd), F32),
        compiler_params=_params("parallel", "parallel", "arbitrary"),
        name="nsa_selected",
    )(qb_tab, kb_tab, q, kvb, kvb, sel, expand)
    return o_c, o_s, o_w


def _nsa_shared_kv(xb, batch, w_kv, cmp_pe, cmp_w1, cmp_b1, cmp_w2, cmp_b2):
    n, d = xb.shape
    g, dk = NSA_KV_GROUPS, NSA_HEAD_DIM
    t = n // batch
    kv = _matmul(xb, w_kv.astype(BF16), tn=768)
    n_c = t // CMP_STRIDE - 1
    ncp = -(-n_c // LANES) * LANES
    rows = batch * n_c * g
    rows_p = -(-rows // 256) * 256
    outs = []
    for i in range(2):
        z = kv[:, i * g * dk:(i + 1) * g * dk].reshape(batch, t // CMP_STRIDE, CMP_STRIDE, g, dk)
        blk = jnp.concatenate([z[:, :-1], z[:, 1:]], axis=2)
        flat = blk.transpose(0, 1, 3, 2, 4).reshape(rows, CMP_LEN * dk)
        flat = jnp.pad(flat, ((0, rows_p - rows), (0, 0)))
        pe = jnp.broadcast_to(cmp_pe[i][:, None, :], (CMP_LEN, 1, dk)).reshape(CMP_LEN * dk)
        c = _cmp_mlp(flat, pe, cmp_w1[i], cmp_b1[i], cmp_w2[i], cmp_b2[i])[:rows]
        c = c.reshape(batch, n_c, g, dk).transpose(0, 2, 1, 3)
        outs.append(jnp.pad(c, ((0, 0), (0, 0), (0, ncp - n_c), (0, 0))))
    return kv.astype(BF16), outs[0], outs[1], n_c


def _nsa_layer(xb, batch, shared, w_in, b_gate, w_o):
    kvb, k_cmp, v_cmp, n_c = shared
    n, d = xb.shape
    hd = w_o.shape[0]
    nh = hd // NSA_HEAD_DIM
    q = _matmul(xb, w_in[:, :hd].astype(BF16), out_dtype=BF16)
    wg = jnp.pad(w_in[:, hd:], ((0, 0), (0, LANES - 3 * nh))).astype(BF16)
    bg = jnp.pad(b_gate, (0, LANES - 3 * nh))
    glog = _matmul(xb, wg, bias=bg)[:, :3 * nh]
    gates = jax.nn.sigmoid(glog).reshape(n, 3, nh, 1)
    o_c, o_s, o_w = _nsa_attention(q, kvb, k_cmp, v_cmp, batch, n_c)
    o = (gates[:, 0] * o_c.reshape(n, nh, -1) + gates[:, 1] * o_s.reshape(n, nh, -1)
         + gates[:, 2] * o_w.reshape(n, nh, -1))
    return _matmul(o.reshape(n, hd).astype(BF16), w_o.astype(BF16))


def kernel(x, ln_g, ln_b, rw_mu, rw_w_rkv, rw_w0, rw_w1, rw_w2, rw_a0, rw_a1, rw_a2, rw_g1, rw_g2, rw_k_k, rw_k_a, rw_r_k, rw_lnx_g, rw_lnx_b, rw_w_o, nsa_w_kv, nsa_cmp_pe, nsa_cmp_w1, nsa_cmp_b1, nsa_cmp_w2, nsa_cmp_b2, nsa_w_in, nsa_b_gate, nsa_w_o, moe_router_w, moe_router_b, moe_w_gu, moe_b_gu, moe_w_down, moe_b_down):
    batch, t, d = x.shape
    depth = ln_g.shape[0]
    n_a = rw_mu.shape[0]
    alpha = (2 * depth) ** 0.25
    h = x.reshape(batch * t, d)
    hb = None
    shared = None
    for layer in range(depth):
        if layer < n_a:
            i = layer
            mix = _rwkv_time_mix(h, batch, rw_mu[i], rw_w_rkv[i], rw_w0[i], rw_w1[i], rw_w2[i], rw_a0[i], rw_a1[i],
                                 rw_a2[i], rw_g1[i], rw_g2[i], rw_k_k[i], rw_k_a[i], rw_r_k[i], rw_lnx_g[i],
                                 rw_lnx_b[i], rw_w_o[i])
        else:
            if shared is None:
                if hb is None:
                    hb = h.astype(BF16)
                shared = _nsa_shared_kv(hb, batch, nsa_w_kv, nsa_cmp_pe, nsa_cmp_w1, nsa_cmp_b1, nsa_cmp_w2,
                                        nsa_cmp_b2)
            j = layer - n_a
            mix = _nsa_layer(hb, batch, shared, nsa_w_in[j], nsa_b_gate[j], nsa_w_o[j])
        h, hb = _add_ln(h, mix, ln_g[layer, 0], ln_b[layer, 0], alpha)
        ffn = _moe_ffn(hb, moe_router_w[layer], moe_router_b[layer], moe_w_gu[layer], moe_b_gu[layer],
                       moe_w_down[layer], moe_b_down[layer])
        h, hb = _add_ln(h, ffn, ln_g[layer, 1], ln_b[layer, 1], alpha)
    return h.reshape(batch, t, d)
```

```python
import functools
import math

import numpy as np
import jax
import jax.numpy as jnp
from jax import lax
from jax.experimental import pallas as pl
from jax.experimental.pallas import tpu as pltpu

F32 = jnp.float32
BF16 = jnp.bfloat16

V7X_VMEM_LIMIT_BYTES = 56 * 1024 * 1024
LANES = 128

LN_EPS = 1e-5
RW_HEAD_DIM = 64
RW_GN_EPS = 64e-5
RW_CHUNK = 64
RW_CHUNKS_PER_STEP = 8
NSA_HEAD_DIM = 128
NSA_KV_GROUPS = 2
CMP_STRIDE = 16
CMP_LEN = 32
SEL_BLOCK = 64
N_SEL = 16
WINDOW = 512
Q_BLOCK = 128
SEL_KEY_TILE = 1024
SEL_HEADS_PER_DOT = 2
N_EXPERTS = 32
TOP_K = 4
SWIGLU_LIMIT = 7.0
SWIGLU_ALPHA = 1.702
MOE_ROW_BLOCK = 512
MOE_F_TILE = 512
NEG_BIG = -1e30


def _params(*sem):
    return pltpu.CompilerParams(dimension_semantics=sem, vmem_limit_bytes=V7X_VMEM_LIMIT_BYTES)


def _bdot(a, b):
    return jnp.dot(a.astype(BF16), b.astype(BF16), preferred_element_type=F32)


def _bdot_nt(a, b):
    return lax.dot_general(a.astype(BF16), b.astype(BF16), (((1,), (1,)), ((), ())),
                           preferred_element_type=F32)


def _dot_const_split(c, x):
    hi = x.astype(BF16)
    lo = (x - hi.astype(F32)).astype(BF16)
    return (jnp.dot(c, hi, preferred_element_type=F32) + jnp.dot(c, lo, preferred_element_type=F32))


def _dot_split_const(x, c):
    hi = x.astype(BF16)
    lo = (x - hi.astype(F32)).astype(BF16)
    return (jnp.dot(hi, c, preferred_element_type=F32) + jnp.dot(lo, c, preferred_element_type=F32))


def _mm_body(a_ref, w_ref, b_ref, o_ref):
    acc = jnp.dot(a_ref[...], w_ref[...], preferred_element_type=F32) + b_ref[...]
    o_ref[...] = acc.astype(o_ref.dtype)


def _matmul(a, w, bias=None, out_dtype=F32, tm=512, tn=1024):
    m, k = a.shape
    n = w.shape[1]
    tm = min(tm, m)
    tn = min(tn, n)
    assert m % tm == 0 and n % tn == 0, (m, n, tm, tn)
    if bias is None:
        bias = jnp.zeros((1, n), F32)
    return pl.pallas_call(
        _mm_body,
        grid=(n // tn, m // tm),
        in_specs=[pl.BlockSpec((tm, k), lambda j, i: (i, 0)),
                  pl.BlockSpec((k, tn), lambda j, i: (0, j)),
                  pl.BlockSpec((1, tn), lambda j, i: (0, j))],
        out_specs=pl.BlockSpec((tm, tn), lambda j, i: (i, j)),
        out_shape=jax.ShapeDtypeStruct((m, n), out_dtype),
        compiler_params=_params("parallel", "parallel"),
        name="dense_matmul",
    )(a, w, bias.reshape(1, n).astype(F32))


def _add_ln_body(x_ref, m_ref, g_ref, b_ref, o_ref, ob_ref, *, alpha):
    z = alpha * x_ref[...] + m_ref[...]
    mu = jnp.mean(z, -1, keepdims=True)
    zc = z - mu
    var = jnp.mean(zc * zc, -1, keepdims=True)
    y = zc * lax.rsqrt(var + LN_EPS) * g_ref[...] + b_ref[...]
    o_ref[...] = y
    ob_ref[...] = y.astype(BF16)


def _add_ln(x, mix, g, b, alpha, tm=256):
    n, d = x.shape
    row = pl.BlockSpec((tm, d), lambda i: (i, 0))
    vec = pl.BlockSpec((1, d), lambda i: (0, 0))
    return pl.pallas_call(
        functools.partial(_add_ln_body, alpha=alpha),
        grid=(n // tm,),
        in_specs=[row, row, vec, vec],
        out_specs=[row, row],
        out_shape=[jax.ShapeDtypeStruct((n, d), F32), jax.ShapeDtypeStruct((n, d), BF16)],
        compiler_params=_params("parallel"),
        name="add_layer_norm",
    )(x, mix, g.reshape(1, d), b.reshape(1, d))


def _rw_mix_body(x_ref, last_ref, mu_ref, *o_refs):
    x = x_ref[...]
    prev = pltpu.roll(x, shift=1, axis=0)
    row = lax.broadcasted_iota(jnp.int32, x.shape, 0)
    prev = jnp.where(row == 0, last_ref[0], prev)
    xx = prev - x
    for i, o_ref in enumerate(o_refs):
        o_ref[...] = (x + xx * mu_ref[i:i + 1, :]).astype(BF16)


def _rw_mix(x2, batch, mu, tm=256):
    n, d = x2.shape
    t = n // batch
    nt = t // tm
    last = x2.reshape(batch, nt, tm, d)[:, :, tm - 1, :]
    last = jnp.concatenate([jnp.zeros((batch, 1, d), F32), last[:, :-1]], axis=1).reshape(batch * nt, 1, d)
    row = pl.BlockSpec((tm, d), lambda i: (i, 0))
    return pl.pallas_call(
        _rw_mix_body,
        grid=(n // tm,),
        in_specs=[row, pl.BlockSpec((1, 1, d), lambda i: (i, 0, 0)), pl.BlockSpec((6, d), lambda i: (0, 0))],
        out_specs=[row] * 6,
        out_shape=[jax.ShapeDtypeStruct((n, d), BF16)] * 6,
        compiler_params=_params("parallel"),
        name="rwkv_token_shift",
    )(x2, last, mu)


def _rw_lowrank_body(xw_ref, xa_ref, xg_ref, w1_ref, w2_ref, w0_ref, a1_ref, a2_ref, a0_ref, g1_ref, g2_ref,
                     lw_ref, a_ref, g_ref):
    z = w0_ref[...] + _bdot(jnp.tanh(jnp.dot(xw_ref[...], w1_ref[...], preferred_element_type=F32)), w2_ref[...])
    w_log = jnp.minimum(z, 0.0) - jnp.log(1.0 + jnp.exp(-jnp.abs(z))) - 0.5
    lw_ref[...] = -jnp.exp(w_log)
    za = a0_ref[...] + _bdot(jnp.dot(xa_ref[...], a1_ref[...], preferred_element_type=F32), a2_ref[...])
    a_ref[...] = jax.nn.sigmoid(za)
    hg = jax.nn.sigmoid(jnp.dot(xg_ref[...], g1_ref[...], preferred_element_type=F32))
    g_ref[...] = _bdot(hg, g2_ref[...])


def _pad_rank(w_in, w_out):
    r = w_in.shape[1]
    rp = -(-r // LANES) * LANES
    return (jnp.pad(w_in, ((0, 0), (0, rp - r))).astype(BF16), jnp.pad(w_out, ((0, rp - r), (0, 0))).astype(BF16))


def _rw_lowrank(xw, xa, xg, w0, w1, w2, a0, a1, a2, g1, g2, tm=256):
    n, d = xw.shape
    w1p, w2p = _pad_rank(w1, w2)
    a1p, a2p = _pad_rank(a1, a2)
    g1p, g2p = _pad_rank(g1, g2)
    row = pl.BlockSpec((tm, d), lambda i: (i, 0))
    full = lambda arr: pl.BlockSpec(arr.shape, lambda i: (0, 0))
    w0r, a0r = w0.reshape(1, d), a0.reshape(1, d)
    return pl.pallas_call(
        _rw_lowrank_body,
        grid=(n // tm,),
        in_specs=[row, row, row, full(w1p), full(w2p), full(w0r), full(a1p), full(a2p), full(a0r), full(g1p), full(g2p)],
        out_specs=[row] * 3,
        out_shape=[jax.ShapeDtypeStruct((n, d), F32)] * 3,
        compiler_params=_params("parallel"),
        name="rwkv_lowrank",
    )(xw, xa, xg, w1p, w2p, w0r, a1p, a2p, a0r, g1p, g2p)


def _wkv_body(r_ref, k_ref, v_ref, lw_ref, a_ref, g_ref, kk_ref, ka_ref, rk_ref, lng_ref, lnb_ref,
              o_ref, s_ref, rp_ref, yq_ref, *, nchunk):
    L = RW_CHUNK
    H2 = 2 * L

    @pl.when(pl.program_id(2) == 0)
    def _():
        s_ref[...] = jnp.zeros_like(s_ref)

    lane = lax.broadcasted_iota(jnp.int32, (1, LANES), 1)
    mask0 = (lane < RW_HEAD_DIM).astype(F32)
    mask1 = 1.0 - mask0
    ri = lax.broadcasted_iota(jnp.int32, (H2, H2), 0)
    ci = lax.broadcasted_iota(jnp.int32, (H2, H2), 1)
    same_head = (ri // L) == (ci // L)
    strict = (same_head & (ci < ri)).astype(F32)
    incl = (same_head & (ci <= ri)).astype(F32)
    diag16 = ((ri // 16) == (ci // 16)).astype(F32)
    eye = (ri == ci).astype(F32)
    head_ones = same_head.astype(BF16)
    tl = lax.broadcasted_iota(jnp.int32, (L, L), 0)
    sl = lax.broadcasted_iota(jnp.int32, (L, L), 1)
    tri_incl = (sl <= tl).astype(BF16)

    def stack(x):
        return jnp.concatenate([x * mask0, x * mask1], axis=0)

    k_k = kk_ref[...]
    k_a = ka_ref[...]
    r_k = rk_ref[...]

    chunks = range(nchunk)

    def each(fn, *lists):
        return [fn(*xs) for xs in zip(*lists)]

    def rows_of(x):
        return [x[c * L:(c + 1) * L] for c in chunks]

    r_all = r_ref[...]
    k_all = k_ref[...]
    v_all = v_ref[...]
    ag_all = a_ref[...]
    kk_all = k_all * k_k
    ss_all = _dot_split_const(kk_all * kk_all, head_ones)
    kk_all = kk_all / jnp.maximum(jnp.sqrt(ss_all), 1e-12)
    kmod_all = k_all * (1.0 + (ag_all - 1.0) * k_a)
    bv_all = kk_all * ag_all
    lw_c = rows_of(lw_ref[...])
    cl_c = each(lambda lw: _dot_const_split(tri_incl, lw), lw_c)
    last_c = each(lambda cl: cl[L - 1:L, :], cl_c)
    cl_all = jnp.concatenate(cl_c, axis=0)
    clp_all = cl_all - lw_ref[...]
    end_all = jnp.concatenate(each(lambda cl, la: la - cl, cl_c, last_c), axis=0)
    e_neg = jnp.exp(-cl_all)
    e_end = jnp.exp(end_all)
    at_c = rows_of(-kk_all * jnp.exp(clp_all))
    rt_c = rows_of(r_all * jnp.exp(cl_all))
    bt_c = rows_of(bv_all * e_neg)
    kt_c = rows_of(kmod_all * e_neg)
    be_c = rows_of(bv_all * e_end)
    ke_c = rows_of(kmod_all * e_end)
    v_s = each(stack, rows_of(v_all))
    at_s = each(stack, at_c)
    rt_s = each(stack, rt_c)
    gmat = each(lambda a_, r_, b_, k_: _bdot_nt(jnp.concatenate([a_, r_], axis=0),
                                                jnp.concatenate([stack(b_), stack(k_)], axis=0)),
                at_s, rt_s, bt_c, kt_c)
    a_ab = each(lambda gm: gm[:H2, :H2] * strict, gmat)
    a_ak = each(lambda gm: gm[:H2, H2:] * strict, gmat)
    a_rb = each(lambda gm: gm[H2:, :H2] * incl, gmat)
    a_rk = each(lambda gm: gm[H2:, H2:] * incl, gmat)
    dblk = each(lambda a_: a_ * diag16, a_ab)
    off = each(lambda a_, d_: a_ - d_, a_ab, dblk)
    d2 = each(_bdot, dblk, dblk)
    d4 = each(_bdot, d2, d2)
    d8 = each(_bdot, d4, d4)
    dinv = each(lambda d_, d2_: _bdot(eye + d_, eye + d2_), dblk, d2)
    dinv = each(lambda di, d4_: _bdot(di, eye + d4_), dinv, d4)
    dinv = each(lambda di, d8_: _bdot(di, eye + d8_), dinv, d8)
    e1 = each(_bdot, dinv, off)
    e2 = each(_bdot, e1, e1)
    minv = each(lambda e1_, e2_: _bdot(eye + e1_, eye + e2_), e1, e2)
    minv = each(_bdot, minv, dinv)
    x_ak = each(_bdot, a_ak, v_s)
    zu = each(lambda mi, a_, x_: _bdot(mi, jnp.concatenate([a_, x_], axis=1)), minv, at_s, x_ak)
    w2 = each(_bdot, a_rb, zu)
    rkv = each(_bdot, a_rk, v_s)
    pq = each(lambda b_, z_: _bdot(stack(b_).T, z_), be_c, zu)
    kv2 = each(lambda k_, v_: _bdot(stack(k_).T, v_), ke_c, v_s)
    for c in chunks:
        rp_ref[c] = jnp.concatenate([rt_s[c] + w2[c][:, :LANES],
                                     eye * jnp.exp(last_c[c]) + pq[c][:, :LANES]], axis=0)
        yq_ref[c] = jnp.concatenate([w2[c][:, LANES:] + rkv[c], pq[c][:, LANES:] + kv2[c]], axis=0)

    ys = []
    s = s_ref[...]
    for c in chunks:
        res = _bdot(rp_ref[c], s) + yq_ref[c]
        ys.append(res[:L] + res[L:H2])
        s = res[H2:]
    s_ref[...] = s
    y = jnp.concatenate(ys, axis=0)
    inv_n = 1.0 / RW_HEAD_DIM
    ym = _dot_split_const(y, head_ones) * inv_n
    yc = y - ym
    yv = _dot_split_const(yc * yc, head_ones) * inv_n
    yn = yc * lax.rsqrt(yv + RW_GN_EPS) * lng_ref[...] + lnb_ref[...]
    bonus = _dot_split_const(r_all * kmod_all * r_k, head_ones) * v_all
    o_ref[...] = ((yn + bonus) * g_ref[...]).astype(BF16)


def _wkv(r, k, v, lw, a, g, k_k, k_a, r_k, lnx_g, lnx_b, batch):
    n, d = r.shape
    t = n // batch
    nchunk = RW_CHUNKS_PER_STEP
    tb = RW_CHUNK * nchunk
    while t % tb:
        nchunk //= 2
        tb = RW_CHUNK * nchunk
    nt = t // tb
    row = pl.BlockSpec((tb, LANES), lambda b, hp, c: (b * nt + c, hp))
    vec = pl.BlockSpec((1, LANES), lambda b, hp, c: (0, hp))
    sq = pltpu.VMEM((nchunk, 2 * LANES, LANES), F32)
    vecs = [z.reshape(1, d) for z in (k_k, k_a, r_k, lnx_g, lnx_b)]
    return pl.pallas_call(
        functools.partial(_wkv_body, nchunk=nchunk),
        grid=(batch, d // LANES, nt),
        in_specs=[row] * 6 + [vec] * 5,
        out_specs=row,
        out_shape=jax.ShapeDtypeStruct((n, d), BF16),
        scratch_shapes=[pltpu.VMEM((LANES, LANES), F32), sq, sq],
        compiler_params=_params("parallel", "parallel", "arbitrary"),
        name="rwkv_chunked_scan",
    )(r, k, v, lw, a, g, *vecs)


def _rwkv_time_mix(x2, batch, mu, w_rkv, w0, w1, w2, a0, a1, a2, g1, g2, k_k, k_a, r_k, lnx_g, lnx_b, w_o):
    xr, xw, xk, xv, xa, xg = _rw_mix(x2, batch, mu)
    r = _matmul(xr, w_rkv[0].astype(BF16))
    k = _matmul(xk, w_rkv[1].astype(BF16))
    v = _matmul(xv, w_rkv[2].astype(BF16))
    lw, a, g = _rw_lowrank(xw, xa, xg, w0, w1, w2, a0, a1, a2, g1, g2)
    z = _wkv(r, k, v, lw, a, g, k_k, k_a, r_k.reshape(-1), lnx_g, lnx_b, batch)
    return _matmul(z, w_o.astype(BF16))


def _router_body(x_ref, w_ref, b_ref, idx_ref, gate_ref):
    logits = jnp.dot(x_ref[...], w_ref[...], preferred_element_type=F32) + b_ref[...]
    lane = lax.broadcasted_iota(jnp.int32, logits.shape, 1).astype(F32)
    cur = logits
    vals, idxs = [], []
    for _ in range(TOP_K):
        m = jnp.max(cur, axis=-1, keepdims=True)
        i = jnp.min(jnp.where(cur == m, lane, float(LANES)), axis=-1, keepdims=True)
        vals.append(m)
        idxs.append(i)
        cur = jnp.where(lane == i, -3e38, cur)
    es = [jnp.exp(vv - vals[0]) for vv in vals]
    den = es[0]
    for e in es[1:]:
        den = den + e
    idx_out = jnp.zeros(logits.shape, F32)
    gate_out = jnp.zeros(logits.shape, F32)
    for kk in range(TOP_K):
        idx_out = jnp.where(lane == kk, idxs[kk], idx_out)
        gate_out = jnp.where(lane == kk, es[kk] / den, gate_out)
    idx_ref[...] = idx_out.astype(jnp.int32)
    gate_ref[...] = gate_out


def _router(xb, router_w, router_b, tm=512):
    n, d = xb.shape
    e = router_w.shape[1]
    wp = jnp.pad(router_w, ((0, 0), (0, LANES - e))).astype(BF16)
    bp = jnp.concatenate([router_b.astype(F32), jnp.full((LANES - e,), NEG_BIG, F32)]).reshape(1, LANES)
    row = pl.BlockSpec((tm, LANES), lambda i: (i, 0))
    idx, gate = pl.pallas_call(
        _router_body,
        grid=(n // tm,),
        in_specs=[pl.BlockSpec((tm, d), lambda i: (i, 0)), pl.BlockSpec((d, LANES), lambda i: (0, 0)),
                  pl.BlockSpec((1, LANES), lambda i: (0, 0))],
        out_specs=[row, row],
        out_shape=[jax.ShapeDtypeStruct((n, LANES), jnp.int32), jax.ShapeDtypeStruct((n, LANES), F32)],
        compiler_params=_params("parallel"),
        name="moe_router",
    )(xb, wp, bp)
    return idx[:, :TOP_K], gate


def _moe_expert_body(be_ref, nu_ref, x_ref, wg_ref, wu_ref, bg_ref, bu_ref, wd_ref, bd_ref, o_ref, acc_ref):
    f = pl.program_id(1)
    half = pl.program_id(2)
    last_f = pl.num_programs(1) - 1
    tm = MOE_ROW_BLOCK
    used = 2 * pl.program_id(0) + half < nu_ref[0]
    rows = pl.ds(pl.multiple_of(half * tm, tm), tm)

    @pl.when(used)
    def _():
        x = x_ref[rows, :]
        gate = jnp.dot(x, wg_ref[...].astype(BF16), preferred_element_type=F32) + bg_ref[...]
        up = jnp.dot(x, wu_ref[...].astype(BF16), preferred_element_type=F32) + bu_ref[...]
        gate = jnp.minimum(gate, SWIGLU_LIMIT)
        up = jnp.clip(up, -SWIGLU_LIMIT, SWIGLU_LIMIT)
        act = (up + 1.0) * (gate * jax.nn.sigmoid(gate * SWIGLU_ALPHA))
        part = jnp.dot(act.astype(BF16), wd_ref[...].astype(BF16), preferred_element_type=F32)

        @pl.when(f == 0)
        def _():
            acc_ref[rows, :] = part + bd_ref[...]

        @pl.when(f > 0)
        def _():
            acc_ref[rows, :] += part

    @pl.when(jnp.logical_not(used) & (f == 0))
    def _():
        acc_ref[rows, :] = jnp.zeros((tm, acc_ref.shape[1]), F32)

    @pl.when((f == last_f) & (half == 1))
    def _():
        o_ref[...] = acc_ref[...].astype(o_ref.dtype)


def _moe_experts(xs, blk_e, n_used, layer, w_gu, b_gu, w_down, b_down):
    n_rows, d = xs.shape
    depth, ne, _, f2 = w_gu.shape
    fdim = f2 // 2
    tm, tf = MOE_ROW_BLOCK, MOE_F_TILE
    nf = fdim // tf
    n_pair = n_rows // (2 * tm)

    def expert(p, h, be):
        return be[2 * p + h]

    grid_spec = pltpu.PrefetchScalarGridSpec(
        num_scalar_prefetch=2,
        grid=(n_pair, nf, 2),
        in_specs=[
            pl.BlockSpec((2 * tm, d), lambda p, f, h, be, nu: (jnp.minimum(p, (nu[0] - 1) // 2), 0)),
            pl.BlockSpec((None, None, d, tf), lambda p, f, h, be, nu: (layer, expert(p, h, be), 0, f)),
            pl.BlockSpec((None, None, d, tf), lambda p, f, h, be, nu: (layer, expert(p, h, be), 0, nf + f)),
            pl.BlockSpec((None, None, 1, tf), lambda p, f, h, be, nu: (layer, expert(p, h, be), 0, f)),
            pl.BlockSpec((None, None, 1, tf), lambda p, f, h, be, nu: (layer, expert(p, h, be), 0, nf + f)),
            pl.BlockSpec((None, None, tf, d), lambda p, f, h, be, nu: (layer, expert(p, h, be), f, 0)),
            pl.BlockSpec((None, None, 1, d), lambda p, f, h, be, nu: (layer, expert(p, h, be), 0, 0)),
        ],
        out_specs=pl.BlockSpec((2 * tm, d), lambda p, f, h, be, nu: (p, 0)),
        scratch_shapes=[pltpu.VMEM((2 * tm, d), F32)],
    )
    bgu = b_gu.reshape(depth, ne, 1, f2)
    return pl.pallas_call(
        _moe_expert_body,
        grid_spec=grid_spec,
        out_shape=jax.ShapeDtypeStruct((n_rows, d), BF16),
        compiler_params=_params("arbitrary", "arbitrary", "arbitrary"),
        name="moe_experts",
    )(blk_e, n_used, xs, w_gu, w_gu, bgu, bgu, w_down, b_down.reshape(depth, ne, 1, d))


def _combine_ln_body(x_ref, y_ref, gate_ref, g_ref, b_ref, o_ref, ob_ref, *, alpha):
    ffn = y_ref[0].astype(F32) * gate_ref[:, 0:1]
    for kk in range(1, TOP_K):
        ffn = ffn + y_ref[kk].astype(F32) * gate_ref[:, kk:kk + 1]
    z = alpha * x_ref[...] + ffn
    mu = jnp.mean(z, -1, keepdims=True)
    zc = z - mu
    var = jnp.mean(zc * zc, -1, keepdims=True)
    y = zc * lax.rsqrt(var + LN_EPS) * g_ref[...] + b_ref[...]
    o_ref[...] = y
    ob_ref[...] = y.astype(BF16)


def _combine_ln(x, y4, gates, g, b, alpha, tm=256):
    n, d = x.shape
    row = pl.BlockSpec((tm, d), lambda i: (i, 0))
    vec = pl.BlockSpec((1, d), lambda i: (0, 0))
    return pl.pallas_call(
        functools.partial(_combine_ln_body, alpha=alpha),
        grid=(n // tm,),
        in_specs=[row, pl.BlockSpec((TOP_K, tm, d), lambda i: (0, i, 0)),
                  pl.BlockSpec((tm, LANES), lambda i: (i, 0)), vec, vec],
        out_specs=[row, row],
        out_shape=[jax.ShapeDtypeStruct((n, d), F32), jax.ShapeDtypeStruct((n, d), BF16)],
        compiler_params=_params("parallel"),
        name="moe_combine_layer_norm",
    )(x, y4, gates, g.reshape(1, d), b.reshape(1, d))


def _moe_ffn(xb, layer, router_w, router_b, w_gu, b_gu, w_down, b_down):
    n, d = xb.shape
    tm = MOE_ROW_BLOCK
    top_i, gates = _router(xb, router_w, router_b)
    flat_e = top_i.reshape(-1)
    onehot = (flat_e[:, None] == jnp.arange(N_EXPERTS, dtype=jnp.int32)[None, :]).astype(jnp.int32)
    csum = jnp.cumsum(onehot, axis=0)
    rank = jnp.take_along_axis(csum, flat_e[:, None], axis=1)[:, 0] - 1
    counts = csum[-1]
    padded = ((counts + tm - 1) // tm) * tm
    pad_end = jnp.cumsum(padded)
    pad_start = pad_end - padded
    dest = pad_start[flat_e] + rank
    n_rows = -(-(n * TOP_K + N_EXPERTS * tm) // (2 * tm)) * (2 * tm)
    n_blk = n_rows // tm
    blk_start = jnp.arange(n_blk, dtype=jnp.int32) * tm
    blk_e = jnp.minimum(jnp.sum((pad_end[None, :] <= blk_start[:, None]).astype(jnp.int32), axis=1),
                        N_EXPERTS - 1).astype(jnp.int32)
    n_used = (pad_end[-1] // tm).astype(jnp.int32).reshape(1)
    flat_tok = jnp.arange(n * TOP_K, dtype=jnp.int32) // TOP_K
    row_tok = jnp.zeros((n_rows,), jnp.int32).at[dest].set(flat_tok)
    xs = jnp.take(xb, row_tok, axis=0, mode='clip')
    ys = _moe_experts(xs, blk_e, n_used, layer, w_gu, b_gu, w_down, b_down)
    dest_kmajor = dest.reshape(n, TOP_K).T.reshape(-1)
    y4 = jnp.take(ys, dest_kmajor, axis=0, mode='clip').reshape(TOP_K, n, d)
    return y4, gates


def _cmp_mlp_body(x_ref, pe_ref, w1_ref, b1_ref, w2_ref, b2_ref, o_ref):
    h = _bdot(x_ref[...] + pe_ref[...], w1_ref[...]) + b1_ref[...]
    h = jax.nn.gelu(h)
    o_ref[...] = (_bdot(h, w2_ref[...]) + b2_ref[...]).astype(o_ref.dtype)


def _cmp_mlp(flat, pe, w1, b1, w2, b2, tm=256):
    m, kd = flat.shape
    hid = w1.shape[1]
    dk = w2.shape[1]
    full = lambda shp: pl.BlockSpec(shp, lambda i: (0, 0))
    return pl.pallas_call(
        _cmp_mlp_body,
        grid=(m // tm,),
        in_specs=[pl.BlockSpec((tm, kd), lambda i: (i, 0)), full((1, kd)), full((kd, hid)), full((1, hid)),
                  full((hid, dk)), full((1, dk))],
        out_specs=pl.BlockSpec((tm, dk), lambda i: (i, 0)),
        out_shape=jax.ShapeDtypeStruct((m, dk), BF16),
        compiler_params=_params("parallel"),
        name="nsa_compress_mlp",
    )(flat, pe.reshape(1, kd), w1.astype(BF16), b1.reshape(1, hid), w2.astype(BF16), b2.reshape(1, dk))


def _group_rows(q_ref, hpg):
    dk = NSA_HEAD_DIM
    return jnp.concatenate([q_ref[:, h * dk:(h + 1) * dk] for h in range(hpg)], axis=0)


def _softmax_rows(s, mask):
    s = jnp.where(mask, s, NEG_BIG)
    m = jnp.max(s, axis=-1, keepdims=True)
    e = jnp.where(mask, jnp.exp(s - m), 0.0)
    den = jnp.sum(e, axis=-1, keepdims=True)
    return e / jnp.where(den > 0, den, 1.0)


def _nsa_cmp_body(q_ref, kc_ref, vc_ref, ov_ref, o_ref, sel_ref, *, hpg, n_c, n_sel, scale):
    qb = pl.program_id(2)
    t0 = qb * Q_BLOCK
    ncp = kc_ref.shape[0]
    n_s = ov_ref.shape[1]
    q8 = _group_rows(q_ref, hpg)
    s = _bdot_nt(q8, kc_ref[...]) * scale
    rid = lax.broadcasted_iota(jnp.int32, s.shape, 0)
    cid = lax.broadcasted_iota(jnp.int32, s.shape, 1)
    t = t0 + (rid % Q_BLOCK)
    valid = (cid * CMP_STRIDE + (CMP_LEN - 1) <= t) & (cid < n_c)
    p = _softmax_rows(s, valid)
    pb = p.astype(BF16)
    o = jnp.dot(pb, vc_ref[...], preferred_element_type=F32)
    for h in range(hpg):
        o_ref[:, h * NSA_HEAD_DIM:(h + 1) * NSA_HEAD_DIM] = o[h * Q_BLOCK:(h + 1) * Q_BLOCK]
    imp_h = jnp.dot(pb, ov_ref[...], preferred_element_type=F32)
    imp = imp_h[:Q_BLOCK]
    for h in range(1, hpg):
        imp = imp + imp_h[h * Q_BLOCK:(h + 1) * Q_BLOCK]
    tq = t0 + lax.broadcasted_iota(jnp.int32, (Q_BLOCK, n_s), 0)
    sid_i = lax.broadcasted_iota(jnp.int32, (Q_BLOCK, n_s), 1)
    cur = tq // SEL_BLOCK
    forced = (sid_i == 0) | (sid_i == cur) | (sid_i == cur - 1)
    score = jnp.where(sid_i * SEL_BLOCK <= tq, jnp.where(forced, 1e30, imp), -1.0)
    sid = sid_i.astype(F32)
    sel = jnp.zeros((Q_BLOCK, n_s), F32)
    for _ in range(n_sel):
        m = jnp.max(score, axis=-1, keepdims=True)
        first = jnp.min(jnp.where(score == m, sid, float(n_s)), axis=-1, keepdims=True)
        hit = sid == first
        sel = jnp.where(hit, 1.0, sel)
        score = jnp.where(hit, -2.0, score)
    sel_ref[...] = sel.astype(BF16)


def _nsa_win_body(q_ref, *refs, hpg, nwb, scale):
    k_refs, v_refs, o_ref = refs[:nwb], refs[nwb:2 * nwb], refs[2 * nwb]
    qb = pl.program_id(2)
    t0 = qb * Q_BLOCK
    q8 = _group_rows(q_ref, hpg)
    kcat = jnp.concatenate([r[...] for r in k_refs], axis=0)
    vcat = jnp.concatenate([r[...] for r in v_refs], axis=0)
    s = _bdot_nt(q8, kcat) * scale
    rid = lax.broadcasted_iota(jnp.int32, s.shape, 0)
    cid = lax.broadcasted_iota(jnp.int32, s.shape, 1)
    t = t0 + (rid % Q_BLOCK)
    kpos = t0 - WINDOW + cid
    mask = (kpos <= t) & (kpos > t - WINDOW) & (kpos >= 0)
    p = _softmax_rows(s, mask)
    o = jnp.dot(p.astype(BF16), vcat, preferred_element_type=F32)
    for h in range(hpg):
        o_ref[:, h * NSA_HEAD_DIM:(h + 1) * NSA_HEAD_DIM] = o[h * Q_BLOCK:(h + 1) * Q_BLOCK]


def _nsa_sel_body(qb_ref, kb_ref, q_ref, k_ref, v_ref, sel_ref, ex_ref, o_ref, m_ref, acc_ref, *, hpg, scale):
    step = pl.program_id(2)
    qb = qb_ref[step]
    kb = kb_ref[step]
    t0 = qb * Q_BLOCK
    tk = k_ref.shape[0]
    dk = NSA_HEAD_DIM
    log2_scale = scale * math.log2(math.e)

    @pl.when(kb == 0)
    def _():
        m_ref[...] = jnp.full_like(m_ref, NEG_BIG)
        acc_ref[...] = jnp.zeros_like(acc_ref)

    picked = jnp.dot(sel_ref[...], ex_ref[...], preferred_element_type=F32)
    tq = t0 + lax.broadcasted_iota(jnp.int32, (Q_BLOCK, tk), 0)
    kpos = kb * tk + lax.broadcasted_iota(jnp.int32, (Q_BLOCK, tk), 1)
    bias = jnp.where((picked > 0.5) & (kpos <= tq), 0.0, NEG_BIG)
    cols = [slice(j * LANES, (j + 1) * LANES) for j in range(tk // LANES)]
    k_tile = k_ref[...]
    v_aug = jnp.concatenate([v_ref[...], jnp.ones((tk, LANES), BF16)], axis=1)
    hg = SEL_HEADS_PER_DOT
    for g0 in range(0, hpg, hg):
        qg = jnp.concatenate([q_ref[:, h * dk:(h + 1) * dk] for h in range(g0, g0 + hg)], axis=0)
        s_g = _bdot_nt(qg, k_tile) * log2_scale
        p_rows, alphas = [], []
        for hl in range(hg):
            rows = slice((g0 + hl) * Q_BLOCK, (g0 + hl + 1) * Q_BLOCK)
            s_h = [s_g[hl * Q_BLOCK:(hl + 1) * Q_BLOCK, cj] + bias[:, cj] for cj in cols]
            mx = s_h[0]
            for s_hj in s_h[1:]:
                mx = jnp.maximum(mx, s_hj)
            m_old = m_ref[rows, :]
            m_new = jnp.maximum(m_old, jnp.broadcast_to(jnp.max(mx, axis=-1, keepdims=True), m_old.shape))
            m_ref[rows, :] = m_new
            alphas.append(jnp.exp2(m_old - m_new))
            p_rows.append(jnp.concatenate([jnp.exp2(s_hj - m_new).astype(BF16) for s_hj in s_h], axis=1))
        pv = jnp.dot(jnp.concatenate(p_rows, axis=0), v_aug, preferred_element_type=F32)
        alpha = jnp.concatenate(alphas, axis=0)
        grows = slice(g0 * Q_BLOCK, (g0 + hg) * Q_BLOCK)
        acc_ref[grows, :dk] = alpha * acc_ref[grows, :dk] + pv[:, :dk]
        acc_ref[grows, dk:] = alpha * acc_ref[grows, dk:] + pv[:, dk:]

    @pl.when(kb == (t0 + Q_BLOCK - 1) // tk)
    def _():
        den = acc_ref[:, dk:]
        o = acc_ref[:, :dk] / jnp.where(den > 0, den, 1.0)
        for h in range(hpg):
            o_ref[:, h * dk:(h + 1) * dk] = o[h * Q_BLOCK:(h + 1) * Q_BLOCK]


def _nsa_attention(q, kvb, k_cmp, v_cmp, batch, n_c):
    n, hd = q.shape
    dk, g = NSA_HEAD_DIM, NSA_KV_GROUPS
    hpg = hd // dk // g
    t = n // batch
    nqb = t // Q_BLOCK
    n_s = t // SEL_BLOCK
    n_sel = min(N_SEL, n_s)
    ncp = k_cmp.shape[2]
    scale = dk ** -0.5
    gw = hpg * dk

    c_lo = np.arange(ncp) * CMP_STRIDE
    s_lo = np.arange(n_s) * SEL_BLOCK
    overlap = ((c_lo[:, None] < s_lo[None, :] + SEL_BLOCK) & (c_lo[:, None] + CMP_LEN > s_lo[None, :])
               & (np.arange(ncp)[:, None] < n_c))
    overlap = jnp.asarray(overlap, BF16)

    qspec = pl.BlockSpec((Q_BLOCK, gw), lambda b, gi, qb: (b * nqb + qb, gi))
    o_c, sel = pl.pallas_call(
        functools.partial(_nsa_cmp_body, hpg=hpg, n_c=n_c, n_sel=n_sel, scale=scale),
        grid=(batch, g, nqb),
        in_specs=[qspec,
                  pl.BlockSpec((None, None, ncp, dk), lambda b, gi, qb: (b, gi, 0, 0)),
                  pl.BlockSpec((None, None, ncp, dk), lambda b, gi, qb: (b, gi, 0, 0)),
                  pl.BlockSpec((ncp, n_s), lambda b, gi, qb: (0, 0))],
        out_specs=[qspec, pl.BlockSpec((None, None, Q_BLOCK, n_s), lambda b, gi, qb: (b, gi, qb, 0))],
        out_shape=[jax.ShapeDtypeStruct((n, hd), F32), jax.ShapeDtypeStruct((batch, g, t, n_s), BF16)],
        compiler_params=_params("parallel", "parallel", "parallel"),
        name="nsa_compressed_select",
    )(q, k_cmp, v_cmp, overlap)

    nwb = WINDOW // Q_BLOCK + 1
    kcol, vcol = 4 * g, 5 * g

    def kv_spec(col, j):
        return pl.BlockSpec((Q_BLOCK, dk),
                            lambda b, gi, qb: (b * nqb + jnp.maximum(qb - (nwb - 1) + j, 0), col + gi))

    o_w = pl.pallas_call(
        functools.partial(_nsa_win_body, hpg=hpg, nwb=nwb, scale=scale),
        grid=(batch, g, nqb),
        in_specs=[qspec] + [kv_spec(kcol, j) for j in range(nwb)] + [kv_spec(vcol, j) for j in range(nwb)],
        out_specs=qspec,
        out_shape=jax.ShapeDtypeStruct((n, hd), F32),
        compiler_params=_params("parallel", "parallel", "parallel"),
        name="nsa_window",
    )(q, *([kvb] * (2 * nwb)))

    tk = min(SEL_KEY_TILE, t)
    nkb = t // tk
    steps = [(qb, kb) for qb in range(nqb) for kb in range((qb * Q_BLOCK + Q_BLOCK - 1) // tk + 1)]
    qb_tab = jnp.asarray([s_[0] for s_ in steps], jnp.int32)
    kb_tab = jnp.asarray([s_[1] for s_ in steps], jnp.int32)
    expand = jnp.asarray(np.arange(n_s)[:, None] == (np.arange(t)[None, :] // SEL_BLOCK), BF16)
    kscol, vscol = 2 * g, 3 * g
    grid_spec = pltpu.PrefetchScalarGridSpec(
        num_scalar_prefetch=2,
        grid=(batch, g, len(steps)),
        in_specs=[
            pl.BlockSpec((Q_BLOCK, gw), lambda b, gi, s_, qt, kt: (b * nqb + qt[s_], gi)),
            pl.BlockSpec((tk, dk), lambda b, gi, s_, qt, kt: (b * nkb + kt[s_], kscol + gi)),
            pl.BlockSpec((tk, dk), lambda b, gi, s_, qt, kt: (b * nkb + kt[s_], vscol + gi)),
            pl.BlockSpec((None, None, Q_BLOCK, n_s), lambda b, gi, s_, qt, kt: (b, gi, qt[s_], 0)),
            pl.BlockSpec((n_s, tk), lambda b, gi, s_, qt, kt: (0, kt[s_])),
        ],
        out_specs=pl.BlockSpec((Q_BLOCK, gw), lambda b, gi, s_, qt, kt: (b * nqb + qt[s_], gi)),
        scratch_shapes=[pltpu.VMEM((hpg * Q_BLOCK, LANES), F32), pltpu.VMEM((hpg * Q_BLOCK, dk + LANES), F32)],
    )
    o_s = pl.pallas_call(
        functools.partial(_nsa_sel_body, hpg=hpg, scale=scale),
        grid_spec=grid_spec,
        out_shape=jax.ShapeDtypeStruct((n, hd), F32),
        compiler_params=_params("parallel", "parallel", "arbitrary"),
        name="nsa_selected",
    )(qb_tab, kb_tab, q, kvb, kvb, sel, expand)
    return o_c, o_s, o_w


def _nsa_shared_kv(xb, batch, w_kv, cmp_pe, cmp_w1, cmp_b1, cmp_w2, cmp_b2):
    n, d = xb.shape
    g, dk = NSA_KV_GROUPS, NSA_HEAD_DIM
    t = n // batch
    kv = _matmul(xb, w_kv.astype(BF16), tn=768)
    n_c = t // CMP_STRIDE - 1
    ncp = -(-n_c // LANES) * LANES
    rows = batch * n_c * g
    rows_p = -(-rows // 256) * 256
    outs = []
    for i in range(2):
        z = kv[:, i * g * dk:(i + 1) * g * dk].reshape(batch, t // CMP_STRIDE, CMP_STRIDE, g, dk)
        blk = jnp.concatenate([z[:, :-1], z[:, 1:]], axis=2)
        flat = blk.transpose(0, 1, 3, 2, 4).reshape(rows, CMP_LEN * dk)
        flat = jnp.pad(flat, ((0, rows_p - rows), (0, 0)))
        pe = jnp.broadcast_to(cmp_pe[i][:, None, :], (CMP_LEN, 1, dk)).reshape(CMP_LEN * dk)
        c = _cmp_mlp(flat, pe, cmp_w1[i], cmp_b1[i], cmp_w2[i], cmp_b2[i])[:rows]
        c = c.reshape(batch, n_c, g, dk).transpose(0, 2, 1, 3)
        outs.append(jnp.pad(c, ((0, 0), (0, 0), (0, ncp - n_c), (0, 0))))
    return kv.astype(BF16), outs[0], outs[1], n_c


def _nsa_layer(xb, batch, shared, w_in, b_gate, w_o):
    kvb, k_cmp, v_cmp, n_c = shared
    n, d = xb.shape
    hd = w_o.shape[0]
    nh = hd // NSA_HEAD_DIM
    q = _matmul(xb, w_in[:, :hd].astype(BF16), out_dtype=BF16)
    wg = jnp.pad(w_in[:, hd:], ((0, 0), (0, LANES - 3 * nh))).astype(BF16)
    bg = jnp.pad(b_gate, (0, LANES - 3 * nh))
    glog = _matmul(xb, wg, bias=bg)[:, :3 * nh]
    gates = jax.nn.sigmoid(glog).reshape(n, 3, nh, 1)
    o_c, o_s, o_w = _nsa_attention(q, kvb, k_cmp, v_cmp, batch, n_c)
    o = (gates[:, 0] * o_c.reshape(n, nh, -1) + gates[:, 1] * o_s.reshape(n, nh, -1)
         + gates[:, 2] * o_w.reshape(n, nh, -1))
    return _matmul(o.reshape(n, hd).astype(BF16), w_o.astype(BF16))


def kernel(x, ln_g, ln_b, rw_mu, rw_w_rkv, rw_w0, rw_w1, rw_w2, rw_a0, rw_a1, rw_a2, rw_g1, rw_g2, rw_k_k, rw_k_a, rw_r_k, rw_lnx_g, rw_lnx_b, rw_w_o, nsa_w_kv, nsa_cmp_pe, nsa_cmp_w1, nsa_cmp_b1, nsa_cmp_w2, nsa_cmp_b2, nsa_w_in, nsa_b_gate, nsa_w_o, moe_router_w, moe_router_b, moe_w_gu, moe_b_gu, moe_w_down, moe_b_down):
    batch, t, d = x.shape
    depth = ln_g.shape[0]
    n_a = rw_mu.shape[0]
    alpha = (2 * depth) ** 0.25
    h = x.reshape(batch * t, d)
    hb = None
    shared = None
    for layer in range(depth):
        if layer < n_a:
            i = layer
            mix = _rwkv_time_mix(h, batch, rw_mu[i], rw_w_rkv[i], rw_w0[i], rw_w1[i], rw_w2[i], rw_a0[i], rw_a1[i],
                                 rw_a2[i], rw_g1[i], rw_g2[i], rw_k_k[i], rw_k_a[i], rw_r_k[i], rw_lnx_g[i],
                                 rw_lnx_b[i], rw_w_o[i])
        else:
            if shared is None:
                if hb is None:
                    hb = h.astype(BF16)
                shared = _nsa_shared_kv(hb, batch, nsa_w_kv, nsa_cmp_pe, nsa_cmp_w1, nsa_cmp_b1, nsa_cmp_w2,
                                        nsa_cmp_b2)
            j = layer - n_a
            mix = _nsa_layer(hb, batch, shared, nsa_w_in[j], nsa_b_gate[j], nsa_w_o[j])
        h, hb = _add_ln(h, mix, ln_g[layer, 0], ln_b[layer, 0], alpha)
        y4, gates = _moe_ffn(hb, layer, moe_router_w[layer], moe_router_b[layer], moe_w_gu, moe_b_gu, moe_w_down,
                             moe_b_down)
        h, hb = _combine_ln(h, y4, gates, ln_g[layer, 1], ln_b[layer, 1], alpha)
    return h.reshape(batch, t, d)
```

```python
import functools
import math

import numpy as np
import jax
import jax.numpy as jnp
from jax import lax
from jax.experimental import pallas as pl
from jax.experimental.pallas import tpu as pltpu

F32 = jnp.float32
BF16 = jnp.bfloat16

V7X_VMEM_LIMIT_BYTES = 56 * 1024 * 1024
LANES = 128

LN_EPS = 1e-5
RW_HEAD_DIM = 64
RW_GN_EPS = 64e-5
RW_CHUNK = 64
RW_CHUNKS_PER_STEP = 8
NSA_HEAD_DIM = 128
NSA_KV_GROUPS = 2
CMP_STRIDE = 16
CMP_LEN = 32
SEL_BLOCK = 64
N_SEL = 16
WINDOW = 512
Q_BLOCK = 128
SEL_KEY_TILE = 1024
SEL_HEADS_PER_DOT = 2
CMP_HEADS_PER_DOT = 4
CMP_SEG_COLS = 256
N_EXPERTS = 32
TOP_K = 4
SWIGLU_LIMIT = 7.0
SWIGLU_ALPHA = 1.702
MOE_ROW_BLOCK = 512
MOE_F_TILE = 512
MOE_OUT_TILE = 512
NEG_BIG = -1e30


def _params(*sem):
    return pltpu.CompilerParams(dimension_semantics=sem, vmem_limit_bytes=V7X_VMEM_LIMIT_BYTES)


def _bdot(a, b):
    return jnp.dot(a.astype(BF16), b.astype(BF16), preferred_element_type=F32)


def _bdot_nt(a, b):
    return lax.dot_general(a.astype(BF16), b.astype(BF16), (((1,), (1,)), ((), ())),
                           preferred_element_type=F32)


def _dot_const_split(c, x):
    hi = x.astype(BF16)
    lo = (x - hi.astype(F32)).astype(BF16)
    return (jnp.dot(c, hi, preferred_element_type=F32) + jnp.dot(c, lo, preferred_element_type=F32))


def _dot_split_const(x, c):
    hi = x.astype(BF16)
    lo = (x - hi.astype(F32)).astype(BF16)
    return (jnp.dot(hi, c, preferred_element_type=F32) + jnp.dot(lo, c, preferred_element_type=F32))


def _mm_body(a_ref, w_ref, b_ref, o_ref, *, out_scale):
    acc = jnp.dot(a_ref[...], w_ref[...], preferred_element_type=F32) + b_ref[...]
    if out_scale is not None:
        acc = acc * out_scale
    o_ref[...] = acc.astype(o_ref.dtype)


def _matmul(a, w, bias=None, out_dtype=F32, out_scale=None, tm=512, tn=1024):
    m, k = a.shape
    n = w.shape[1]
    tm = min(tm, m)
    tn = min(tn, n)
    assert m % tm == 0 and n % tn == 0, (m, n, tm, tn)
    if bias is None:
        bias = jnp.zeros((1, n), F32)
    return pl.pallas_call(
        functools.partial(_mm_body, out_scale=out_scale),
        grid=(n // tn, m // tm),
        in_specs=[pl.BlockSpec((tm, k), lambda j, i: (i, 0)),
                  pl.BlockSpec((k, tn), lambda j, i: (0, j)),
                  pl.BlockSpec((1, tn), lambda j, i: (0, j))],
        out_specs=pl.BlockSpec((tm, tn), lambda j, i: (i, j)),
        out_shape=jax.ShapeDtypeStruct((m, n), out_dtype),
        compiler_params=_params("parallel", "parallel"),
        name="dense_matmul",
    )(a, w, bias.reshape(1, n).astype(F32))


def _add_ln_body(x_ref, m_ref, g_ref, b_ref, o_ref, ob_ref, *, alpha):
    z = alpha * x_ref[...] + m_ref[...]
    mu = jnp.mean(z, -1, keepdims=True)
    zc = z - mu
    var = jnp.mean(zc * zc, -1, keepdims=True)
    y = zc * lax.rsqrt(var + LN_EPS) * g_ref[...] + b_ref[...]
    o_ref[...] = y
    ob_ref[...] = y.astype(BF16)


def _add_ln(x, mix, g, b, alpha, tm=256):
    n, d = x.shape
    row = pl.BlockSpec((tm, d), lambda i: (i, 0))
    vec = pl.BlockSpec((1, d), lambda i: (0, 0))
    return pl.pallas_call(
        functools.partial(_add_ln_body, alpha=alpha),
        grid=(n // tm,),
        in_specs=[row, row, vec, vec],
        out_specs=[row, row],
        out_shape=[jax.ShapeDtypeStruct((n, d), F32), jax.ShapeDtypeStruct((n, d), BF16)],
        compiler_params=_params("parallel"),
        name="add_layer_norm",
    )(x, mix, g.reshape(1, d), b.reshape(1, d))


def _rw_mix_body(x_ref, last_ref, mu_ref, *o_refs):
    x = x_ref[...]
    prev = pltpu.roll(x, shift=1, axis=0)
    row = lax.broadcasted_iota(jnp.int32, x.shape, 0)
    prev = jnp.where(row == 0, last_ref[0], prev)
    xx = prev - x
    for i, o_ref in enumerate(o_refs):
        o_ref[...] = (x + xx * mu_ref[i:i + 1, :]).astype(BF16)


def _rw_mix(x2, batch, mu, tm=256):
    n, d = x2.shape
    t = n // batch
    nt = t // tm
    last = x2.reshape(batch, nt, tm, d)[:, :, tm - 1, :]
    last = jnp.concatenate([jnp.zeros((batch, 1, d), F32), last[:, :-1]], axis=1).reshape(batch * nt, 1, d)
    row = pl.BlockSpec((tm, d), lambda i: (i, 0))
    return pl.pallas_call(
        _rw_mix_body,
        grid=(n // tm,),
        in_specs=[row, pl.BlockSpec((1, 1, d), lambda i: (i, 0, 0)), pl.BlockSpec((6, d), lambda i: (0, 0))],
        out_specs=[row] * 6,
        out_shape=[jax.ShapeDtypeStruct((n, d), BF16)] * 6,
        compiler_params=_params("parallel"),
        name="rwkv_token_shift",
    )(x2, last, mu)


def _rw_lowrank_body(xw_ref, xa_ref, xg_ref, w1_ref, w2_ref, w0_ref, a1_ref, a2_ref, a0_ref, g1_ref, g2_ref,
                     lw_ref, a_ref, g_ref):
    z = w0_ref[...] + _bdot(jnp.tanh(jnp.dot(xw_ref[...], w1_ref[...], preferred_element_type=F32)), w2_ref[...])
    w_log = jnp.minimum(z, 0.0) - jnp.log(1.0 + jnp.exp(-jnp.abs(z))) - 0.5
    lw_ref[...] = -jnp.exp(w_log)
    za = a0_ref[...] + _bdot(jnp.dot(xa_ref[...], a1_ref[...], preferred_element_type=F32), a2_ref[...])
    a_ref[...] = jax.nn.sigmoid(za)
    hg = jax.nn.sigmoid(jnp.dot(xg_ref[...], g1_ref[...], preferred_element_type=F32))
    g_ref[...] = _bdot(hg, g2_ref[...])


def _pad_rank(w_in, w_out):
    r = w_in.shape[1]
    rp = -(-r // LANES) * LANES
    return (jnp.pad(w_in, ((0, 0), (0, rp - r))).astype(BF16), jnp.pad(w_out, ((0, rp - r), (0, 0))).astype(BF16))


def _rw_lowrank(xw, xa, xg, w0, w1, w2, a0, a1, a2, g1, g2, tm=256):
    n, d = xw.shape
    w1p, w2p = _pad_rank(w1, w2)
    a1p, a2p = _pad_rank(a1, a2)
    g1p, g2p = _pad_rank(g1, g2)
    row = pl.BlockSpec((tm, d), lambda i: (i, 0))
    full = lambda arr: pl.BlockSpec(arr.shape, lambda i: (0, 0))
    w0r, a0r = w0.reshape(1, d), a0.reshape(1, d)
    return pl.pallas_call(
        _rw_lowrank_body,
        grid=(n // tm,),
        in_specs=[row, row, row, full(w1p), full(w2p), full(w0r), full(a1p), full(a2p), full(a0r), full(g1p), full(g2p)],
        out_specs=[row] * 3,
        out_shape=[jax.ShapeDtypeStruct((n, d), F32)] * 3,
        compiler_params=_params("parallel"),
        name="rwkv_lowrank",
    )(xw, xa, xg, w1p, w2p, w0r, a1p, a2p, a0r, g1p, g2p)


def _wkv_body(r_ref, k_ref, v_ref, lw_ref, a_ref, g_ref, kk_ref, ka_ref, rk_ref, lng_ref, lnb_ref,
              o_ref, s_ref, rp_ref, yq_ref, *, nchunk):
    L = RW_CHUNK
    H2 = 2 * L

    @pl.when(pl.program_id(2) == 0)
    def _():
        s_ref[...] = jnp.zeros_like(s_ref)

    lane = lax.broadcasted_iota(jnp.int32, (1, LANES), 1)
    mask0 = (lane < RW_HEAD_DIM).astype(F32)
    mask1 = 1.0 - mask0
    ri = lax.broadcasted_iota(jnp.int32, (H2, H2), 0)
    ci = lax.broadcasted_iota(jnp.int32, (H2, H2), 1)
    same_head = (ri // L) == (ci // L)
    strict = (same_head & (ci < ri)).astype(F32)
    incl = (same_head & (ci <= ri)).astype(F32)
    diag16 = ((ri // 16) == (ci // 16)).astype(F32)
    eye = (ri == ci).astype(F32)
    head_ones = same_head.astype(BF16)
    tl = lax.broadcasted_iota(jnp.int32, (L, L), 0)
    sl = lax.broadcasted_iota(jnp.int32, (L, L), 1)
    tri_incl = (sl <= tl).astype(BF16)

    def stack(x):
        return jnp.concatenate([x * mask0, x * mask1], axis=0)

    k_k = kk_ref[...]
    k_a = ka_ref[...]
    r_k = rk_ref[...]

    chunks = range(nchunk)

    def each(fn, *lists):
        return [fn(*xs) for xs in zip(*lists)]

    def rows_of(x):
        return [x[c * L:(c + 1) * L] for c in chunks]

    r_all = r_ref[...]
    k_all = k_ref[...]
    v_all = v_ref[...]
    ag_all = a_ref[...]
    kk_all = k_all * k_k
    ss_all = _dot_split_const(kk_all * kk_all, head_ones)
    kk_all = kk_all / jnp.maximum(jnp.sqrt(ss_all), 1e-12)
    kmod_all = k_all * (1.0 + (ag_all - 1.0) * k_a)
    bv_all = kk_all * ag_all
    lw_c = rows_of(lw_ref[...])
    cl_c = each(lambda lw: _dot_const_split(tri_incl, lw), lw_c)
    last_c = each(lambda cl: cl[L - 1:L, :], cl_c)
    cl_all = jnp.concatenate(cl_c, axis=0)
    clp_all = cl_all - lw_ref[...]
    end_all = jnp.concatenate(each(lambda cl, la: la - cl, cl_c, last_c), axis=0)
    e_neg = jnp.exp(-cl_all)
    e_end = jnp.exp(end_all)
    at_c = rows_of(-kk_all * jnp.exp(clp_all))
    rt_c = rows_of(r_all * jnp.exp(cl_all))
    bt_c = rows_of(bv_all * e_neg)
    kt_c = rows_of(kmod_all * e_neg)
    be_c = rows_of(bv_all * e_end)
    ke_c = rows_of(kmod_all * e_end)
    v_s = each(stack, rows_of(v_all))
    at_s = each(stack, at_c)
    rt_s = each(stack, rt_c)
    gmat = each(lambda a_, r_, b_, k_: _bdot_nt(jnp.concatenate([a_, r_], axis=0),
                                                jnp.concatenate([stack(b_), stack(k_)], axis=0)),
                at_s, rt_s, bt_c, kt_c)
    a_ab = each(lambda gm: gm[:H2, :H2] * strict, gmat)
    a_ak = each(lambda gm: gm[:H2, H2:] * strict, gmat)
    a_rb = each(lambda gm: gm[H2:, :H2] * incl, gmat)
    a_rk = each(lambda gm: gm[H2:, H2:] * incl, gmat)
    dblk = each(lambda a_: a_ * diag16, a_ab)
    off = each(lambda a_, d_: a_ - d_, a_ab, dblk)
    d2 = each(_bdot, dblk, dblk)
    d4 = each(_bdot, d2, d2)
    d8 = each(_bdot, d4, d4)
    dinv = each(lambda d_, d2_: _bdot(eye + d_, eye + d2_), dblk, d2)
    dinv = each(lambda di, d4_: _bdot(di, eye + d4_), dinv, d4)
    dinv = each(lambda di, d8_: _bdot(di, eye + d8_), dinv, d8)
    e1 = each(_bdot, dinv, off)
    e2 = each(_bdot, e1, e1)
    minv = each(lambda e1_, e2_: _bdot(eye + e1_, eye + e2_), e1, e2)
    minv = each(_bdot, minv, dinv)
    x_ak = each(_bdot, a_ak, v_s)
    zu = each(lambda mi, a_, x_: _bdot(mi, jnp.concatenate([a_, x_], axis=1)), minv, at_s, x_ak)
    w2 = each(_bdot, a_rb, zu)
    rkv = each(_bdot, a_rk, v_s)
    pq = each(lambda b_, z_: _bdot(stack(b_).T, z_), be_c, zu)
    kv2 = each(lambda k_, v_: _bdot(stack(k_).T, v_), ke_c, v_s)
    for c in chunks:
        rp_ref[c] = jnp.concatenate([rt_s[c] + w2[c][:, :LANES],
                                     eye * jnp.exp(last_c[c]) + pq[c][:, :LANES]], axis=0)
        yq_ref[c] = jnp.concatenate([w2[c][:, LANES:] + rkv[c], pq[c][:, LANES:] + kv2[c]], axis=0)

    ys = []
    s = s_ref[...]
    for c in chunks:
        res = _bdot(rp_ref[c], s) + yq_ref[c]
        ys.append(res[:L] + res[L:H2])
        s = res[H2:]
    s_ref[...] = s
    y = jnp.concatenate(ys, axis=0)
    inv_n = 1.0 / RW_HEAD_DIM
    ym = _dot_split_const(y, head_ones) * inv_n
    yc = y - ym
    yv = _dot_split_const(yc * yc, head_ones) * inv_n
    yn = yc * lax.rsqrt(yv + RW_GN_EPS) * lng_ref[...] + lnb_ref[...]
    bonus = _dot_split_const(r_all * kmod_all * r_k, head_ones) * v_all
    o_ref[...] = ((yn + bonus) * g_ref[...]).astype(BF16)


def _wkv(r, k, v, lw, a, g, k_k, k_a, r_k, lnx_g, lnx_b, batch):
    n, d = r.shape
    t = n // batch
    nchunk = RW_CHUNKS_PER_STEP
    tb = RW_CHUNK * nchunk
    while t % tb:
        nchunk //= 2
        tb = RW_CHUNK * nchunk
    nt = t // tb
    row = pl.BlockSpec((tb, LANES), lambda b, hp, c: (b * nt + c, hp))
    vec = pl.BlockSpec((1, LANES), lambda b, hp, c: (0, hp))
    sq = pltpu.VMEM((nchunk, 2 * LANES, LANES), F32)
    vecs = [z.reshape(1, d) for z in (k_k, k_a, r_k, lnx_g, lnx_b)]
    return pl.pallas_call(
        functools.partial(_wkv_body, nchunk=nchunk),
        grid=(batch, d // LANES, nt),
        in_specs=[row] * 6 + [vec] * 5,
        out_specs=row,
        out_shape=jax.ShapeDtypeStruct((n, d), BF16),
        scratch_shapes=[pltpu.VMEM((LANES, LANES), F32), sq, sq],
        compiler_params=_params("parallel", "parallel", "arbitrary"),
        name="rwkv_chunked_scan",
    )(r, k, v, lw, a, g, *vecs)


def _rwkv_time_mix(x2, batch, mu, w_rkv, w0, w1, w2, a0, a1, a2, g1, g2, k_k, k_a, r_k, lnx_g, lnx_b, w_o):
    xr, xw, xk, xv, xa, xg = _rw_mix(x2, batch, mu)
    r = _matmul(xr, w_rkv[0].astype(BF16))
    k = _matmul(xk, w_rkv[1].astype(BF16))
    v = _matmul(xv, w_rkv[2].astype(BF16))
    lw, a, g = _rw_lowrank(xw, xa, xg, w0, w1, w2, a0, a1, a2, g1, g2)
    z = _wkv(r, k, v, lw, a, g, k_k, k_a, r_k.reshape(-1), lnx_g, lnx_b, batch)
    return _matmul(z, w_o.astype(BF16))


def _router_body(x_ref, w_ref, b_ref, idx_ref, gate_ref):
    logits = jnp.dot(x_ref[...], w_ref[...], preferred_element_type=F32) + b_ref[...]
    lane = lax.broadcasted_iota(jnp.int32, logits.shape, 1).astype(F32)
    cur = logits
    vals, idxs = [], []
    for _ in range(TOP_K):
        m = jnp.max(cur, axis=-1, keepdims=True)
        i = jnp.min(jnp.where(cur == m, lane, float(LANES)), axis=-1, keepdims=True)
        vals.append(m)
        idxs.append(i)
        cur = jnp.where(lane == i, -3e38, cur)
    es = [jnp.exp(vv - vals[0]) for vv in vals]
    den = es[0]
    for e in es[1:]:
        den = den + e
    idx_out = jnp.zeros(logits.shape, F32)
    gate_out = jnp.zeros(logits.shape, F32)
    for kk in range(TOP_K):
        idx_out = jnp.where(lane == kk, idxs[kk], idx_out)
        gate_out = jnp.where(lane == kk, es[kk] / den, gate_out)
    idx_ref[...] = idx_out.astype(jnp.int32)
    gate_ref[...] = gate_out


def _router(xb, router_w, router_b, tm=512):
    n, d = xb.shape
    e = router_w.shape[1]
    wp = jnp.pad(router_w, ((0, 0), (0, LANES - e))).astype(BF16)
    bp = jnp.concatenate([router_b.astype(F32), jnp.full((LANES - e,), NEG_BIG, F32)]).reshape(1, LANES)
    row = pl.BlockSpec((tm, LANES), lambda i: (i, 0))
    idx, gate = pl.pallas_call(
        _router_body,
        grid=(n // tm,),
        in_specs=[pl.BlockSpec((tm, d), lambda i: (i, 0)), pl.BlockSpec((d, LANES), lambda i: (0, 0)),
                  pl.BlockSpec((1, LANES), lambda i: (0, 0))],
        out_specs=[row, row],
        out_shape=[jax.ShapeDtypeStruct((n, LANES), jnp.int32), jax.ShapeDtypeStruct((n, LANES), F32)],
        compiler_params=_params("parallel"),
        name="moe_router",
    )(xb, wp, bp)
    return idx[:, :TOP_K], gate


def _moe_expert_body(be_ref, nu_ref, x_ref, wg_ref, wu_ref, bg_ref, bu_ref, wd_ref, bd_ref, o_ref,
                     act_ref, wgu_ref, wdb_ref, *, nf):
    p = pl.program_id(0)
    s = pl.program_id(1)
    half = pl.program_id(2)
    tm = MOE_ROW_BLOCK
    blk = 2 * p + half
    used = blk < nu_ref[0]
    rows = pl.ds(pl.multiple_of(half * tm, tm), tm)
    recast = (half == 0) | (be_ref[blk] != be_ref[2 * p])

    @pl.when(s < nf)
    def _():
        @pl.when(recast)
        def _():
            wgu_ref[0] = wg_ref[...].astype(BF16)
            wgu_ref[1] = wu_ref[...].astype(BF16)

        @pl.when(used)
        def _():
            x = x_ref[rows, :]
            gate = jnp.dot(x, wgu_ref[0], preferred_element_type=F32) + bg_ref[...]
            up = jnp.dot(x, wgu_ref[1], preferred_element_type=F32) + bu_ref[...]
            gate = jnp.minimum(gate, SWIGLU_LIMIT)
            up = jnp.clip(up, -SWIGLU_LIMIT, SWIGLU_LIMIT)
            act = (up + 1.0) * (gate * jax.nn.sigmoid(gate * SWIGLU_ALPHA))
            act_ref[jnp.minimum(s, nf - 1), rows, :] = act.astype(BF16)

    @pl.when(s >= nf)
    def _():
        @pl.when(recast)
        def _():
            wdb_ref[...] = wd_ref[...].astype(BF16)

        @pl.when(used)
        def _():
            act = jnp.concatenate([act_ref[f, rows, :] for f in range(nf)], axis=1)
            y = jnp.dot(act, wdb_ref[...], preferred_element_type=F32) + bd_ref[...]
            o_ref[rows, :] = y.astype(o_ref.dtype)

        @pl.when(jnp.logical_not(used))
        def _():
            o_ref[rows, :] = jnp.zeros((tm, o_ref.shape[1]), o_ref.dtype)


def _moe_experts(xs, blk_e, n_used, layer, w_gu, b_gu, w_down, b_down):
    n_rows, d = xs.shape
    depth, ne, _, f2 = w_gu.shape
    fdim = f2 // 2
    tm, tf = MOE_ROW_BLOCK, MOE_F_TILE
    nf = fdim // tf
    n_pair = n_rows // (2 * tm)

    def expert(p, h, be):
        return be[2 * p + h]

    tn = MOE_OUT_TILE
    nn = d // tn

    def fidx(s):
        return jnp.minimum(s, nf - 1)

    def nidx(s):
        return jnp.maximum(s - nf, 0)

    grid_spec = pltpu.PrefetchScalarGridSpec(
        num_scalar_prefetch=2,
        grid=(n_pair, nf + nn, 2),
        in_specs=[
            pl.BlockSpec((2 * tm, d), lambda p, s, h, be, nu: (jnp.minimum(p, (nu[0] - 1) // 2), 0)),
            pl.BlockSpec((None, None, d, tf), lambda p, s, h, be, nu: (layer, expert(p, h, be), 0, fidx(s))),
            pl.BlockSpec((None, None, d, tf), lambda p, s, h, be, nu: (layer, expert(p, h, be), 0, nf + fidx(s))),
            pl.BlockSpec((None, None, 1, tf), lambda p, s, h, be, nu: (layer, expert(p, h, be), 0, fidx(s))),
            pl.BlockSpec((None, None, 1, tf), lambda p, s, h, be, nu: (layer, expert(p, h, be), 0, nf + fidx(s))),
            pl.BlockSpec((None, None, fdim, tn), lambda p, s, h, be, nu: (layer, expert(p, h, be), 0, nidx(s))),
            pl.BlockSpec((None, None, 1, tn), lambda p, s, h, be, nu: (layer, expert(p, h, be), 0, nidx(s))),
        ],
        out_specs=pl.BlockSpec((2 * tm, tn), lambda p, s, h, be, nu: (p, nidx(s))),
        scratch_shapes=[pltpu.VMEM((nf, 2 * tm, tf), BF16), pltpu.VMEM((2, d, tf), BF16),
                        pltpu.VMEM((fdim, tn), BF16)],
    )
    bgu = b_gu.reshape(depth, ne, 1, f2)
    return pl.pallas_call(
        functools.partial(_moe_expert_body, nf=nf),
        grid_spec=grid_spec,
        out_shape=jax.ShapeDtypeStruct((n_rows, d), BF16),
        compiler_params=_params("arbitrary", "arbitrary", "arbitrary"),
        name="moe_experts",
    )(blk_e, n_used, xs, w_gu, w_gu, bgu, bgu, w_down, b_down.reshape(depth, ne, 1, d))


def _combine_ln_body(x_ref, y_ref, gate_ref, g_ref, b_ref, o_ref, ob_ref, *, alpha):
    ffn = y_ref[0].astype(F32) * gate_ref[:, 0:1]
    for kk in range(1, TOP_K):
        ffn = ffn + y_ref[kk].astype(F32) * gate_ref[:, kk:kk + 1]
    z = alpha * x_ref[...] + ffn
    mu = jnp.mean(z, -1, keepdims=True)
    zc = z - mu
    var = jnp.mean(zc * zc, -1, keepdims=True)
    y = zc * lax.rsqrt(var + LN_EPS) * g_ref[...] + b_ref[...]
    o_ref[...] = y
    ob_ref[...] = y.astype(BF16)


def _combine_ln(x, y4, gates, g, b, alpha, tm=256):
    n, d = x.shape
    row = pl.BlockSpec((tm, d), lambda i: (i, 0))
    vec = pl.BlockSpec((1, d), lambda i: (0, 0))
    return pl.pallas_call(
        functools.partial(_combine_ln_body, alpha=alpha),
        grid=(n // tm,),
        in_specs=[row, pl.BlockSpec((TOP_K, tm, d), lambda i: (0, i, 0)),
                  pl.BlockSpec((tm, LANES), lambda i: (i, 0)), vec, vec],
        out_specs=[row, row],
        out_shape=[jax.ShapeDtypeStruct((n, d), F32), jax.ShapeDtypeStruct((n, d), BF16)],
        compiler_params=_params("parallel"),
        name="moe_combine_layer_norm",
    )(x, y4, gates, g.reshape(1, d), b.reshape(1, d))


def _moe_ffn(xb, layer, router_w, router_b, w_gu, b_gu, w_down, b_down):
    n, d = xb.shape
    tm = MOE_ROW_BLOCK
    top_i, gates = _router(xb, router_w, router_b)
    flat_e = top_i.reshape(-1)
    onehot = (flat_e[:, None] == jnp.arange(N_EXPERTS, dtype=jnp.int32)[None, :]).astype(jnp.int32)
    csum = jnp.cumsum(onehot, axis=0)
    rank = jnp.take_along_axis(csum, flat_e[:, None], axis=1)[:, 0] - 1
    counts = csum[-1]
    padded = ((counts + tm - 1) // tm) * tm
    pad_end = jnp.cumsum(padded)
    pad_start = pad_end - padded
    dest = pad_start[flat_e] + rank
    n_rows = -(-(n * TOP_K + N_EXPERTS * tm) // (2 * tm)) * (2 * tm)
    n_blk = n_rows // tm
    blk_start = jnp.arange(n_blk, dtype=jnp.int32) * tm
    blk_e = jnp.minimum(jnp.sum((pad_end[None, :] <= blk_start[:, None]).astype(jnp.int32), axis=1),
                        N_EXPERTS - 1).astype(jnp.int32)
    n_used = (pad_end[-1] // tm).astype(jnp.int32).reshape(1)
    flat_tok = jnp.arange(n * TOP_K, dtype=jnp.int32) // TOP_K
    row_tok = (jnp.arange(n_rows, dtype=jnp.int32) % n).at[dest].set(flat_tok)
    xs = jnp.take(xb, row_tok, axis=0, mode='clip')
    ys = _moe_experts(xs, blk_e, n_used, layer, w_gu, b_gu, w_down, b_down)
    dest_kmajor = dest.reshape(n, TOP_K).T.reshape(-1)
    y4 = jnp.take(ys, dest_kmajor, axis=0, mode='clip').reshape(TOP_K, n, d)
    return y4, gates


def _cmp_mlp_body(x_ref, pe_ref, w1_ref, b1_ref, w2_ref, b2_ref, o_ref):
    h = _bdot(x_ref[...] + pe_ref[...], w1_ref[...]) + b1_ref[...]
    h = jax.nn.gelu(h)
    o_ref[...] = (_bdot(h, w2_ref[...]) + b2_ref[...]).astype(o_ref.dtype)


def _cmp_mlp(flat, pe, w1, b1, w2, b2, tm=256):
    m, kd = flat.shape
    hid = w1.shape[1]
    dk = w2.shape[1]
    full = lambda shp: pl.BlockSpec(shp, lambda i: (0, 0))
    return pl.pallas_call(
        _cmp_mlp_body,
        grid=(m // tm,),
        in_specs=[pl.BlockSpec((tm, kd), lambda i: (i, 0)), full((1, kd)), full((kd, hid)), full((1, hid)),
                  full((hid, dk)), full((1, dk))],
        out_specs=pl.BlockSpec((tm, dk), lambda i: (i, 0)),
        out_shape=jax.ShapeDtypeStruct((m, dk), BF16),
        compiler_params=_params("parallel"),
        name="nsa_compress_mlp",
    )(flat, pe.reshape(1, kd), w1.astype(BF16), b1.reshape(1, hid), w2.astype(BF16), b2.reshape(1, dk))


def _group_rows(q_ref, hpg):
    dk = NSA_HEAD_DIM
    return jnp.concatenate([q_ref[:, h * dk:(h + 1) * dk] for h in range(hpg)], axis=0)


def _softmax_rows(s, mask):
    s = jnp.where(mask, s, NEG_BIG)
    m = jnp.max(s, axis=-1, keepdims=True)
    e = jnp.where(mask, jnp.exp2(s - m), 0.0)
    den = jnp.sum(e, axis=-1, keepdims=True)
    return e / jnp.where(den > 0, den, 1.0)


def _exp2_rows(s_rows, bias, cols):
    s_h = [s_rows[:, cj] + bias[:, cj] for cj in cols]
    mx = s_h[0]
    for s_hj in s_h[1:]:
        mx = jnp.maximum(mx, s_hj)
    m = jnp.broadcast_to(jnp.max(mx, axis=-1, keepdims=True), mx.shape)
    return jnp.concatenate([jnp.exp2(s_hj - m).astype(BF16) for s_hj in s_h], axis=1)


def _nsa_cmp_body(q_ref, kc_ref, vc_ref, ov_ref, *rest, hpg, n_c, n_sel, qb0):
    o_ref, sel_ref = rest[-2:]
    qb = pl.program_id(2) + qb0
    t0 = qb * Q_BLOCK
    ncols = kc_ref.shape[0]
    n_s = ov_ref.shape[1]
    dk = NSA_HEAD_DIM
    tq_c = t0 + lax.broadcasted_iota(jnp.int32, (Q_BLOCK, ncols), 0)
    cid = lax.broadcasted_iota(jnp.int32, (Q_BLOCK, ncols), 1)
    bias = jnp.where((cid * CMP_STRIDE + (CMP_LEN - 1) <= tq_c) & (cid < n_c), 0.0, NEG_BIG)
    t_row = t0 + lax.broadcasted_iota(jnp.int32, (Q_BLOCK, LANES), 0)
    row_live = t_row >= CMP_LEN - 1
    cols = [slice(j * LANES, (j + 1) * LANES) for j in range(ncols // LANES)]
    k_tile = kc_ref[...]
    v_aug = jnp.concatenate([vc_ref[...], jnp.ones((ncols, LANES), BF16)], axis=1)
    ov = ov_ref[...]
    hg = min(CMP_HEADS_PER_DOT, hpg)
    imp = jnp.zeros((Q_BLOCK, n_s), F32)
    for g0 in range(0, hpg, hg):
        qg = jnp.concatenate([q_ref[:, h * dk:(h + 1) * dk] for h in range(g0, g0 + hg)], axis=0)
        s_g = _bdot_nt(qg, k_tile)
        e = jnp.concatenate([_exp2_rows(s_g[hl * Q_BLOCK:(hl + 1) * Q_BLOCK], bias, cols) for hl in range(hg)], axis=0)
        od = jnp.dot(e, v_aug, preferred_element_type=F32)
        ih = jnp.dot(e, ov, preferred_element_type=F32)
        for hl in range(hg):
            rows = slice(hl * Q_BLOCK, (hl + 1) * Q_BLOCK)
            inv = jnp.where(row_live, 1.0 / od[rows, dk:], 0.0)
            o_ref[:, (g0 + hl) * dk:(g0 + hl + 1) * dk] = (od[rows, :dk] * inv).astype(o_ref.dtype)
            inv_s = inv[:, :n_s] if n_s <= LANES else jnp.concatenate([inv] * (n_s // LANES), axis=1)
            imp = imp + ih[rows] * inv_s
    imp_t = imp.T
    tq = t0 + lax.broadcasted_iota(jnp.int32, (n_s, Q_BLOCK), 1)
    sid_i = lax.broadcasted_iota(jnp.int32, (n_s, Q_BLOCK), 0)
    cur = tq // SEL_BLOCK
    forced = (sid_i == 0) | (sid_i == cur) | (sid_i == cur - 1)
    score = jnp.where(sid_i * SEL_BLOCK <= tq, jnp.where(forced, 1e30, imp_t), -1.0)
    sid = sid_i.astype(F32)
    sel = jnp.zeros((n_s, Q_BLOCK), F32)
    for _ in range(n_sel):
        m = jnp.max(score, axis=0, keepdims=True)
        first = jnp.min(jnp.where(score == m, sid, float(n_s)), axis=0, keepdims=True)
        hit = sid == first
        sel = jnp.where(hit, 1.0, sel)
        score = jnp.where(hit, -2.0, score)
    sel_ref[...] = sel.T.astype(BF16)


def _nsa_win_body(q_ref, *refs, hpg, nwb):
    k_refs, v_refs, o_ref = refs[:nwb], refs[nwb:2 * nwb], refs[2 * nwb]
    qb = pl.program_id(2)
    t0 = qb * Q_BLOCK
    dk = NSA_HEAD_DIM
    nk = nwb * Q_BLOCK
    kcat = jnp.concatenate([r[...] for r in k_refs], axis=0)
    v_aug = jnp.concatenate([r[...] for r in v_refs], axis=0)
    v_aug = jnp.concatenate([v_aug, jnp.ones((nk, LANES), BF16)], axis=1)
    t = t0 + lax.broadcasted_iota(jnp.int32, (Q_BLOCK, nk), 0)
    kpos = t0 - WINDOW + lax.broadcasted_iota(jnp.int32, (Q_BLOCK, nk), 1)
    bias = jnp.where((kpos <= t) & (kpos > t - WINDOW) & (kpos >= 0), 0.0, NEG_BIG)
    cols = [slice(j * LANES, (j + 1) * LANES) for j in range(nk // LANES)]
    hg = min(CMP_HEADS_PER_DOT, hpg)
    for g0 in range(0, hpg, hg):
        qg = jnp.concatenate([q_ref[:, h * dk:(h + 1) * dk] for h in range(g0, g0 + hg)], axis=0)
        s_g = _bdot_nt(qg, kcat)
        e = jnp.concatenate([_exp2_rows(s_g[hl * Q_BLOCK:(hl + 1) * Q_BLOCK], bias, cols) for hl in range(hg)], axis=0)
        od = jnp.dot(e, v_aug, preferred_element_type=F32)
        for hl in range(hg):
            rows = slice(hl * Q_BLOCK, (hl + 1) * Q_BLOCK)
            o_ref[:, (g0 + hl) * dk:(g0 + hl + 1) * dk] = (od[rows, :dk] / od[rows, dk:]).astype(o_ref.dtype)


def _nsa_sel_body(qb_ref, kb_ref, q_ref, k_ref, v_ref, sel_ref, ex_ref, o_ref, m_ref, acc_ref, *, hpg):
    step = pl.program_id(2)
    qb = qb_ref[step]
    kb = kb_ref[step]
    t0 = qb * Q_BLOCK
    tk = k_ref.shape[0]
    dk = NSA_HEAD_DIM

    @pl.when(kb == 0)
    def _():
        m_ref[...] = jnp.full_like(m_ref, NEG_BIG)
        acc_ref[...] = jnp.zeros_like(acc_ref)

    picked = jnp.dot(sel_ref[...], ex_ref[...], preferred_element_type=F32)
    tq = t0 + lax.broadcasted_iota(jnp.int32, (Q_BLOCK, tk), 0)
    kpos = kb * tk + lax.broadcasted_iota(jnp.int32, (Q_BLOCK, tk), 1)
    bias = jnp.where((picked > 0.5) & (kpos <= tq), 0.0, NEG_BIG)
    cols = [slice(j * LANES, (j + 1) * LANES) for j in range(tk // LANES)]
    k_tile = k_ref[...]
    v_aug = jnp.concatenate([v_ref[...], jnp.ones((tk, LANES), BF16)], axis=1)
    hg = SEL_HEADS_PER_DOT

    def scores(g0):
        qg = jnp.concatenate([q_ref[:, h * dk:(h + 1) * dk] for h in range(g0, g0 + hg)], axis=0)
        return _bdot_nt(qg, k_tile)

    s_next = scores(0)
    for g0 in range(0, hpg, hg):
        s_g = s_next
        if g0 + hg < hpg:
            s_next = scores(g0 + hg)
        p_rows, alphas = [], []
        for hl in range(hg):
            rows = slice((g0 + hl) * Q_BLOCK, (g0 + hl + 1) * Q_BLOCK)
            s_h = [s_g[hl * Q_BLOCK:(hl + 1) * Q_BLOCK, cj] + bias[:, cj] for cj in cols]
            mx = s_h[0]
            for s_hj in s_h[1:]:
                mx = jnp.maximum(mx, s_hj)
            m_old = m_ref[rows, :]
            m_new = jnp.maximum(m_old, jnp.broadcast_to(jnp.max(mx, axis=-1, keepdims=True), m_old.shape))
            m_ref[rows, :] = m_new
            alphas.append(jnp.exp2(m_old - m_new))
            p_rows.append(jnp.concatenate([jnp.exp2(s_hj - m_new).astype(BF16) for s_hj in s_h], axis=1))
        pv = jnp.dot(jnp.concatenate(p_rows, axis=0), v_aug, preferred_element_type=F32)
        alpha = jnp.concatenate(alphas, axis=0)
        grows = slice(g0 * Q_BLOCK, (g0 + hg) * Q_BLOCK)
        acc_ref[grows, :dk] = alpha * acc_ref[grows, :dk] + pv[:, :dk]
        acc_ref[grows, dk:] = alpha * acc_ref[grows, dk:] + pv[:, dk:]

    @pl.when(kb == (t0 + Q_BLOCK - 1) // tk)
    def _():
        den = acc_ref[:, dk:]
        o = acc_ref[:, :dk] / jnp.where(den > 0, den, 1.0)
        for h in range(hpg):
            o_ref[:, h * dk:(h + 1) * dk] = o[h * Q_BLOCK:(h + 1) * Q_BLOCK].astype(o_ref.dtype)


def _nsa_attention(q, kvb, k_cmp, v_cmp, batch, n_c):
    n, hd = q.shape
    dk, g = NSA_HEAD_DIM, NSA_KV_GROUPS
    hpg = hd // dk // g
    t = n // batch
    nqb = t // Q_BLOCK
    n_s = t // SEL_BLOCK
    n_sel = min(N_SEL, n_s)
    ncp = k_cmp.shape[2]
    gw = hpg * dk

    c_lo = np.arange(ncp) * CMP_STRIDE
    s_lo = np.arange(n_s) * SEL_BLOCK
    overlap = ((c_lo[:, None] < s_lo[None, :] + SEL_BLOCK) & (c_lo[:, None] + CMP_LEN > s_lo[None, :])
               & (np.arange(ncp)[:, None] < n_c))
    overlap = jnp.asarray(overlap, BF16)

    qspec = pl.BlockSpec((Q_BLOCK, gw), lambda b, gi, qb: (b * nqb + qb, gi))
    seg_qb = CMP_SEG_COLS * CMP_STRIDE // Q_BLOCK
    o_c = sel = None
    for qb0 in range(0, nqb, seg_qb):
        nq = min(seg_qb, nqb - qb0)
        ncols = min(ncp, -(-((qb0 + nq) * Q_BLOCK // CMP_STRIDE) // LANES) * LANES)
        oq = pl.BlockSpec((Q_BLOCK, gw), lambda b, gi, qb, qb0=qb0: (b * nqb + qb0 + qb, gi))
        in_specs = [oq,
                    pl.BlockSpec((None, None, ncols, dk), lambda b, gi, qb: (b, gi, 0, 0)),
                    pl.BlockSpec((None, None, ncols, dk), lambda b, gi, qb: (b, gi, 0, 0)),
                    pl.BlockSpec((ncols, n_s), lambda b, gi, qb: (0, 0))]
        args = [q, k_cmp, v_cmp, overlap]
        aliases = {}
        if o_c is not None:
            in_specs += [pl.BlockSpec(memory_space=pl.ANY), pl.BlockSpec(memory_space=pl.ANY)]
            args += [o_c, sel]
            aliases = {4: 0, 5: 1}
        o_c, sel = pl.pallas_call(
            functools.partial(_nsa_cmp_body, hpg=hpg, n_c=n_c, n_sel=n_sel, qb0=qb0),
            grid=(batch, g, nq),
            in_specs=in_specs,
            out_specs=[oq, pl.BlockSpec((None, None, Q_BLOCK, n_s), lambda b, gi, qb, qb0=qb0: (b, gi, qb0 + qb, 0))],
            out_shape=[jax.ShapeDtypeStruct((n, hd), BF16), jax.ShapeDtypeStruct((batch, g, t, n_s), BF16)],
            input_output_aliases=aliases,
            compiler_params=_params("parallel", "parallel", "parallel"),
            name="nsa_compressed_select",
        )(*args)

    nwb = WINDOW // Q_BLOCK + 1
    kcol, vcol = 4 * g, 5 * g

    def kv_spec(col, j):
        return pl.BlockSpec((Q_BLOCK, dk),
                            lambda b, gi, qb: (b * nqb + jnp.maximum(qb - (nwb - 1) + j, 0), col + gi))

    o_w = pl.pallas_call(
        functools.partial(_nsa_win_body, hpg=hpg, nwb=nwb),
        grid=(batch, g, nqb),
        in_specs=[qspec] + [kv_spec(kcol, j) for j in range(nwb)] + [kv_spec(vcol, j) for j in range(nwb)],
        out_specs=qspec,
        out_shape=jax.ShapeDtypeStruct((n, hd), BF16),
        compiler_params=_params("parallel", "parallel", "parallel"),
        name="nsa_window",
    )(q, *([kvb] * (2 * nwb)))

    tk = min(SEL_KEY_TILE, t)
    nkb = t // tk
    steps = [(qb, kb) for qb in range(nqb) for kb in range((qb * Q_BLOCK + Q_BLOCK - 1) // tk + 1)]
    qb_tab = jnp.asarray([s_[0] for s_ in steps], jnp.int32)
    kb_tab = jnp.asarray([s_[1] for s_ in steps], jnp.int32)
    expand = jnp.asarray(np.arange(n_s)[:, None] == (np.arange(t)[None, :] // SEL_BLOCK), BF16)
    kscol, vscol = 2 * g, 3 * g
    grid_spec = pltpu.PrefetchScalarGridSpec(
        num_scalar_prefetch=2,
        grid=(batch, g, len(steps)),
        in_specs=[
            pl.BlockSpec((Q_BLOCK, gw), lambda b, gi, s_, qt, kt: (b * nqb + qt[s_], gi)),
            pl.BlockSpec((tk, dk), lambda b, gi, s_, qt, kt: (b * nkb + kt[s_], kscol + gi)),
            pl.BlockSpec((tk, dk), lambda b, gi, s_, qt, kt: (b * nkb + kt[s_], vscol + gi)),
            pl.BlockSpec((None, None, Q_BLOCK, n_s), lambda b, gi, s_, qt, kt: (b, gi, qt[s_], 0)),
            pl.BlockSpec((n_s, tk), lambda b, gi, s_, qt, kt: (0, kt[s_])),
        ],
        out_specs=pl.BlockSpec((Q_BLOCK, gw), lambda b, gi, s_, qt, kt: (b * nqb + qt[s_], gi)),
        scratch_shapes=[pltpu.VMEM((hpg * Q_BLOCK, LANES), F32), pltpu.VMEM((hpg * Q_BLOCK, dk + LANES), F32)],
    )
    o_s = pl.pallas_call(
        functools.partial(_nsa_sel_body, hpg=hpg),
        grid_spec=grid_spec,
        out_shape=jax.ShapeDtypeStruct((n, hd), BF16),
        compiler_params=_params("parallel", "parallel", "arbitrary"),
        name="nsa_selected",
    )(qb_tab, kb_tab, q, kvb, kvb, sel, expand)
    return o_c, o_s, o_w


def _nsa_shared_kv(xb, batch, w_kv, cmp_pe, cmp_w1, cmp_b1, cmp_w2, cmp_b2):
    n, d = xb.shape
    g, dk = NSA_KV_GROUPS, NSA_HEAD_DIM
    t = n // batch
    kv = _matmul(xb, w_kv.astype(BF16), tn=768)
    n_c = t // CMP_STRIDE - 1
    ncp = -(-n_c // LANES) * LANES
    rows = batch * n_c * g
    rows_p = -(-rows // 256) * 256
    outs = []
    for i in range(2):
        z = kv[:, i * g * dk:(i + 1) * g * dk].reshape(batch, t // CMP_STRIDE, CMP_STRIDE, g, dk)
        blk = jnp.concatenate([z[:, :-1], z[:, 1:]], axis=2)
        flat = blk.transpose(0, 1, 3, 2, 4).reshape(rows, CMP_LEN * dk)
        flat = jnp.pad(flat, ((0, rows_p - rows), (0, 0)))
        pe = jnp.broadcast_to(cmp_pe[i][:, None, :], (CMP_LEN, 1, dk)).reshape(CMP_LEN * dk)
        c = _cmp_mlp(flat, pe, cmp_w1[i], cmp_b1[i], cmp_w2[i], cmp_b2[i])[:rows]
        c = c.reshape(batch, n_c, g, dk).transpose(0, 2, 1, 3)
        outs.append(jnp.pad(c, ((0, 0), (0, 0), (0, ncp - n_c), (0, 0))))
    return kv.astype(BF16), outs[0], outs[1], n_c


def _gate_combine_body(oc_ref, os_ref, ow_ref, gl_ref, ex_ref, o_ref):
    gates = jax.nn.sigmoid(gl_ref[...])
    acc = None
    for i, r in enumerate((oc_ref, os_ref, ow_ref)):
        term = _dot_split_const(gates, ex_ref[i]) * r[...].astype(F32)
        acc = term if acc is None else acc + term
    o_ref[...] = acc.astype(BF16)


def _gate_combine(o_c, o_s, o_w, glog, nh, tm=256):
    n, hd = o_c.shape
    dk = hd // nh
    ex = np.zeros((3, LANES, hd), np.float32)
    for i in range(3):
        for h in range(nh):
            ex[i, i * nh + h, h * dk:(h + 1) * dk] = 1.0
    row = pl.BlockSpec((tm, hd), lambda i: (i, 0))
    return pl.pallas_call(
        _gate_combine_body,
        grid=(n // tm,),
        in_specs=[row, row, row, pl.BlockSpec((tm, LANES), lambda i: (i, 0)),
                  pl.BlockSpec((3, LANES, hd), lambda i: (0, 0, 0))],
        out_specs=row,
        out_shape=jax.ShapeDtypeStruct((n, hd), BF16),
        compiler_params=_params("parallel"),
        name="nsa_gate_combine",
    )(o_c, o_s, o_w, glog, jnp.asarray(ex, BF16))


def _nsa_layer(xb, batch, shared, w_in, b_gate, w_o):
    kvb, k_cmp, v_cmp, n_c = shared
    n, d = xb.shape
    hd = w_o.shape[0]
    nh = hd // NSA_HEAD_DIM
    q = _matmul(xb, w_in[:, :hd].astype(BF16), out_dtype=BF16,
                out_scale=NSA_HEAD_DIM ** -0.5 * math.log2(math.e))
    wg = jnp.pad(w_in[:, hd:], ((0, 0), (0, LANES - 3 * nh))).astype(BF16)
    bg = jnp.pad(b_gate, (0, LANES - 3 * nh))
    glog = _matmul(xb, wg, bias=bg)
    o_c, o_s, o_w = _nsa_attention(q, kvb, k_cmp, v_cmp, batch, n_c)
    o = _gate_combine(o_c, o_s, o_w, glog, nh)
    return _matmul(o, w_o.astype(BF16))


def kernel(x, ln_g, ln_b, rw_mu, rw_w_rkv, rw_w0, rw_w1, rw_w2, rw_a0, rw_a1, rw_a2, rw_g1, rw_g2, rw_k_k, rw_k_a, rw_r_k, rw_lnx_g, rw_lnx_b, rw_w_o, nsa_w_kv, nsa_cmp_pe, nsa_cmp_w1, nsa_cmp_b1, nsa_cmp_w2, nsa_cmp_b2, nsa_w_in, nsa_b_gate, nsa_w_o, moe_router_w, moe_router_b, moe_w_gu, moe_b_gu, moe_w_down, moe_b_down):
    batch, t, d = x.shape
    depth = ln_g.shape[0]
    n_a = rw_mu.shape[0]
    alpha = (2 * depth) ** 0.25
    h = x.reshape(batch * t, d)
    hb = None
    shared = None
    for layer in range(depth):
        if layer < n_a:
            i = layer
            mix = _rwkv_time_mix(h, batch, rw_mu[i], rw_w_rkv[i], rw_w0[i], rw_w1[i], rw_w2[i], rw_a0[i], rw_a1[i],
                                 rw_a2[i], rw_g1[i], rw_g2[i], rw_k_k[i], rw_k_a[i], rw_r_k[i], rw_lnx_g[i],
                                 rw_lnx_b[i], rw_w_o[i])
        else:
            if shared is None:
                if hb is None:
                    hb = h.astype(BF16)
                shared = _nsa_shared_kv(hb, batch, nsa_w_kv, nsa_cmp_pe, nsa_cmp_w1, nsa_cmp_b1, nsa_cmp_w2,
                                        nsa_cmp_b2)
            j = layer - n_a
            mix = _nsa_layer(hb, batch, shared, nsa_w_in[j], nsa_b_gate[j], nsa_w_o[j])
        h, hb = _add_ln(h, mix, ln_g[layer, 0], ln_b[layer, 0], alpha)
        y4, gates = _moe_ffn(hb, layer, moe_router_w[layer], moe_router_b[layer], moe_w_gu, moe_b_gu, moe_w_down,
                             moe_b_down)
        h, hb = _combine_ln(h, y4, gates, ln_g[layer, 1], ln_b[layer, 1], alpha)
    return h.reshape(batch, t, d)
```

```python
import functools
import math

import numpy as np
import jax
import jax.numpy as jnp
from jax import lax
from jax.experimental import pallas as pl
from jax.experimental.pallas import tpu as pltpu

F32 = jnp.float32
BF16 = jnp.bfloat16

V7X_VMEM_LIMIT_BYTES = 56 * 1024 * 1024
LANES = 128

LN_EPS = 1e-5
RW_HEAD_DIM = 64
RW_GN_EPS = 64e-5
RW_CHUNK = 64
RW_CHUNKS_PER_STEP = 8
NSA_HEAD_DIM = 128
NSA_KV_GROUPS = 2
CMP_STRIDE = 16
CMP_LEN = 32
SEL_BLOCK = 64
N_SEL = 16
WINDOW = 512
Q_BLOCK = 128
SEL_KEY_TILE = 1024
SEL_Q_TILE = 256
SEL_HEADS_PER_DOT = 2
CMP_HEADS_PER_DOT = 4
CMP_SEG_COLS = 256
N_EXPERTS = 32
TOP_K = 4
SWIGLU_LIMIT = 7.0
SWIGLU_ALPHA = 1.702
MOE_ROW_BLOCK = 1024
MOE_F_TILE = 512
MOE_OUT_TILE = 512
NEG_BIG = -1e30


def _params(*sem):
    return pltpu.CompilerParams(dimension_semantics=sem, vmem_limit_bytes=V7X_VMEM_LIMIT_BYTES)


def _bdot(a, b):
    return jnp.dot(a.astype(BF16), b.astype(BF16), preferred_element_type=F32)


def _bdot_nt(a, b):
    return lax.dot_general(a.astype(BF16), b.astype(BF16), (((1,), (1,)), ((), ())),
                           preferred_element_type=F32)


def _dot_const_split(c, x):
    hi = x.astype(BF16)
    lo = (x - hi.astype(F32)).astype(BF16)
    return (jnp.dot(c, hi, preferred_element_type=F32) + jnp.dot(c, lo, preferred_element_type=F32))


def _dot_split_const(x, c):
    hi = x.astype(BF16)
    lo = (x - hi.astype(F32)).astype(BF16)
    return (jnp.dot(hi, c, preferred_element_type=F32) + jnp.dot(lo, c, preferred_element_type=F32))


def _mm_body(a_ref, w_ref, b_ref, o_ref, *, out_scale):
    acc = jnp.dot(a_ref[...], w_ref[...], preferred_element_type=F32) + b_ref[...]
    if out_scale is not None:
        acc = acc * out_scale
    o_ref[...] = acc.astype(o_ref.dtype)


def _matmul(a, w, bias=None, out_dtype=F32, out_scale=None, tm=512, tn=1024):
    m, k = a.shape
    n = w.shape[1]
    tm = min(tm, m)
    tn = min(tn, n)
    assert m % tm == 0 and n % tn == 0, (m, n, tm, tn)
    if bias is None:
        bias = jnp.zeros((1, n), F32)
    return pl.pallas_call(
        functools.partial(_mm_body, out_scale=out_scale),
        grid=(n // tn, m // tm),
        in_specs=[pl.BlockSpec((tm, k), lambda j, i: (i, 0)),
                  pl.BlockSpec((k, tn), lambda j, i: (0, j)),
                  pl.BlockSpec((1, tn), lambda j, i: (0, j))],
        out_specs=pl.BlockSpec((tm, tn), lambda j, i: (i, j)),
        out_shape=jax.ShapeDtypeStruct((m, n), out_dtype),
        compiler_params=_params("parallel", "parallel"),
        name="dense_matmul",
    )(a, w, bias.reshape(1, n).astype(F32))


def _add_ln_body(x_ref, m_ref, g_ref, b_ref, o_ref, ob_ref, *, alpha):
    z = alpha * x_ref[...] + m_ref[...]
    mu = jnp.mean(z, -1, keepdims=True)
    zc = z - mu
    var = jnp.mean(zc * zc, -1, keepdims=True)
    y = zc * lax.rsqrt(var + LN_EPS) * g_ref[...] + b_ref[...]
    o_ref[...] = y
    ob_ref[...] = y.astype(BF16)


def _add_ln(x, mix, g, b, alpha, tm=256):
    n, d = x.shape
    row = pl.BlockSpec((tm, d), lambda i: (i, 0))
    vec = pl.BlockSpec((1, d), lambda i: (0, 0))
    return pl.pallas_call(
        functools.partial(_add_ln_body, alpha=alpha),
        grid=(n // tm,),
        in_specs=[row, row, vec, vec],
        out_specs=[row, row],
        out_shape=[jax.ShapeDtypeStruct((n, d), F32), jax.ShapeDtypeStruct((n, d), BF16)],
        compiler_params=_params("parallel"),
        name="add_layer_norm",
    )(x, mix, g.reshape(1, d), b.reshape(1, d))


def _rw_mix_body(x_ref, last_ref, mu_ref, *o_refs):
    x = x_ref[...]
    prev = pltpu.roll(x, shift=1, axis=0)
    row = lax.broadcasted_iota(jnp.int32, x.shape, 0)
    prev = jnp.where(row == 0, last_ref[0], prev)
    xx = prev - x
    for i, o_ref in enumerate(o_refs):
        o_ref[...] = (x + xx * mu_ref[i:i + 1, :]).astype(BF16)


def _rw_mix(x2, batch, mu, tm=256):
    n, d = x2.shape
    t = n // batch
    nt = t // tm
    last = x2.reshape(batch, nt, tm, d)[:, :, tm - 1, :]
    last = jnp.concatenate([jnp.zeros((batch, 1, d), F32), last[:, :-1]], axis=1).reshape(batch * nt, 1, d)
    row = pl.BlockSpec((tm, d), lambda i: (i, 0))
    return pl.pallas_call(
        _rw_mix_body,
        grid=(n // tm,),
        in_specs=[row, pl.BlockSpec((1, 1, d), lambda i: (i, 0, 0)), pl.BlockSpec((6, d), lambda i: (0, 0))],
        out_specs=[row] * 6,
        out_shape=[jax.ShapeDtypeStruct((n, d), BF16)] * 6,
        compiler_params=_params("parallel"),
        name="rwkv_token_shift",
    )(x2, last, mu)


def _rw_lowrank_body(xw_ref, xa_ref, xg_ref, w1_ref, w2_ref, w0_ref, a1_ref, a2_ref, a0_ref, g1_ref, g2_ref,
                     lw_ref, a_ref, g_ref):
    z = w0_ref[...] + _bdot(jnp.tanh(jnp.dot(xw_ref[...], w1_ref[...], preferred_element_type=F32)), w2_ref[...])
    w_log = jnp.minimum(z, 0.0) - jnp.log(1.0 + jnp.exp(-jnp.abs(z))) - 0.5
    lw_ref[...] = -jnp.exp(w_log)
    za = a0_ref[...] + _bdot(jnp.dot(xa_ref[...], a1_ref[...], preferred_element_type=F32), a2_ref[...])
    a_ref[...] = jax.nn.sigmoid(za)
    hg = jax.nn.sigmoid(jnp.dot(xg_ref[...], g1_ref[...], preferred_element_type=F32))
    g_ref[...] = _bdot(hg, g2_ref[...])


def _pad_rank(w_in, w_out):
    r = w_in.shape[1]
    rp = -(-r // LANES) * LANES
    return (jnp.pad(w_in, ((0, 0), (0, rp - r))).astype(BF16), jnp.pad(w_out, ((0, rp - r), (0, 0))).astype(BF16))


def _rw_lowrank(xw, xa, xg, w0, w1, w2, a0, a1, a2, g1, g2, tm=256):
    n, d = xw.shape
    w1p, w2p = _pad_rank(w1, w2)
    a1p, a2p = _pad_rank(a1, a2)
    g1p, g2p = _pad_rank(g1, g2)
    row = pl.BlockSpec((tm, d), lambda i: (i, 0))
    full = lambda arr: pl.BlockSpec(arr.shape, lambda i: (0, 0))
    w0r, a0r = w0.reshape(1, d), a0.reshape(1, d)
    return pl.pallas_call(
        _rw_lowrank_body,
        grid=(n // tm,),
        in_specs=[row, row, row, full(w1p), full(w2p), full(w0r), full(a1p), full(a2p), full(a0r), full(g1p), full(g2p)],
        out_specs=[row] * 3,
        out_shape=[jax.ShapeDtypeStruct((n, d), F32)] * 3,
        compiler_params=_params("parallel"),
        name="rwkv_lowrank",
    )(xw, xa, xg, w1p, w2p, w0r, a1p, a2p, a0r, g1p, g2p)


def _wkv_body(r_ref, k_ref, v_ref, lw_ref, a_ref, g_ref, kk_ref, ka_ref, rk_ref, lng_ref, lnb_ref,
              o_ref, s_ref, rp_ref, yq_ref, *, nchunk):
    L = RW_CHUNK
    H2 = 2 * L

    @pl.when(pl.program_id(2) == 0)
    def _():
        s_ref[...] = jnp.zeros_like(s_ref)

    lane = lax.broadcasted_iota(jnp.int32, (1, LANES), 1)
    mask0 = (lane < RW_HEAD_DIM).astype(F32)
    mask1 = 1.0 - mask0
    ri = lax.broadcasted_iota(jnp.int32, (H2, H2), 0)
    ci = lax.broadcasted_iota(jnp.int32, (H2, H2), 1)
    same_head = (ri // L) == (ci // L)
    strict = (same_head & (ci < ri)).astype(F32)
    incl = (same_head & (ci <= ri)).astype(F32)
    diag16 = ((ri // 16) == (ci // 16)).astype(F32)
    eye = (ri == ci).astype(F32)
    head_ones = same_head.astype(BF16)
    tl = lax.broadcasted_iota(jnp.int32, (L, L), 0)
    sl = lax.broadcasted_iota(jnp.int32, (L, L), 1)
    tri_incl = (sl <= tl).astype(BF16)

    def stack(x):
        return jnp.concatenate([x * mask0, x * mask1], axis=0)

    k_k = kk_ref[...]
    k_a = ka_ref[...]
    r_k = rk_ref[...]

    chunks = range(nchunk)

    def each(fn, *lists):
        return [fn(*xs) for xs in zip(*lists)]

    def rows_of(x):
        return [x[c * L:(c + 1) * L] for c in chunks]

    r_all = r_ref[...]
    k_all = k_ref[...]
    v_all = v_ref[...]
    ag_all = a_ref[...]
    kk_all = k_all * k_k
    ss_all = _dot_split_const(kk_all * kk_all, head_ones)
    kk_all = kk_all / jnp.maximum(jnp.sqrt(ss_all), 1e-12)
    kmod_all = k_all * (1.0 + (ag_all - 1.0) * k_a)
    bv_all = kk_all * ag_all
    lw_c = rows_of(lw_ref[...])
    cl_c = each(lambda lw: _dot_const_split(tri_incl, lw), lw_c)
    last_c = each(lambda cl: cl[L - 1:L, :], cl_c)
    cl_all = jnp.concatenate(cl_c, axis=0)
    clp_all = cl_all - lw_ref[...]
    end_all = jnp.concatenate(each(lambda cl, la: la - cl, cl_c, last_c), axis=0)
    e_neg = jnp.exp(-cl_all)
    e_end = jnp.exp(end_all)
    at_c = rows_of(-kk_all * jnp.exp(clp_all))
    rt_c = rows_of(r_all * jnp.exp(cl_all))
    bt_c = rows_of(bv_all * e_neg)
    kt_c = rows_of(kmod_all * e_neg)
    be_c = rows_of(bv_all * e_end)
    ke_c = rows_of(kmod_all * e_end)
    v_s = each(stack, rows_of(v_all))
    at_s = each(stack, at_c)
    rt_s = each(stack, rt_c)
    gmat = each(lambda a_, r_, b_, k_: _bdot_nt(jnp.concatenate([a_, r_], axis=0),
                                                jnp.concatenate([stack(b_), stack(k_)], axis=0)),
                at_s, rt_s, bt_c, kt_c)
    a_ab = each(lambda gm: gm[:H2, :H2] * strict, gmat)
    a_ak = each(lambda gm: gm[:H2, H2:] * strict, gmat)
    a_rb = each(lambda gm: gm[H2:, :H2] * incl, gmat)
    a_rk = each(lambda gm: gm[H2:, H2:] * incl, gmat)
    dblk = each(lambda a_: a_ * diag16, a_ab)
    off = each(lambda a_, d_: a_ - d_, a_ab, dblk)
    d2 = each(_bdot, dblk, dblk)
    d4 = each(_bdot, d2, d2)
    d8 = each(_bdot, d4, d4)
    dinv = each(lambda d_, d2_: _bdot(eye + d_, eye + d2_), dblk, d2)
    dinv = each(lambda di, d4_: _bdot(di, eye + d4_), dinv, d4)
    dinv = each(lambda di, d8_: _bdot(di, eye + d8_), dinv, d8)
    e1 = each(_bdot, dinv, off)
    e2 = each(_bdot, e1, e1)
    minv = each(lambda e1_, e2_: _bdot(eye + e1_, eye + e2_), e1, e2)
    minv = each(_bdot, minv, dinv)
    x_ak = each(_bdot, a_ak, v_s)
    zu = each(lambda mi, a_, x_: _bdot(mi, jnp.concatenate([a_, x_], axis=1)), minv, at_s, x_ak)
    w2 = each(_bdot, a_rb, zu)
    rkv = each(_bdot, a_rk, v_s)
    pq = each(lambda b_, z_: _bdot(stack(b_).T, z_), be_c, zu)
    kv2 = each(lambda k_, v_: _bdot(stack(k_).T, v_), ke_c, v_s)
    for c in chunks:
        rp_ref[c] = jnp.concatenate([rt_s[c] + w2[c][:, :LANES],
                                     eye * jnp.exp(last_c[c]) + pq[c][:, :LANES]], axis=0)
        yq_ref[c] = jnp.concatenate([w2[c][:, LANES:] + rkv[c], pq[c][:, LANES:] + kv2[c]], axis=0)

    ys = []
    s = s_ref[...]
    for c in chunks:
        res = _bdot(rp_ref[c], s) + yq_ref[c]
        ys.append(res[:L] + res[L:H2])
        s = res[H2:]
    s_ref[...] = s
    y = jnp.concatenate(ys, axis=0)
    inv_n = 1.0 / RW_HEAD_DIM
    ym = _dot_split_const(y, head_ones) * inv_n
    yc = y - ym
    yv = _dot_split_const(yc * yc, head_ones) * inv_n
    yn = yc * lax.rsqrt(yv + RW_GN_EPS) * lng_ref[...] + lnb_ref[...]
    bonus = _dot_split_const(r_all * kmod_all * r_k, head_ones) * v_all
    o_ref[...] = ((yn + bonus) * g_ref[...]).astype(BF16)


def _wkv(r, k, v, lw, a, g, k_k, k_a, r_k, lnx_g, lnx_b, batch):
    n, d = r.shape
    t = n // batch
    nchunk = RW_CHUNKS_PER_STEP
    tb = RW_CHUNK * nchunk
    while t % tb:
        nchunk //= 2
        tb = RW_CHUNK * nchunk
    nt = t // tb
    row = pl.BlockSpec((tb, LANES), lambda b, hp, c: (b * nt + c, hp))
    vec = pl.BlockSpec((1, LANES), lambda b, hp, c: (0, hp))
    sq = pltpu.VMEM((nchunk, 2 * LANES, LANES), F32)
    vecs = [z.reshape(1, d) for z in (k_k, k_a, r_k, lnx_g, lnx_b)]
    return pl.pallas_call(
        functools.partial(_wkv_body, nchunk=nchunk),
        grid=(batch, d // LANES, nt),
        in_specs=[row] * 6 + [vec] * 5,
        out_specs=row,
        out_shape=jax.ShapeDtypeStruct((n, d), BF16),
        scratch_shapes=[pltpu.VMEM((LANES, LANES), F32), sq, sq],
        compiler_params=_params("parallel", "parallel", "arbitrary"),
        name="rwkv_chunked_scan",
    )(r, k, v, lw, a, g, *vecs)


def _rwkv_time_mix(x2, batch, mu, w_rkv, w0, w1, w2, a0, a1, a2, g1, g2, k_k, k_a, r_k, lnx_g, lnx_b, w_o):
    xr, xw, xk, xv, xa, xg = _rw_mix(x2, batch, mu)
    r = _matmul(xr, w_rkv[0].astype(BF16))
    k = _matmul(xk, w_rkv[1].astype(BF16))
    v = _matmul(xv, w_rkv[2].astype(BF16))
    lw, a, g = _rw_lowrank(xw, xa, xg, w0, w1, w2, a0, a1, a2, g1, g2)
    z = _wkv(r, k, v, lw, a, g, k_k, k_a, r_k.reshape(-1), lnx_g, lnx_b, batch)
    return _matmul(z, w_o.astype(BF16))


def _router_body(x_ref, w_ref, b_ref, idx_ref, gate_ref):
    logits = jnp.dot(x_ref[...], w_ref[...], preferred_element_type=F32) + b_ref[...]
    lane = lax.broadcasted_iota(jnp.int32, logits.shape, 1).astype(F32)
    cur = logits
    vals, idxs = [], []
    for _ in range(TOP_K):
        m = jnp.max(cur, axis=-1, keepdims=True)
        i = jnp.min(jnp.where(cur == m, lane, float(LANES)), axis=-1, keepdims=True)
        vals.append(m)
        idxs.append(i)
        cur = jnp.where(lane == i, -3e38, cur)
    es = [jnp.exp(vv - vals[0]) for vv in vals]
    den = es[0]
    for e in es[1:]:
        den = den + e
    idx_out = jnp.zeros(logits.shape, F32)
    gate_out = jnp.zeros(logits.shape, F32)
    for kk in range(TOP_K):
        idx_out = jnp.where(lane == kk, idxs[kk], idx_out)
        gate_out = jnp.where(lane == kk, es[kk] / den, gate_out)
    idx_ref[...] = idx_out.astype(jnp.int32)
    gate_ref[...] = gate_out


def _router(xb, router_w, router_b, tm=512):
    n, d = xb.shape
    e = router_w.shape[1]
    wp = jnp.pad(router_w, ((0, 0), (0, LANES - e))).astype(BF16)
    bp = jnp.concatenate([router_b.astype(F32), jnp.full((LANES - e,), NEG_BIG, F32)]).reshape(1, LANES)
    row = pl.BlockSpec((tm, LANES), lambda i: (i, 0))
    idx, gate = pl.pallas_call(
        _router_body,
        grid=(n // tm,),
        in_specs=[pl.BlockSpec((tm, d), lambda i: (i, 0)), pl.BlockSpec((d, LANES), lambda i: (0, 0)),
                  pl.BlockSpec((1, LANES), lambda i: (0, 0))],
        out_specs=[row, row],
        out_shape=[jax.ShapeDtypeStruct((n, LANES), jnp.int32), jax.ShapeDtypeStruct((n, LANES), F32)],
        compiler_params=_params("parallel"),
        name="moe_router",
    )(xb, wp, bp)
    return idx[:, :TOP_K], gate


def _moe_expert_body(be_ref, nu_ref, x_ref, wg_ref, wu_ref, bg_ref, bu_ref, wd_ref, bd_ref, o_ref,
                     act_ref, *, nf):
    s = pl.program_id(1)
    used = pl.program_id(0) < nu_ref[0]

    @pl.when(used & (s < nf))
    def _():
        x = x_ref[...]
        gate = jnp.dot(x, wg_ref[...].astype(BF16), preferred_element_type=F32) + bg_ref[...]
        up = jnp.dot(x, wu_ref[...].astype(BF16), preferred_element_type=F32) + bu_ref[...]
        gate = jnp.minimum(gate, SWIGLU_LIMIT)
        up = jnp.clip(up, -SWIGLU_LIMIT, SWIGLU_LIMIT)
        act = (up + 1.0) * (gate * jax.nn.sigmoid(gate * SWIGLU_ALPHA))
        act_ref[jnp.minimum(s, nf - 1)] = act.astype(BF16)

    @pl.when(used & (s >= nf))
    def _():
        act = jnp.concatenate([act_ref[f] for f in range(nf)], axis=1)
        y = jnp.dot(act, wd_ref[...].astype(BF16), preferred_element_type=F32) + bd_ref[...]
        o_ref[...] = y.astype(o_ref.dtype)

    @pl.when(jnp.logical_not(used) & (s >= nf))
    def _():
        o_ref[...] = jnp.zeros_like(o_ref)


def _moe_experts(xs, blk_e, n_used, layer, w_gu, b_gu, w_down, b_down):
    n_rows, d = xs.shape
    depth, ne, _, f2 = w_gu.shape
    fdim = f2 // 2
    tm, tf = MOE_ROW_BLOCK, MOE_F_TILE
    nf = fdim // tf
    n_blk = n_rows // tm
    tn = MOE_OUT_TILE
    nn = d // tn

    def fidx(s):
        return jnp.minimum(s, nf - 1)

    def nidx(s):
        return jnp.maximum(s - nf, 0)

    grid_spec = pltpu.PrefetchScalarGridSpec(
        num_scalar_prefetch=2,
        grid=(n_blk, nf + nn),
        in_specs=[
            pl.BlockSpec((tm, d), lambda i, s, be, nu: (jnp.minimum(i, nu[0] - 1), 0)),
            pl.BlockSpec((None, None, d, tf), lambda i, s, be, nu: (layer, be[i], 0, fidx(s))),
            pl.BlockSpec((None, None, d, tf), lambda i, s, be, nu: (layer, be[i], 0, nf + fidx(s))),
            pl.BlockSpec((None, None, 1, tf), lambda i, s, be, nu: (layer, be[i], 0, fidx(s))),
            pl.BlockSpec((None, None, 1, tf), lambda i, s, be, nu: (layer, be[i], 0, nf + fidx(s))),
            pl.BlockSpec((None, None, fdim, tn), lambda i, s, be, nu: (layer, be[i], 0, nidx(s))),
            pl.BlockSpec((None, None, 1, tn), lambda i, s, be, nu: (layer, be[i], 0, nidx(s))),
        ],
        out_specs=pl.BlockSpec((tm, tn), lambda i, s, be, nu: (i, nidx(s))),
        scratch_shapes=[pltpu.VMEM((nf, tm, tf), BF16)],
    )
    bgu = b_gu.reshape(depth, ne, 1, f2)
    return pl.pallas_call(
        functools.partial(_moe_expert_body, nf=nf),
        grid_spec=grid_spec,
        out_shape=jax.ShapeDtypeStruct((n_rows, d), BF16),
        compiler_params=_params("arbitrary", "arbitrary"),
        name="moe_experts",
    )(blk_e, n_used, xs, w_gu, w_gu, bgu, bgu, w_down, b_down.reshape(depth, ne, 1, d))


def _combine_ln_body(x_ref, y_ref, gate_ref, g_ref, b_ref, o_ref, ob_ref, *, alpha):
    ffn = y_ref[0].astype(F32) * gate_ref[:, 0:1]
    for kk in range(1, TOP_K):
        ffn = ffn + y_ref[kk].astype(F32) * gate_ref[:, kk:kk + 1]
    z = alpha * x_ref[...] + ffn
    mu = jnp.mean(z, -1, keepdims=True)
    zc = z - mu
    var = jnp.mean(zc * zc, -1, keepdims=True)
    y = zc * lax.rsqrt(var + LN_EPS) * g_ref[...] + b_ref[...]
    o_ref[...] = y
    ob_ref[...] = y.astype(BF16)


def _combine_ln(x, y4, gates, g, b, alpha, tm=256):
    n, d = x.shape
    row = pl.BlockSpec((tm, d), lambda i: (i, 0))
    vec = pl.BlockSpec((1, d), lambda i: (0, 0))
    return pl.pallas_call(
        functools.partial(_combine_ln_body, alpha=alpha),
        grid=(n // tm,),
        in_specs=[row, pl.BlockSpec((TOP_K, tm, d), lambda i: (0, i, 0)),
                  pl.BlockSpec((tm, LANES), lambda i: (i, 0)), vec, vec],
        out_specs=[row, row],
        out_shape=[jax.ShapeDtypeStruct((n, d), F32), jax.ShapeDtypeStruct((n, d), BF16)],
        compiler_params=_params("parallel"),
        name="moe_combine_layer_norm",
    )(x, y4, gates, g.reshape(1, d), b.reshape(1, d))


def _moe_ffn(xb, layer, router_w, router_b, w_gu, b_gu, w_down, b_down):
    n, d = xb.shape
    tm = MOE_ROW_BLOCK
    top_i, gates = _router(xb, router_w, router_b)
    flat_e = top_i.reshape(-1)
    onehot = (flat_e[:, None] == jnp.arange(N_EXPERTS, dtype=jnp.int32)[None, :]).astype(jnp.int32)
    csum = jnp.cumsum(onehot, axis=0)
    rank = jnp.take_along_axis(csum, flat_e[:, None], axis=1)[:, 0] - 1
    counts = csum[-1]
    padded = ((counts + tm - 1) // tm) * tm
    pad_end = jnp.cumsum(padded)
    pad_start = pad_end - padded
    dest = pad_start[flat_e] + rank
    n_rows = -(-(n * TOP_K) // tm) * tm + N_EXPERTS * tm
    n_blk = n_rows // tm
    blk_start = jnp.arange(n_blk, dtype=jnp.int32) * tm
    blk_e = jnp.minimum(jnp.sum((pad_end[None, :] <= blk_start[:, None]).astype(jnp.int32), axis=1),
                        N_EXPERTS - 1).astype(jnp.int32)
    n_used = (pad_end[-1] // tm).astype(jnp.int32).reshape(1)
    flat_tok = jnp.arange(n * TOP_K, dtype=jnp.int32) // TOP_K
    row_tok = (jnp.arange(n_rows, dtype=jnp.int32) % n).at[dest].set(flat_tok)
    xs = jnp.take(xb, row_tok, axis=0, mode='clip')
    ys = _moe_experts(xs, blk_e, n_used, layer, w_gu, b_gu, w_down, b_down)
    dest_kmajor = dest.reshape(n, TOP_K).T.reshape(-1)
    y4 = jnp.take(ys, dest_kmajor, axis=0, mode='clip').reshape(TOP_K, n, d)
    return y4, gates


def _cmp_mlp_body(x_ref, pe_ref, w1_ref, b1_ref, w2_ref, b2_ref, o_ref):
    h = _bdot(x_ref[...] + pe_ref[...], w1_ref[...]) + b1_ref[...]
    h = jax.nn.gelu(h)
    o_ref[...] = (_bdot(h, w2_ref[...]) + b2_ref[...]).astype(o_ref.dtype)


def _cmp_mlp(flat, pe, w1, b1, w2, b2, tm=256):
    m, kd = flat.shape
    hid = w1.shape[1]
    dk = w2.shape[1]
    full = lambda shp: pl.BlockSpec(shp, lambda i: (0, 0))
    return pl.pallas_call(
        _cmp_mlp_body,
        grid=(m // tm,),
        in_specs=[pl.BlockSpec((tm, kd), lambda i: (i, 0)), full((1, kd)), full((kd, hid)), full((1, hid)),
                  full((hid, dk)), full((1, dk))],
        out_specs=pl.BlockSpec((tm, dk), lambda i: (i, 0)),
        out_shape=jax.ShapeDtypeStruct((m, dk), BF16),
        compiler_params=_params("parallel"),
        name="nsa_compress_mlp",
    )(flat, pe.reshape(1, kd), w1.astype(BF16), b1.reshape(1, hid), w2.astype(BF16), b2.reshape(1, dk))


def _group_rows(q_ref, hpg):
    dk = NSA_HEAD_DIM
    return jnp.concatenate([q_ref[:, h * dk:(h + 1) * dk] for h in range(hpg)], axis=0)


def _softmax_rows(s, mask):
    s = jnp.where(mask, s, NEG_BIG)
    m = jnp.max(s, axis=-1, keepdims=True)
    e = jnp.where(mask, jnp.exp2(s - m), 0.0)
    den = jnp.sum(e, axis=-1, keepdims=True)
    return e / jnp.where(den > 0, den, 1.0)


def _exp2_rows(s_rows, bias, cols):
    s_h = [s_rows[:, cj] + bias[:, cj] for cj in cols]
    mx = s_h[0]
    for s_hj in s_h[1:]:
        mx = jnp.maximum(mx, s_hj)
    m = jnp.broadcast_to(jnp.max(mx, axis=-1, keepdims=True), mx.shape)
    return jnp.concatenate([jnp.exp2(s_hj - m).astype(BF16) for s_hj in s_h], axis=1)


def _nsa_cmp_body(q_ref, kc_ref, vc_ref, ov_ref, *rest, hpg, n_c, n_sel, qb0):
    o_ref, sel_ref = rest[-2:]
    qb = pl.program_id(2) + qb0
    t0 = qb * Q_BLOCK
    ncols = kc_ref.shape[0]
    n_s = ov_ref.shape[1]
    dk = NSA_HEAD_DIM
    tq_c = t0 + lax.broadcasted_iota(jnp.int32, (Q_BLOCK, ncols), 0)
    cid = lax.broadcasted_iota(jnp.int32, (Q_BLOCK, ncols), 1)
    bias = jnp.where((cid * CMP_STRIDE + (CMP_LEN - 1) <= tq_c) & (cid < n_c), 0.0, NEG_BIG)
    t_row = t0 + lax.broadcasted_iota(jnp.int32, (Q_BLOCK, LANES), 0)
    row_live = t_row >= CMP_LEN - 1
    cols = [slice(j * LANES, (j + 1) * LANES) for j in range(ncols // LANES)]
    k_tile = kc_ref[...]
    v_aug = jnp.concatenate([vc_ref[...], jnp.ones((ncols, LANES), BF16)], axis=1)
    ov = ov_ref[...]
    hg = min(CMP_HEADS_PER_DOT, hpg)
    imp = jnp.zeros((Q_BLOCK, n_s), F32)
    for g0 in range(0, hpg, hg):
        qg = jnp.concatenate([q_ref[:, h * dk:(h + 1) * dk] for h in range(g0, g0 + hg)], axis=0)
        s_g = _bdot_nt(qg, k_tile)
        e = jnp.concatenate([_exp2_rows(s_g[hl * Q_BLOCK:(hl + 1) * Q_BLOCK], bias, cols) for hl in range(hg)], axis=0)
        od = jnp.dot(e, v_aug, preferred_element_type=F32)
        ih = jnp.dot(e, ov, preferred_element_type=F32)
        for hl in range(hg):
            rows = slice(hl * Q_BLOCK, (hl + 1) * Q_BLOCK)
            inv = jnp.where(row_live, 1.0 / od[rows, dk:], 0.0)
            o_ref[:, (g0 + hl) * dk:(g0 + hl + 1) * dk] = (od[rows, :dk] * inv).astype(o_ref.dtype)
            inv_s = inv[:, :n_s] if n_s <= LANES else jnp.concatenate([inv] * (n_s // LANES), axis=1)
            imp = imp + ih[rows] * inv_s
    imp_t = imp.T
    tq = t0 + lax.broadcasted_iota(jnp.int32, (n_s, Q_BLOCK), 1)
    sid_i = lax.broadcasted_iota(jnp.int32, (n_s, Q_BLOCK), 0)
    cur = tq // SEL_BLOCK
    forced = (sid_i == 0) | (sid_i == cur) | (sid_i == cur - 1)
    score = jnp.where(sid_i * SEL_BLOCK <= tq, jnp.where(forced, 1e30, imp_t), -1.0)
    sid = sid_i.astype(F32)
    sel = jnp.zeros((n_s, Q_BLOCK), F32)
    for _ in range(n_sel):
        m = jnp.max(score, axis=0, keepdims=True)
        first = jnp.min(jnp.where(score == m, sid, float(n_s)), axis=0, keepdims=True)
        hit = sid == first
        sel = jnp.where(hit, 1.0, sel)
        score = jnp.where(hit, -2.0, score)
    sel_ref[...] = sel.T.astype(BF16)


def _nsa_win_body(q_ref, *refs, hpg, nwb):
    k_refs, v_refs, o_ref = refs[:nwb], refs[nwb:2 * nwb], refs[2 * nwb]
    qb = pl.program_id(2)
    t0 = qb * Q_BLOCK
    dk = NSA_HEAD_DIM
    nk = nwb * Q_BLOCK
    kcat = jnp.concatenate([r[...] for r in k_refs], axis=0)
    v_aug = jnp.concatenate([r[...] for r in v_refs], axis=0)
    v_aug = jnp.concatenate([v_aug, jnp.ones((nk, LANES), BF16)], axis=1)
    t = t0 + lax.broadcasted_iota(jnp.int32, (Q_BLOCK, nk), 0)
    kpos = t0 - WINDOW + lax.broadcasted_iota(jnp.int32, (Q_BLOCK, nk), 1)
    bias = jnp.where((kpos <= t) & (kpos > t - WINDOW) & (kpos >= 0), 0.0, NEG_BIG)
    cols = [slice(j * LANES, (j + 1) * LANES) for j in range(nk // LANES)]
    hg = min(CMP_HEADS_PER_DOT, hpg)
    for g0 in range(0, hpg, hg):
        qg = jnp.concatenate([q_ref[:, h * dk:(h + 1) * dk] for h in range(g0, g0 + hg)], axis=0)
        s_g = _bdot_nt(qg, kcat)
        e = jnp.concatenate([_exp2_rows(s_g[hl * Q_BLOCK:(hl + 1) * Q_BLOCK], bias, cols) for hl in range(hg)], axis=0)
        od = jnp.dot(e, v_aug, preferred_element_type=F32)
        for hl in range(hg):
            rows = slice(hl * Q_BLOCK, (hl + 1) * Q_BLOCK)
            o_ref[:, (g0 + hl) * dk:(g0 + hl + 1) * dk] = (od[rows, :dk] / od[rows, dk:]).astype(o_ref.dtype)


def _nsa_sel_body(qb_ref, kb_ref, q_ref, k_ref, v_ref, sel_ref, ex_ref, o_ref, m_ref, acc_ref, *, hpg):
    step = pl.program_id(2)
    qb = qb_ref[step]
    kb = kb_ref[step]
    qn = q_ref.shape[0]
    t0 = qb * qn
    tk = k_ref.shape[0]
    dk = NSA_HEAD_DIM

    @pl.when(kb == 0)
    def _():
        m_ref[...] = jnp.full_like(m_ref, NEG_BIG)
        acc_ref[...] = jnp.zeros_like(acc_ref)

    picked = jnp.dot(sel_ref[...], ex_ref[...], preferred_element_type=F32)
    tq = t0 + lax.broadcasted_iota(jnp.int32, (qn, tk), 0)
    kpos = kb * tk + lax.broadcasted_iota(jnp.int32, (qn, tk), 1)
    bias = jnp.where((picked > 0.5) & (kpos <= tq), 0.0, NEG_BIG)
    cols = [slice(j * LANES, (j + 1) * LANES) for j in range(tk // LANES)]
    k_tile = k_ref[...]
    v_aug = jnp.concatenate([v_ref[...], jnp.ones((tk, LANES), BF16)], axis=1)
    hg = SEL_HEADS_PER_DOT

    def scores(g0):
        qg = jnp.concatenate([q_ref[:, h * dk:(h + 1) * dk] for h in range(g0, g0 + hg)], axis=0)
        return _bdot_nt(qg, k_tile)

    s_next = scores(0)
    for g0 in range(0, hpg, hg):
        s_g = s_next
        if g0 + hg < hpg:
            s_next = scores(g0 + hg)
        p_rows, alphas = [], []
        for hl in range(hg):
            rows = slice((g0 + hl) * qn, (g0 + hl + 1) * qn)
            s_h = [s_g[hl * qn:(hl + 1) * qn, cj] + bias[:, cj] for cj in cols]
            mx = s_h[0]
            for s_hj in s_h[1:]:
                mx = jnp.maximum(mx, s_hj)
            m_old = m_ref[rows, :]
            m_new = jnp.maximum(m_old, jnp.broadcast_to(jnp.max(mx, axis=-1, keepdims=True), m_old.shape))
            m_ref[rows, :] = m_new
            alphas.append(jnp.exp2(m_old - m_new))
            p_rows.append(jnp.concatenate([jnp.exp2(s_hj - m_new).astype(BF16) for s_hj in s_h], axis=1))
        pv = jnp.dot(jnp.concatenate(p_rows, axis=0), v_aug, preferred_element_type=F32)
        alpha = jnp.concatenate(alphas, axis=0)
        grows = slice(g0 * qn, (g0 + hg) * qn)
        acc_ref[grows, :dk] = alpha * acc_ref[grows, :dk] + pv[:, :dk]
        acc_ref[grows, dk:] = alpha * acc_ref[grows, dk:] + pv[:, dk:]

    @pl.when(kb == (t0 + qn - 1) // tk)
    def _():
        den = acc_ref[:, dk:]
        o = acc_ref[:, :dk] / jnp.where(den > 0, den, 1.0)
        for h in range(hpg):
            o_ref[:, h * dk:(h + 1) * dk] = o[h * qn:(h + 1) * qn].astype(o_ref.dtype)


def _nsa_attention(q, kvb, k_cmp, v_cmp, batch, n_c):
    n, hd = q.shape
    dk, g = NSA_HEAD_DIM, NSA_KV_GROUPS
    hpg = hd // dk // g
    t = n // batch
    nqb = t // Q_BLOCK
    n_s = t // SEL_BLOCK
    n_sel = min(N_SEL, n_s)
    ncp = k_cmp.shape[2]
    gw = hpg * dk

    c_lo = np.arange(ncp) * CMP_STRIDE
    s_lo = np.arange(n_s) * SEL_BLOCK
    overlap = ((c_lo[:, None] < s_lo[None, :] + SEL_BLOCK) & (c_lo[:, None] + CMP_LEN > s_lo[None, :])
               & (np.arange(ncp)[:, None] < n_c))
    overlap = jnp.asarray(overlap, BF16)

    qspec = pl.BlockSpec((Q_BLOCK, gw), lambda b, gi, qb: (b * nqb + qb, gi))
    seg_qb = CMP_SEG_COLS * CMP_STRIDE // Q_BLOCK
    o_c = sel = None
    for qb0 in range(0, nqb, seg_qb):
        nq = min(seg_qb, nqb - qb0)
        ncols = min(ncp, -(-((qb0 + nq) * Q_BLOCK // CMP_STRIDE) // LANES) * LANES)
        oq = pl.BlockSpec((Q_BLOCK, gw), lambda b, gi, qb, qb0=qb0: (b * nqb + qb0 + qb, gi))
        in_specs = [oq,
                    pl.BlockSpec((None, None, ncols, dk), lambda b, gi, qb: (b, gi, 0, 0)),
                    pl.BlockSpec((None, None, ncols, dk), lambda b, gi, qb: (b, gi, 0, 0)),
                    pl.BlockSpec((ncols, n_s), lambda b, gi, qb: (0, 0))]
        args = [q, k_cmp, v_cmp, overlap]
        aliases = {}
        if o_c is not None:
            in_specs += [pl.BlockSpec(memory_space=pl.ANY), pl.BlockSpec(memory_space=pl.ANY)]
            args += [o_c, sel]
            aliases = {4: 0, 5: 1}
        o_c, sel = pl.pallas_call(
            functools.partial(_nsa_cmp_body, hpg=hpg, n_c=n_c, n_sel=n_sel, qb0=qb0),
            grid=(batch, g, nq),
            in_specs=in_specs,
            out_specs=[oq, pl.BlockSpec((None, None, Q_BLOCK, n_s), lambda b, gi, qb, qb0=qb0: (b, gi, qb0 + qb, 0))],
            out_shape=[jax.ShapeDtypeStruct((n, hd), BF16), jax.ShapeDtypeStruct((batch, g, t, n_s), BF16)],
            input_output_aliases=aliases,
            compiler_params=_params("parallel", "parallel", "parallel"),
            name="nsa_compressed_select",
        )(*args)

    nwb = WINDOW // Q_BLOCK + 1
    kcol, vcol = 4 * g, 5 * g

    def kv_spec(col, j):
        return pl.BlockSpec((Q_BLOCK, dk),
                            lambda b, gi, qb: (b * nqb + jnp.maximum(qb - (nwb - 1) + j, 0), col + gi))

    o_w = pl.pallas_call(
        functools.partial(_nsa_win_body, hpg=hpg, nwb=nwb),
        grid=(batch, g, nqb),
        in_specs=[qspec] + [kv_spec(kcol, j) for j in range(nwb)] + [kv_spec(vcol, j) for j in range(nwb)],
        out_specs=qspec,
        out_shape=jax.ShapeDtypeStruct((n, hd), BF16),
        compiler_params=_params("parallel", "parallel", "parallel"),
        name="nsa_window",
    )(q, *([kvb] * (2 * nwb)))

    tk = min(SEL_KEY_TILE, t)
    nkb = t // tk
    qn = min(SEL_Q_TILE, t)
    nqt = t // qn
    steps = [(qb, kb) for qb in range(nqt) for kb in range((qb * qn + qn - 1) // tk + 1)]
    qb_tab = jnp.asarray([s_[0] for s_ in steps], jnp.int32)
    kb_tab = jnp.asarray([s_[1] for s_ in steps], jnp.int32)
    expand = jnp.asarray(np.arange(n_s)[:, None] == (np.arange(t)[None, :] // SEL_BLOCK), BF16)
    kscol, vscol = 2 * g, 3 * g
    grid_spec = pltpu.PrefetchScalarGridSpec(
        num_scalar_prefetch=2,
        grid=(batch, g, len(steps)),
        in_specs=[
            pl.BlockSpec((qn, gw), lambda b, gi, s_, qt, kt: (b * nqt + qt[s_], gi)),
            pl.BlockSpec((tk, dk), lambda b, gi, s_, qt, kt: (b * nkb + kt[s_], kscol + gi)),
            pl.BlockSpec((tk, dk), lambda b, gi, s_, qt, kt: (b * nkb + kt[s_], vscol + gi)),
            pl.BlockSpec((None, None, qn, n_s), lambda b, gi, s_, qt, kt: (b, gi, qt[s_], 0)),
            pl.BlockSpec((n_s, tk), lambda b, gi, s_, qt, kt: (0, kt[s_])),
        ],
        out_specs=pl.BlockSpec((qn, gw), lambda b, gi, s_, qt, kt: (b * nqt + qt[s_], gi)),
        scratch_shapes=[pltpu.VMEM((hpg * qn, LANES), F32), pltpu.VMEM((hpg * qn, dk + LANES), F32)],
    )
    o_s = pl.pallas_call(
        functools.partial(_nsa_sel_body, hpg=hpg),
        grid_spec=grid_spec,
        out_shape=jax.ShapeDtypeStruct((n, hd), BF16),
        compiler_params=_params("parallel", "parallel", "arbitrary"),
        name="nsa_selected",
    )(qb_tab, kb_tab, q, kvb, kvb, sel, expand)
    return o_c, o_s, o_w


def _nsa_shared_kv(xb, batch, w_kv, cmp_pe, cmp_w1, cmp_b1, cmp_w2, cmp_b2):
    n, d = xb.shape
    g, dk = NSA_KV_GROUPS, NSA_HEAD_DIM
    t = n // batch
    kv = _matmul(xb, w_kv.astype(BF16), tn=768)
    n_c = t // CMP_STRIDE - 1
    ncp = -(-n_c // LANES) * LANES
    rows = batch * n_c * g
    rows_p = -(-rows // 256) * 256
    outs = []
    for i in range(2):
        z = kv[:, i * g * dk:(i + 1) * g * dk].reshape(batch, t // CMP_STRIDE, CMP_STRIDE, g, dk)
        blk = jnp.concatenate([z[:, :-1], z[:, 1:]], axis=2)
        flat = blk.transpose(0, 1, 3, 2, 4).reshape(rows, CMP_LEN * dk)
        flat = jnp.pad(flat, ((0, rows_p - rows), (0, 0)))
        pe = jnp.broadcast_to(cmp_pe[i][:, None, :], (CMP_LEN, 1, dk)).reshape(CMP_LEN * dk)
        c = _cmp_mlp(flat, pe, cmp_w1[i], cmp_b1[i], cmp_w2[i], cmp_b2[i])[:rows]
        c = c.reshape(batch, n_c, g, dk).transpose(0, 2, 1, 3)
        outs.append(jnp.pad(c, ((0, 0), (0, 0), (0, ncp - n_c), (0, 0))))
    return kv.astype(BF16), outs[0], outs[1], n_c


def _gate_combine_body(oc_ref, os_ref, ow_ref, gl_ref, ex_ref, o_ref):
    gates = jax.nn.sigmoid(gl_ref[...])
    acc = None
    for i, r in enumerate((oc_ref, os_ref, ow_ref)):
        term = _dot_split_const(gates, ex_ref[i]) * r[...].astype(F32)
        acc = term if acc is None else acc + term
    o_ref[...] = acc.astype(BF16)


def _gate_combine(o_c, o_s, o_w, glog, nh, tm=256):
    n, hd = o_c.shape
    dk = hd // nh
    ex = np.zeros((3, LANES, hd), np.float32)
    for i in range(3):
        for h in range(nh):
            ex[i, i * nh + h, h * dk:(h + 1) * dk] = 1.0
    row = pl.BlockSpec((tm, hd), lambda i: (i, 0))
    return pl.pallas_call(
        _gate_combine_body,
        grid=(n // tm,),
        in_specs=[row, row, row, pl.BlockSpec((tm, LANES), lambda i: (i, 0)),
                  pl.BlockSpec((3, LANES, hd), lambda i: (0, 0, 0))],
        out_specs=row,
        out_shape=jax.ShapeDtypeStruct((n, hd), BF16),
        compiler_params=_params("parallel"),
        name="nsa_gate_combine",
    )(o_c, o_s, o_w, glog, jnp.asarray(ex, BF16))


def _nsa_layer(xb, batch, shared, w_in, b_gate, w_o):
    kvb, k_cmp, v_cmp, n_c = shared
    n, d = xb.shape
    hd = w_o.shape[0]
    nh = hd // NSA_HEAD_DIM
    q = _matmul(xb, w_in[:, :hd].astype(BF16), out_dtype=BF16,
                out_scale=NSA_HEAD_DIM ** -0.5 * math.log2(math.e))
    wg = jnp.pad(w_in[:, hd:], ((0, 0), (0, LANES - 3 * nh))).astype(BF16)
    bg = jnp.pad(b_gate, (0, LANES - 3 * nh))
    glog = _matmul(xb, wg, bias=bg)
    o_c, o_s, o_w = _nsa_attention(q, kvb, k_cmp, v_cmp, batch, n_c)
    o = _gate_combine(o_c, o_s, o_w, glog, nh)
    return _matmul(o, w_o.astype(BF16))


def kernel(x, ln_g, ln_b, rw_mu, rw_w_rkv, rw_w0, rw_w1, rw_w2, rw_a0, rw_a1, rw_a2, rw_g1, rw_g2, rw_k_k, rw_k_a, rw_r_k, rw_lnx_g, rw_lnx_b, rw_w_o, nsa_w_kv, nsa_cmp_pe, nsa_cmp_w1, nsa_cmp_b1, nsa_cmp_w2, nsa_cmp_b2, nsa_w_in, nsa_b_gate, nsa_w_o, moe_router_w, moe_router_b, moe_w_gu, moe_b_gu, moe_w_down, moe_b_down):
    batch, t, d = x.shape
    depth = ln_g.shape[0]
    n_a = rw_mu.shape[0]
    alpha = (2 * depth) ** 0.25
    h = x.reshape(batch * t, d)
    hb = None
    shared = None
    for layer in range(depth):
        if layer < n_a:
            i = layer
            mix = _rwkv_time_mix(h, batch, rw_mu[i], rw_w_rkv[i], rw_w0[i], rw_w1[i], rw_w2[i], rw_a0[i], rw_a1[i],
                                 rw_a2[i], rw_g1[i], rw_g2[i], rw_k_k[i], rw_k_a[i], rw_r_k[i], rw_lnx_g[i],
                                 rw_lnx_b[i], rw_w_o[i])
        else:
            if shared is None:
                if hb is None:
                    hb = h.astype(BF16)
                shared = _nsa_shared_kv(hb, batch, nsa_w_kv, nsa_cmp_pe, nsa_cmp_w1, nsa_cmp_b1, nsa_cmp_w2,
                                        nsa_cmp_b2)
            j = layer - n_a
            mix = _nsa_layer(hb, batch, shared, nsa_w_in[j], nsa_b_gate[j], nsa_w_o[j])
        h, hb = _add_ln(h, mix, ln_g[layer, 0], ln_b[layer, 0], alpha)
        y4, gates = _moe_ffn(hb, layer, moe_router_w[layer], moe_router_b[layer], moe_w_gu, moe_b_gu, moe_w_down,
                             moe_b_down)
        h, hb = _combine_ln(h, y4, gates, ln_g[layer, 1], ln_b[layer, 1], alpha)
    return h.reshape(batch, t, d)
```

```python
import functools
import math

import numpy as np
import jax
import jax.numpy as jnp
from jax import lax
from jax.experimental import pallas as pl
from jax.experimental.pallas import tpu as pltpu

F32 = jnp.float32
BF16 = jnp.bfloat16

V7X_VMEM_LIMIT_BYTES = 56 * 1024 * 1024
LANES = 128

LN_EPS = 1e-5
RW_HEAD_DIM = 64
RW_GN_EPS = 64e-5
RW_CHUNK = 64
RW_CHUNKS_PER_STEP = 8
NSA_HEAD_DIM = 128
NSA_KV_GROUPS = 2
CMP_STRIDE = 16
CMP_LEN = 32
SEL_BLOCK = 64
N_SEL = 16
WINDOW = 512
Q_BLOCK = 128
SEL_KEY_TILE = 1024
SEL_Q_TILE = 256
SEL_HEADS_PER_DOT = 2
CMP_HEADS_PER_DOT = 4
CMP_SEG_COLS = 256
N_EXPERTS = 32
TOP_K = 4
SWIGLU_LIMIT = 7.0
SWIGLU_ALPHA = 1.702
MOE_ROW_BLOCK = 1024
MOE_F_TILE = 512
MOE_OUT_TILE = 512
MOE_DMA_SPLIT = 4
NEG_BIG = -1e30


def _params(*sem):
    return pltpu.CompilerParams(dimension_semantics=sem, vmem_limit_bytes=V7X_VMEM_LIMIT_BYTES)


def _bdot(a, b):
    return jnp.dot(a.astype(BF16), b.astype(BF16), preferred_element_type=F32)


def _bdot_nt(a, b):
    return lax.dot_general(a.astype(BF16), b.astype(BF16), (((1,), (1,)), ((), ())),
                           preferred_element_type=F32)


def _dot_const_split(c, x):
    hi = x.astype(BF16)
    lo = (x - hi.astype(F32)).astype(BF16)
    return (jnp.dot(c, hi, preferred_element_type=F32) + jnp.dot(c, lo, preferred_element_type=F32))


def _dot_split_const(x, c):
    hi = x.astype(BF16)
    lo = (x - hi.astype(F32)).astype(BF16)
    return (jnp.dot(hi, c, preferred_element_type=F32) + jnp.dot(lo, c, preferred_element_type=F32))


def _mm_body(a_ref, w_ref, b_ref, o_ref, *, out_scale):
    acc = jnp.dot(a_ref[...], w_ref[...], preferred_element_type=F32) + b_ref[...]
    if out_scale is not None:
        acc = acc * out_scale
    o_ref[...] = acc.astype(o_ref.dtype)


def _matmul(a, w, bias=None, out_dtype=F32, out_scale=None, tm=512, tn=1024):
    m, k = a.shape
    n = w.shape[1]
    tm = min(tm, m)
    tn = min(tn, n)
    assert m % tm == 0 and n % tn == 0, (m, n, tm, tn)
    if bias is None:
        bias = jnp.zeros((1, n), F32)
    return pl.pallas_call(
        functools.partial(_mm_body, out_scale=out_scale),
        grid=(n // tn, m // tm),
        in_specs=[pl.BlockSpec((tm, k), lambda j, i: (i, 0)),
                  pl.BlockSpec((k, tn), lambda j, i: (0, j)),
                  pl.BlockSpec((1, tn), lambda j, i: (0, j))],
        out_specs=pl.BlockSpec((tm, tn), lambda j, i: (i, j)),
        out_shape=jax.ShapeDtypeStruct((m, n), out_dtype),
        compiler_params=_params("parallel", "parallel"),
        name="dense_matmul",
    )(a, w, bias.reshape(1, n).astype(F32))


def _add_ln_body(x_ref, m_ref, g_ref, b_ref, o_ref, ob_ref, *, alpha):
    z = alpha * x_ref[...] + m_ref[...]
    mu = jnp.mean(z, -1, keepdims=True)
    zc = z - mu
    var = jnp.mean(zc * zc, -1, keepdims=True)
    y = zc * lax.rsqrt(var + LN_EPS) * g_ref[...] + b_ref[...]
    o_ref[...] = y
    ob_ref[...] = y.astype(BF16)


def _add_ln(x, mix, g, b, alpha, tm=256):
    n, d = x.shape
    row = pl.BlockSpec((tm, d), lambda i: (i, 0))
    vec = pl.BlockSpec((1, d), lambda i: (0, 0))
    return pl.pallas_call(
        functools.partial(_add_ln_body, alpha=alpha),
        grid=(n // tm,),
        in_specs=[row, row, vec, vec],
        out_specs=[row, row],
        out_shape=[jax.ShapeDtypeStruct((n, d), F32), jax.ShapeDtypeStruct((n, d), BF16)],
        compiler_params=_params("parallel"),
        name="add_layer_norm",
    )(x, mix, g.reshape(1, d), b.reshape(1, d))


def _rw_mix_body(x_ref, last_ref, mu_ref, *o_refs):
    x = x_ref[...]
    prev = pltpu.roll(x, shift=1, axis=0)
    row = lax.broadcasted_iota(jnp.int32, x.shape, 0)
    prev = jnp.where(row == 0, last_ref[0], prev)
    xx = prev - x
    for i, o_ref in enumerate(o_refs):
        o_ref[...] = (x + xx * mu_ref[i:i + 1, :]).astype(BF16)


def _rw_mix(x2, batch, mu, tm=256):
    n, d = x2.shape
    t = n // batch
    nt = t // tm
    last = x2.reshape(batch, nt, tm, d)[:, :, tm - 1, :]
    last = jnp.concatenate([jnp.zeros((batch, 1, d), F32), last[:, :-1]], axis=1).reshape(batch * nt, 1, d)
    row = pl.BlockSpec((tm, d), lambda i: (i, 0))
    return pl.pallas_call(
        _rw_mix_body,
        grid=(n // tm,),
        in_specs=[row, pl.BlockSpec((1, 1, d), lambda i: (i, 0, 0)), pl.BlockSpec((6, d), lambda i: (0, 0))],
        out_specs=[row] * 6,
        out_shape=[jax.ShapeDtypeStruct((n, d), BF16)] * 6,
        compiler_params=_params("parallel"),
        name="rwkv_token_shift",
    )(x2, last, mu)


def _rw_lowrank_body(xw_ref, xa_ref, xg_ref, w1_ref, w2_ref, w0_ref, a1_ref, a2_ref, a0_ref, g1_ref, g2_ref,
                     lw_ref, a_ref, g_ref):
    z = w0_ref[...] + _bdot(jnp.tanh(jnp.dot(xw_ref[...], w1_ref[...], preferred_element_type=F32)), w2_ref[...])
    w_log = jnp.minimum(z, 0.0) - jnp.log(1.0 + jnp.exp(-jnp.abs(z))) - 0.5
    lw_ref[...] = -jnp.exp(w_log)
    za = a0_ref[...] + _bdot(jnp.dot(xa_ref[...], a1_ref[...], preferred_element_type=F32), a2_ref[...])
    a_ref[...] = jax.nn.sigmoid(za)
    hg = jax.nn.sigmoid(jnp.dot(xg_ref[...], g1_ref[...], preferred_element_type=F32))
    g_ref[...] = _bdot(hg, g2_ref[...])


def _pad_rank(w_in, w_out):
    r = w_in.shape[1]
    rp = -(-r // LANES) * LANES
    return (jnp.pad(w_in, ((0, 0), (0, rp - r))).astype(BF16), jnp.pad(w_out, ((0, rp - r), (0, 0))).astype(BF16))


def _rw_lowrank(xw, xa, xg, w0, w1, w2, a0, a1, a2, g1, g2, tm=256):
    n, d = xw.shape
    w1p, w2p = _pad_rank(w1, w2)
    a1p, a2p = _pad_rank(a1, a2)
    g1p, g2p = _pad_rank(g1, g2)
    row = pl.BlockSpec((tm, d), lambda i: (i, 0))
    full = lambda arr: pl.BlockSpec(arr.shape, lambda i: (0, 0))
    w0r, a0r = w0.reshape(1, d), a0.reshape(1, d)
    return pl.pallas_call(
        _rw_lowrank_body,
        grid=(n // tm,),
        in_specs=[row, row, row, full(w1p), full(w2p), full(w0r), full(a1p), full(a2p), full(a0r), full(g1p), full(g2p)],
        out_specs=[row] * 3,
        out_shape=[jax.ShapeDtypeStruct((n, d), F32)] * 3,
        compiler_params=_params("parallel"),
        name="rwkv_lowrank",
    )(xw, xa, xg, w1p, w2p, w0r, a1p, a2p, a0r, g1p, g2p)


def _wkv_body(r_ref, k_ref, v_ref, lw_ref, a_ref, g_ref, kk_ref, ka_ref, rk_ref, lng_ref, lnb_ref,
              o_ref, s_ref, rp_ref, yq_ref, bonus_ref, gs_ref, *, nchunk):
    L = RW_CHUNK
    H2 = 2 * L

    step = pl.program_id(2)
    slot_w = step % 2
    slot_r = 1 - slot_w

    @pl.when(step == 0)
    def _():
        s_ref[...] = jnp.zeros_like(s_ref)
        rp_ref[1] = jnp.zeros(rp_ref.shape[1:], F32)
        yq_ref[1] = jnp.zeros(yq_ref.shape[1:], F32)
        bonus_ref[1] = jnp.zeros(bonus_ref.shape[1:], F32)
        gs_ref[1] = jnp.zeros(gs_ref.shape[1:], F32)

    lane = lax.broadcasted_iota(jnp.int32, (1, LANES), 1)
    mask0 = (lane < RW_HEAD_DIM).astype(F32)
    mask1 = 1.0 - mask0
    ri = lax.broadcasted_iota(jnp.int32, (H2, H2), 0)
    ci = lax.broadcasted_iota(jnp.int32, (H2, H2), 1)
    same_head = (ri // L) == (ci // L)
    strict = (same_head & (ci < ri)).astype(F32)
    incl = (same_head & (ci <= ri)).astype(F32)
    diag16 = ((ri // 16) == (ci // 16)).astype(F32)
    eye = (ri == ci).astype(F32)
    head_ones = same_head.astype(BF16)
    tl = lax.broadcasted_iota(jnp.int32, (L, L), 0)
    sl = lax.broadcasted_iota(jnp.int32, (L, L), 1)
    tri_incl = (sl <= tl).astype(BF16)

    def stack(x):
        return jnp.concatenate([x * mask0, x * mask1], axis=0)

    k_k = kk_ref[...]
    k_a = ka_ref[...]
    r_k = rk_ref[...]

    chunks = range(nchunk)

    rp_prev = [rp_ref[slot_r, c] for c in chunks]
    yq_prev = [yq_ref[slot_r, c] for c in chunks]
    chain = {"s": s_ref[...], "ys": []}

    def chain_step():
        c = len(chain["ys"])
        if c >= nchunk:
            return
        res = _bdot(rp_prev[c], chain["s"]) + yq_prev[c]
        chain["ys"].append(res[:L] + res[L:H2])
        chain["s"] = res[H2:]
        if c == nchunk - 1:
            s_ref[...] = chain["s"]
            y = jnp.concatenate(chain["ys"], axis=0)
            inv_n = 1.0 / RW_HEAD_DIM
            ym = _dot_split_const(y, head_ones) * inv_n
            yc = y - ym
            yv = _dot_split_const(yc * yc, head_ones) * inv_n
            yn = yc * lax.rsqrt(yv + RW_GN_EPS) * lng_ref[...] + lnb_ref[...]
            o_ref[...] = ((yn + bonus_ref[slot_r]) * gs_ref[slot_r]).astype(BF16)

    def each(fn, *lists):
        return [fn(*xs) for xs in zip(*lists)]

    def mm_stage(fn, *lists):
        out = each(fn, *lists)
        chain_step()
        return out

    def rows_of(x):
        return [x[c * L:(c + 1) * L] for c in chunks]

    r_all = r_ref[...]
    k_all = k_ref[...]
    v_all = v_ref[...]
    ag_all = a_ref[...]
    kk_all = k_all * k_k
    ss_all = _dot_split_const(kk_all * kk_all, head_ones)
    kk_all = kk_all / jnp.maximum(jnp.sqrt(ss_all), 1e-12)
    kmod_all = k_all * (1.0 + (ag_all - 1.0) * k_a)
    bv_all = kk_all * ag_all
    lw_c = rows_of(lw_ref[...])
    cl_c = each(lambda lw: _dot_const_split(tri_incl, lw), lw_c)
    last_c = each(lambda cl: cl[L - 1:L, :], cl_c)
    cl_all = jnp.concatenate(cl_c, axis=0)
    clp_all = cl_all - lw_ref[...]
    end_all = jnp.concatenate(each(lambda cl, la: la - cl, cl_c, last_c), axis=0)
    e_neg = jnp.exp(-cl_all)
    e_end = jnp.exp(end_all)
    at_c = rows_of(-kk_all * jnp.exp(clp_all))
    rt_c = rows_of(r_all * jnp.exp(cl_all))
    bt_c = rows_of(bv_all * e_neg)
    kt_c = rows_of(kmod_all * e_neg)
    be_c = rows_of(bv_all * e_end)
    ke_c = rows_of(kmod_all * e_end)
    v_s = each(stack, rows_of(v_all))
    at_s = each(stack, at_c)
    rt_s = each(stack, rt_c)
    gmat = mm_stage(lambda a_, r_, b_, k_: _bdot_nt(jnp.concatenate([a_, r_], axis=0),
                                                    jnp.concatenate([stack(b_), stack(k_)], axis=0)),
                    at_s, rt_s, bt_c, kt_c)
    a_ab = each(lambda gm: gm[:H2, :H2] * strict, gmat)
    a_ak = each(lambda gm: gm[:H2, H2:] * strict, gmat)
    a_rb = each(lambda gm: gm[H2:, :H2] * incl, gmat)
    a_rk = each(lambda gm: gm[H2:, H2:] * incl, gmat)
    dblk = each(lambda a_: a_ * diag16, a_ab)
    off = each(lambda a_, d_: a_ - d_, a_ab, dblk)
    d2 = mm_stage(_bdot, dblk, dblk)
    d4 = mm_stage(_bdot, d2, d2)
    d8 = mm_stage(_bdot, d4, d4)
    dinv = mm_stage(lambda d_, d2_: _bdot(eye + d_, eye + d2_), dblk, d2)
    dinv = mm_stage(lambda di, d4_: _bdot(di, eye + d4_), dinv, d4)
    dinv = mm_stage(lambda di, d8_: _bdot(di, eye + d8_), dinv, d8)
    e1 = mm_stage(_bdot, dinv, off)
    e2 = mm_stage(_bdot, e1, e1)
    minv = mm_stage(lambda e1_, e2_: _bdot(eye + e1_, eye + e2_), e1, e2)
    minv = mm_stage(_bdot, minv, dinv)
    x_ak = mm_stage(_bdot, a_ak, v_s)
    zu = each(lambda mi, a_, x_: _bdot(mi, jnp.concatenate([a_, x_], axis=1)), minv, at_s, x_ak)
    w2 = each(_bdot, a_rb, zu)
    rkv = each(_bdot, a_rk, v_s)
    pq = each(lambda b_, z_: _bdot(stack(b_).T, z_), be_c, zu)
    kv2 = each(lambda k_, v_: _bdot(stack(k_).T, v_), ke_c, v_s)
    while len(chain["ys"]) < nchunk:
        chain_step()
    for c in chunks:
        rp_ref[slot_w, c] = jnp.concatenate([rt_s[c] + w2[c][:, :LANES],
                                             eye * jnp.exp(last_c[c]) + pq[c][:, :LANES]], axis=0)
        yq_ref[slot_w, c] = jnp.concatenate([w2[c][:, LANES:] + rkv[c], pq[c][:, LANES:] + kv2[c]], axis=0)
    bonus_ref[slot_w] = _dot_split_const(r_all * kmod_all * r_k, head_ones) * v_all
    gs_ref[slot_w] = g_ref[...]


def _wkv(r, k, v, lw, a, g, k_k, k_a, r_k, lnx_g, lnx_b, batch):
    n, d = r.shape
    t = n // batch
    nchunk = RW_CHUNKS_PER_STEP
    tb = RW_CHUNK * nchunk
    while t % tb:
        nchunk //= 2
        tb = RW_CHUNK * nchunk
    nt = t // tb
    row_in = pl.BlockSpec((tb, LANES), lambda b, hp, c: (b * nt + jnp.minimum(c, nt - 1), hp))
    row_out = pl.BlockSpec((tb, LANES), lambda b, hp, c: (b * nt + jnp.maximum(c - 1, 0), hp))
    vec = pl.BlockSpec((1, LANES), lambda b, hp, c: (0, hp))
    sq = pltpu.VMEM((2, nchunk, 2 * LANES, LANES), F32)
    blk = pltpu.VMEM((2, tb, LANES), F32)
    vecs = [z.reshape(1, d) for z in (k_k, k_a, r_k, lnx_g, lnx_b)]
    return pl.pallas_call(
        functools.partial(_wkv_body, nchunk=nchunk),
        grid=(batch, d // LANES, nt + 1),
        in_specs=[row_in] * 6 + [vec] * 5,
        out_specs=row_out,
        out_shape=jax.ShapeDtypeStruct((n, d), BF16),
        scratch_shapes=[pltpu.VMEM((LANES, LANES), F32), sq, sq, blk, blk],
        compiler_params=_params("parallel", "parallel", "arbitrary"),
        name="rwkv_chunked_scan",
    )(r, k, v, lw, a, g, *vecs)


def _rwkv_time_mix(x2, batch, mu, w_rkv, w0, w1, w2, a0, a1, a2, g1, g2, k_k, k_a, r_k, lnx_g, lnx_b, w_o):
    xr, xw, xk, xv, xa, xg = _rw_mix(x2, batch, mu)
    r = _matmul(xr, w_rkv[0].astype(BF16))
    k = _matmul(xk, w_rkv[1].astype(BF16))
    v = _matmul(xv, w_rkv[2].astype(BF16))
    lw, a, g = _rw_lowrank(xw, xa, xg, w0, w1, w2, a0, a1, a2, g1, g2)
    z = _wkv(r, k, v, lw, a, g, k_k, k_a, r_k.reshape(-1), lnx_g, lnx_b, batch)
    return _matmul(z, w_o.astype(BF16))


def _router_body(x_ref, w_ref, b_ref, idx_ref, gate_ref):
    logits = jnp.dot(x_ref[...], w_ref[...], preferred_element_type=F32) + b_ref[...]
    lane = lax.broadcasted_iota(jnp.int32, logits.shape, 1).astype(F32)
    cur = logits
    vals, idxs = [], []
    for _ in range(TOP_K):
        m = jnp.max(cur, axis=-1, keepdims=True)
        i = jnp.min(jnp.where(cur == m, lane, float(LANES)), axis=-1, keepdims=True)
        vals.append(m)
        idxs.append(i)
        cur = jnp.where(lane == i, -3e38, cur)
    es = [jnp.exp(vv - vals[0]) for vv in vals]
    den = es[0]
    for e in es[1:]:
        den = den + e
    idx_out = jnp.zeros(logits.shape, F32)
    gate_out = jnp.zeros(logits.shape, F32)
    for kk in range(TOP_K):
        idx_out = jnp.where(lane == kk, idxs[kk], idx_out)
        gate_out = jnp.where(lane == kk, es[kk] / den, gate_out)
    idx_ref[...] = idx_out.astype(jnp.int32)
    gate_ref[...] = gate_out


def _router(xb, router_w, router_b, tm=512):
    n, d = xb.shape
    e = router_w.shape[1]
    wp = jnp.pad(router_w, ((0, 0), (0, LANES - e))).astype(BF16)
    bp = jnp.concatenate([router_b.astype(F32), jnp.full((LANES - e,), NEG_BIG, F32)]).reshape(1, LANES)
    row = pl.BlockSpec((tm, LANES), lambda i: (i, 0))
    idx, gate = pl.pallas_call(
        _router_body,
        grid=(n // tm,),
        in_specs=[pl.BlockSpec((tm, d), lambda i: (i, 0)), pl.BlockSpec((d, LANES), lambda i: (0, 0)),
                  pl.BlockSpec((1, LANES), lambda i: (0, 0))],
        out_specs=[row, row],
        out_shape=[jax.ShapeDtypeStruct((n, LANES), jnp.int32), jax.ShapeDtypeStruct((n, LANES), F32)],
        compiler_params=_params("parallel"),
        name="moe_router",
    )(xb, wp, bp)
    return idx[:, :TOP_K], gate


def _moe_expert_body(be_ref, nu_ref, x_ref, *refs, nf, nsplit):
    wg_refs, wu_refs = refs[:nsplit], refs[nsplit:2 * nsplit]
    bg_ref, bu_ref = refs[2 * nsplit:2 * nsplit + 2]
    wd_refs = refs[2 * nsplit + 2:3 * nsplit + 2]
    bd_ref, o_ref, act_ref = refs[3 * nsplit + 2:]
    s = pl.program_id(1)
    used = pl.program_id(0) < nu_ref[0]

    def tile(parts):
        return jnp.concatenate([r[...].astype(BF16) for r in parts], axis=0)

    @pl.when(used & (s < nf))
    def _():
        x = x_ref[...]
        gate = jnp.dot(x, tile(wg_refs), preferred_element_type=F32) + bg_ref[...]
        up = jnp.dot(x, tile(wu_refs), preferred_element_type=F32) + bu_ref[...]
        gate = jnp.minimum(gate, SWIGLU_LIMIT)
        up = jnp.clip(up, -SWIGLU_LIMIT, SWIGLU_LIMIT)
        act = (up + 1.0) * (gate * jax.nn.sigmoid(gate * SWIGLU_ALPHA))
        act_ref[jnp.minimum(s, nf - 1)] = act.astype(BF16)

    @pl.when(used & (s >= nf))
    def _():
        act = jnp.concatenate([act_ref[f] for f in range(nf)], axis=1)
        y = jnp.dot(act, tile(wd_refs), preferred_element_type=F32) + bd_ref[...]
        o_ref[...] = y.astype(o_ref.dtype)

    @pl.when(jnp.logical_not(used) & (s >= nf))
    def _():
        o_ref[...] = jnp.zeros_like(o_ref)


def _moe_experts(xs, blk_e, n_used, layer, w_gu, b_gu, w_down, b_down):
    n_rows, d = xs.shape
    depth, ne, _, f2 = w_gu.shape
    fdim = f2 // 2
    tm, tf = MOE_ROW_BLOCK, MOE_F_TILE
    nf = fdim // tf
    n_blk = n_rows // tm
    tn = MOE_OUT_TILE
    nn = d // tn

    def fidx(s):
        return jnp.minimum(s, nf - 1)

    def nidx(s):
        return jnp.maximum(s - nf, 0)

    ns = MOE_DMA_SPLIT

    def slab(rows, cols, col_of, j):
        return pl.BlockSpec((None, None, rows, cols), lambda i, s, be, nu: (layer, be[i], j, col_of(s)))

    grid_spec = pltpu.PrefetchScalarGridSpec(
        num_scalar_prefetch=2,
        grid=(n_blk, nf + nn),
        in_specs=(
            [pl.BlockSpec((tm, d), lambda i, s, be, nu: (jnp.minimum(i, nu[0] - 1), 0))]
            + [slab(d // ns, tf, fidx, j) for j in range(ns)]
            + [slab(d // ns, tf, lambda s: nf + fidx(s), j) for j in range(ns)]
            + [pl.BlockSpec((None, None, 1, tf), lambda i, s, be, nu: (layer, be[i], 0, fidx(s))),
               pl.BlockSpec((None, None, 1, tf), lambda i, s, be, nu: (layer, be[i], 0, nf + fidx(s)))]
            + [slab(fdim // ns, tn, nidx, j) for j in range(ns)]
            + [pl.BlockSpec((None, None, 1, tn), lambda i, s, be, nu: (layer, be[i], 0, nidx(s)))]
        ),
        out_specs=pl.BlockSpec((tm, tn), lambda i, s, be, nu: (i, nidx(s))),
        scratch_shapes=[pltpu.VMEM((nf, tm, tf), BF16)],
    )
    bgu = b_gu.reshape(depth, ne, 1, f2)
    return pl.pallas_call(
        functools.partial(_moe_expert_body, nf=nf, nsplit=ns),
        grid_spec=grid_spec,
        out_shape=jax.ShapeDtypeStruct((n_rows, d), BF16),
        compiler_params=_params("arbitrary", "arbitrary"),
        name="moe_experts",
    )(blk_e, n_used, xs, *([w_gu] * (2 * ns)), bgu, bgu, *([w_down] * ns), b_down.reshape(depth, ne, 1, d))


def _combine_ln_body(x_ref, y_ref, gate_ref, g_ref, b_ref, o_ref, ob_ref, *, alpha):
    ffn = y_ref[0].astype(F32) * gate_ref[:, 0:1]
    for kk in range(1, TOP_K):
        ffn = ffn + y_ref[kk].astype(F32) * gate_ref[:, kk:kk + 1]
    z = alpha * x_ref[...] + ffn
    mu = jnp.mean(z, -1, keepdims=True)
    zc = z - mu
    var = jnp.mean(zc * zc, -1, keepdims=True)
    y = zc * lax.rsqrt(var + LN_EPS) * g_ref[...] + b_ref[...]
    o_ref[...] = y
    ob_ref[...] = y.astype(BF16)


def _combine_ln(x, y4, gates, g, b, alpha, tm=256):
    n, d = x.shape
    row = pl.BlockSpec((tm, d), lambda i: (i, 0))
    vec = pl.BlockSpec((1, d), lambda i: (0, 0))
    return pl.pallas_call(
        functools.partial(_combine_ln_body, alpha=alpha),
        grid=(n // tm,),
        in_specs=[row, pl.BlockSpec((TOP_K, tm, d), lambda i: (0, i, 0)),
                  pl.BlockSpec((tm, LANES), lambda i: (i, 0)), vec, vec],
        out_specs=[row, row],
        out_shape=[jax.ShapeDtypeStruct((n, d), F32), jax.ShapeDtypeStruct((n, d), BF16)],
        compiler_params=_params("parallel"),
        name="moe_combine_layer_norm",
    )(x, y4, gates, g.reshape(1, d), b.reshape(1, d))


def _moe_ffn(xb, layer, router_w, router_b, w_gu, b_gu, w_down, b_down):
    n, d = xb.shape
    tm = MOE_ROW_BLOCK
    top_i, gates = _router(xb, router_w, router_b)
    flat_e = top_i.reshape(-1)
    onehot = (flat_e[:, None] == jnp.arange(N_EXPERTS, dtype=jnp.int32)[None, :]).astype(jnp.int32)
    csum = jnp.cumsum(onehot, axis=0)
    rank = jnp.take_along_axis(csum, flat_e[:, None], axis=1)[:, 0] - 1
    counts = csum[-1]
    padded = ((counts + tm - 1) // tm) * tm
    pad_end = jnp.cumsum(padded)
    pad_start = pad_end - padded
    dest = pad_start[flat_e] + rank
    n_rows = -(-(n * TOP_K) // tm) * tm + N_EXPERTS * tm
    n_blk = n_rows // tm
    blk_start = jnp.arange(n_blk, dtype=jnp.int32) * tm
    blk_e = jnp.minimum(jnp.sum((pad_end[None, :] <= blk_start[:, None]).astype(jnp.int32), axis=1),
                        N_EXPERTS - 1).astype(jnp.int32)
    n_used = (pad_end[-1] // tm).astype(jnp.int32).reshape(1)
    flat_tok = jnp.arange(n * TOP_K, dtype=jnp.int32) // TOP_K
    row_tok = (jnp.arange(n_rows, dtype=jnp.int32) % n).at[dest].set(flat_tok)
    xs = jnp.take(xb, row_tok, axis=0, mode='clip')
    ys = _moe_experts(xs, blk_e, n_used, layer, w_gu, b_gu, w_down, b_down)
    dest_kmajor = dest.reshape(n, TOP_K).T.reshape(-1)
    y4 = jnp.take(ys, dest_kmajor, axis=0, mode='clip').reshape(TOP_K, n, d)
    return y4, gates


def _cmp_mlp_body(x_ref, pe_ref, w1_ref, b1_ref, w2_ref, b2_ref, o_ref):
    h = _bdot(x_ref[...] + pe_ref[...], w1_ref[...]) + b1_ref[...]
    h = jax.nn.gelu(h)
    o_ref[...] = (_bdot(h, w2_ref[...]) + b2_ref[...]).astype(o_ref.dtype)


def _cmp_mlp(flat, pe, w1, b1, w2, b2, tm=256):
    m, kd = flat.shape
    hid = w1.shape[1]
    dk = w2.shape[1]
    full = lambda shp: pl.BlockSpec(shp, lambda i: (0, 0))
    return pl.pallas_call(
        _cmp_mlp_body,
        grid=(m // tm,),
        in_specs=[pl.BlockSpec((tm, kd), lambda i: (i, 0)), full((1, kd)), full((kd, hid)), full((1, hid)),
                  full((hid, dk)), full((1, dk))],
        out_specs=pl.BlockSpec((tm, dk), lambda i: (i, 0)),
        out_shape=jax.ShapeDtypeStruct((m, dk), BF16),
        compiler_params=_params("parallel"),
        name="nsa_compress_mlp",
    )(flat, pe.reshape(1, kd), w1.astype(BF16), b1.reshape(1, hid), w2.astype(BF16), b2.reshape(1, dk))


def _group_rows(q_ref, hpg):
    dk = NSA_HEAD_DIM
    return jnp.concatenate([q_ref[:, h * dk:(h + 1) * dk] for h in range(hpg)], axis=0)


def _softmax_rows(s, mask):
    s = jnp.where(mask, s, NEG_BIG)
    m = jnp.max(s, axis=-1, keepdims=True)
    e = jnp.where(mask, jnp.exp2(s - m), 0.0)
    den = jnp.sum(e, axis=-1, keepdims=True)
    return e / jnp.where(den > 0, den, 1.0)


def _exp2_rows(s_rows, bias, cols):
    s_h = [s_rows[:, cj] + bias[:, cj] for cj in cols]
    mx = s_h[0]
    for s_hj in s_h[1:]:
        mx = jnp.maximum(mx, s_hj)
    m = jnp.broadcast_to(jnp.max(mx, axis=-1, keepdims=True), mx.shape)
    return jnp.concatenate([jnp.exp2(s_hj - m).astype(BF16) for s_hj in s_h], axis=1)


def _nsa_cmp_body(q_ref, kc_ref, vc_ref, ov_ref, *rest, hpg, n_c, n_sel, qb0):
    o_ref, sel_ref = rest[-2:]
    qb = pl.program_id(2) + qb0
    t0 = qb * Q_BLOCK
    ncols = kc_ref.shape[0]
    n_s = ov_ref.shape[1]
    dk = NSA_HEAD_DIM
    tq_c = t0 + lax.broadcasted_iota(jnp.int32, (Q_BLOCK, ncols), 0)
    cid = lax.broadcasted_iota(jnp.int32, (Q_BLOCK, ncols), 1)
    bias = jnp.where((cid * CMP_STRIDE + (CMP_LEN - 1) <= tq_c) & (cid < n_c), 0.0, NEG_BIG)
    t_row = t0 + lax.broadcasted_iota(jnp.int32, (Q_BLOCK, LANES), 0)
    row_live = t_row >= CMP_LEN - 1
    cols = [slice(j * LANES, (j + 1) * LANES) for j in range(ncols // LANES)]
    k_tile = kc_ref[...]
    v_aug = jnp.concatenate([vc_ref[...], jnp.ones((ncols, LANES), BF16)], axis=1)
    ov = ov_ref[...]
    hg = min(CMP_HEADS_PER_DOT, hpg)
    imp = jnp.zeros((Q_BLOCK, n_s), F32)
    for g0 in range(0, hpg, hg):
        qg = jnp.concatenate([q_ref[:, h * dk:(h + 1) * dk] for h in range(g0, g0 + hg)], axis=0)
        s_g = _bdot_nt(qg, k_tile)
        e = jnp.concatenate([_exp2_rows(s_g[hl * Q_BLOCK:(hl + 1) * Q_BLOCK], bias, cols) for hl in range(hg)], axis=0)
        od = jnp.dot(e, v_aug, preferred_element_type=F32)
        ih = jnp.dot(e, ov, preferred_element_type=F32)
        for hl in range(hg):
            rows = slice(hl * Q_BLOCK, (hl + 1) * Q_BLOCK)
            inv = jnp.where(row_live, 1.0 / od[rows, dk:], 0.0)
            o_ref[:, (g0 + hl) * dk:(g0 + hl + 1) * dk] = (od[rows, :dk] * inv).astype(o_ref.dtype)
            inv_s = inv[:, :n_s] if n_s <= LANES else jnp.concatenate([inv] * (n_s // LANES), axis=1)
            imp = imp + ih[rows] * inv_s
    imp_t = imp.T
    tq = t0 + lax.broadcasted_iota(jnp.int32, (n_s, Q_BLOCK), 1)
    sid_i = lax.broadcasted_iota(jnp.int32, (n_s, Q_BLOCK), 0)
    cur = tq // SEL_BLOCK
    forced = (sid_i == 0) | (sid_i == cur) | (sid_i == cur - 1)
    score = jnp.where(sid_i * SEL_BLOCK <= tq, jnp.where(forced, 1e30, imp_t), -1.0)
    sid = sid_i.astype(F32)
    sel = jnp.zeros((n_s, Q_BLOCK), F32)
    for _ in range(n_sel):
        m = jnp.max(score, axis=0, keepdims=True)
        first = jnp.min(jnp.where(score == m, sid, float(n_s)), axis=0, keepdims=True)
        hit = sid == first
        sel = jnp.where(hit, 1.0, sel)
        score = jnp.where(hit, -2.0, score)
    sel_ref[...] = sel.T.astype(BF16)


def _nsa_win_body(q_ref, *refs, hpg, nwb):
    k_refs, v_refs, o_ref = refs[:nwb], refs[nwb:2 * nwb], refs[2 * nwb]
    qb = pl.program_id(2)
    t0 = qb * Q_BLOCK
    dk = NSA_HEAD_DIM
    nk = nwb * Q_BLOCK
    kcat = jnp.concatenate([r[...] for r in k_refs], axis=0)
    v_aug = jnp.concatenate([r[...] for r in v_refs], axis=0)
    v_aug = jnp.concatenate([v_aug, jnp.ones((nk, LANES), BF16)], axis=1)
    t = t0 + lax.broadcasted_iota(jnp.int32, (Q_BLOCK, nk), 0)
    kpos = t0 - WINDOW + lax.broadcasted_iota(jnp.int32, (Q_BLOCK, nk), 1)
    bias = jnp.where((kpos <= t) & (kpos > t - WINDOW) & (kpos >= 0), 0.0, NEG_BIG)
    cols = [slice(j * LANES, (j + 1) * LANES) for j in range(nk // LANES)]
    hg = min(CMP_HEADS_PER_DOT, hpg)
    for g0 in range(0, hpg, hg):
        qg = jnp.concatenate([q_ref[:, h * dk:(h + 1) * dk] for h in range(g0, g0 + hg)], axis=0)
        s_g = _bdot_nt(qg, kcat)
        e = jnp.concatenate([_exp2_rows(s_g[hl * Q_BLOCK:(hl + 1) * Q_BLOCK], bias, cols) for hl in range(hg)], axis=0)
        od = jnp.dot(e, v_aug, preferred_element_type=F32)
        for hl in range(hg):
            rows = slice(hl * Q_BLOCK, (hl + 1) * Q_BLOCK)
            o_ref[:, (g0 + hl) * dk:(g0 + hl + 1) * dk] = (od[rows, :dk] / od[rows, dk:]).astype(o_ref.dtype)


def _nsa_sel_body(qb_ref, kb_ref, q_ref, k_ref, v_ref, sel_ref, ex_ref, o_ref, m_ref, acc_ref, *, hpg):
    step = pl.program_id(2)
    qb = qb_ref[step]
    kb = kb_ref[step]
    qn = q_ref.shape[0]
    t0 = qb * qn
    tk = k_ref.shape[0]
    dk = NSA_HEAD_DIM

    @pl.when(kb == 0)
    def _():
        m_ref[...] = jnp.full_like(m_ref, NEG_BIG)
        acc_ref[...] = jnp.zeros_like(acc_ref)

    picked = jnp.dot(sel_ref[...], ex_ref[...], preferred_element_type=F32)
    tq = t0 + lax.broadcasted_iota(jnp.int32, (qn, tk), 0)
    kpos = kb * tk + lax.broadcasted_iota(jnp.int32, (qn, tk), 1)
    bias = jnp.where((picked > 0.5) & (kpos <= tq), 0.0, NEG_BIG)
    cols = [slice(j * LANES, (j + 1) * LANES) for j in range(tk // LANES)]
    k_tile = k_ref[...]
    v_aug = jnp.concatenate([v_ref[...], jnp.ones((tk, LANES), BF16)], axis=1)
    hg = SEL_HEADS_PER_DOT

    def scores(g0):
        qg = jnp.concatenate([q_ref[:, h * dk:(h + 1) * dk] for h in range(g0, g0 + hg)], axis=0)
        return _bdot_nt(qg, k_tile)

    s_next = scores(0)
    for g0 in range(0, hpg, hg):
        s_g = s_next
        if g0 + hg < hpg:
            s_next = scores(g0 + hg)
        p_rows, alphas = [], []
        for hl in range(hg):
            rows = slice((g0 + hl) * qn, (g0 + hl + 1) * qn)
            s_h = [s_g[hl * qn:(hl + 1) * qn, cj] + bias[:, cj] for cj in cols]
            mx = s_h[0]
            for s_hj in s_h[1:]:
                mx = jnp.maximum(mx, s_hj)
            m_old = m_ref[rows, :]
            m_new = jnp.maximum(m_old, jnp.broadcast_to(jnp.max(mx, axis=-1, keepdims=True), m_old.shape))
            m_ref[rows, :] = m_new
            alphas.append(jnp.exp2(m_old - m_new))
            p_rows.append(jnp.concatenate([jnp.exp2(s_hj - m_new).astype(BF16) for s_hj in s_h], axis=1))
        pv = jnp.dot(jnp.concatenate(p_rows, axis=0), v_aug, preferred_element_type=F32)
        alpha = jnp.concatenate(alphas, axis=0)
        grows = slice(g0 * qn, (g0 + hg) * qn)
        acc_ref[grows, :dk] = alpha * acc_ref[grows, :dk] + pv[:, :dk]
        acc_ref[grows, dk:] = alpha * acc_ref[grows, dk:] + pv[:, dk:]

    @pl.when(kb == (t0 + qn - 1) // tk)
    def _():
        den = acc_ref[:, dk:]
        o = acc_ref[:, :dk] / jnp.where(den > 0, den, 1.0)
        for h in range(hpg):
            o_ref[:, h * dk:(h + 1) * dk] = o[h * qn:(h + 1) * qn].astype(o_ref.dtype)


def _nsa_attention(q, kvb, k_cmp, v_cmp, batch, n_c):
    n, hd = q.shape
    dk, g = NSA_HEAD_DIM, NSA_KV_GROUPS
    hpg = hd // dk // g
    t = n // batch
    nqb = t // Q_BLOCK
    n_s = t // SEL_BLOCK
    n_sel = min(N_SEL, n_s)
    ncp = k_cmp.shape[2]
    gw = hpg * dk

    c_lo = np.arange(ncp) * CMP_STRIDE
    s_lo = np.arange(n_s) * SEL_BLOCK
    overlap = ((c_lo[:, None] < s_lo[None, :] + SEL_BLOCK) & (c_lo[:, None] + CMP_LEN > s_lo[None, :])
               & (np.arange(ncp)[:, None] < n_c))
    overlap = jnp.asarray(overlap, BF16)

    qspec = pl.BlockSpec((Q_BLOCK, gw), lambda b, gi, qb: (b * nqb + qb, gi))
    seg_qb = CMP_SEG_COLS * CMP_STRIDE // Q_BLOCK
    o_c = sel = None
    for qb0 in range(0, nqb, seg_qb):
        nq = min(seg_qb, nqb - qb0)
        ncols = min(ncp, -(-((qb0 + nq) * Q_BLOCK // CMP_STRIDE) // LANES) * LANES)
        oq = pl.BlockSpec((Q_BLOCK, gw), lambda b, gi, qb, qb0=qb0: (b * nqb + qb0 + qb, gi))
        in_specs = [oq,
                    pl.BlockSpec((None, None, ncols, dk), lambda b, gi, qb: (b, gi, 0, 0)),
                    pl.BlockSpec((None, None, ncols, dk), lambda b, gi, qb: (b, gi, 0, 0)),
                    pl.BlockSpec((ncols, n_s), lambda b, gi, qb: (0, 0))]
        args = [q, k_cmp, v_cmp, overlap]
        aliases = {}
        if o_c is not None:
            in_specs += [pl.BlockSpec(memory_space=pl.ANY), pl.BlockSpec(memory_space=pl.ANY)]
            args += [o_c, sel]
            aliases = {4: 0, 5: 1}
        o_c, sel = pl.pallas_call(
            functools.partial(_nsa_cmp_body, hpg=hpg, n_c=n_c, n_sel=n_sel, qb0=qb0),
            grid=(batch, g, nq),
            in_specs=in_specs,
            out_specs=[oq, pl.BlockSpec((None, None, Q_BLOCK, n_s), lambda b, gi, qb, qb0=qb0: (b, gi, qb0 + qb, 0))],
            out_shape=[jax.ShapeDtypeStruct((n, hd), BF16), jax.ShapeDtypeStruct((batch, g, t, n_s), BF16)],
            input_output_aliases=aliases,
            compiler_params=_params("parallel", "parallel", "parallel"),
            name="nsa_compressed_select",
        )(*args)

    nwb = WINDOW // Q_BLOCK + 1
    kcol, vcol = 4 * g, 5 * g

    def kv_spec(col, j):
        return pl.BlockSpec((Q_BLOCK, dk),
                            lambda b, gi, qb: (b * nqb + jnp.maximum(qb - (nwb - 1) + j, 0), col + gi))

    o_w = pl.pallas_call(
        functools.partial(_nsa_win_body, hpg=hpg, nwb=nwb),
        grid=(batch, g, nqb),
        in_specs=[qspec] + [kv_spec(kcol, j) for j in range(nwb)] + [kv_spec(vcol, j) for j in range(nwb)],
        out_specs=qspec,
        out_shape=jax.ShapeDtypeStruct((n, hd), BF16),
        compiler_params=_params("parallel", "parallel", "parallel"),
        name="nsa_window",
    )(q, *([kvb] * (2 * nwb)))

    tk = min(SEL_KEY_TILE, t)
    nkb = t // tk
    qn = min(SEL_Q_TILE, t)
    nqt = t // qn
    steps = [(qb, kb) for qb in range(nqt) for kb in range((qb * qn + qn - 1) // tk + 1)]
    qb_tab = jnp.asarray([s_[0] for s_ in steps], jnp.int32)
    kb_tab = jnp.asarray([s_[1] for s_ in steps], jnp.int32)
    expand = jnp.asarray(np.arange(n_s)[:, None] == (np.arange(t)[None, :] // SEL_BLOCK), BF16)
    kscol, vscol = 2 * g, 3 * g
    grid_spec = pltpu.PrefetchScalarGridSpec(
        num_scalar_prefetch=2,
        grid=(batch, g, len(steps)),
        in_specs=[
            pl.BlockSpec((qn, gw), lambda b, gi, s_, qt, kt: (b * nqt + qt[s_], gi)),
            pl.BlockSpec((tk, dk), lambda b, gi, s_, qt, kt: (b * nkb + kt[s_], kscol + gi)),
            pl.BlockSpec((tk, dk), lambda b, gi, s_, qt, kt: (b * nkb + kt[s_], vscol + gi)),
            pl.BlockSpec((None, None, qn, n_s), lambda b, gi, s_, qt, kt: (b, gi, qt[s_], 0)),
            pl.BlockSpec((n_s, tk), lambda b, gi, s_, qt, kt: (0, kt[s_])),
        ],
        out_specs=pl.BlockSpec((qn, gw), lambda b, gi, s_, qt, kt: (b * nqt + qt[s_], gi)),
        scratch_shapes=[pltpu.VMEM((hpg * qn, LANES), F32), pltpu.VMEM((hpg * qn, dk + LANES), F32)],
    )
    o_s = pl.pallas_call(
        functools.partial(_nsa_sel_body, hpg=hpg),
        grid_spec=grid_spec,
        out_shape=jax.ShapeDtypeStruct((n, hd), BF16),
        compiler_params=_params("parallel", "parallel", "arbitrary"),
        name="nsa_selected",
    )(qb_tab, kb_tab, q, kvb, kvb, sel, expand)
    return o_c, o_s, o_w


def _nsa_shared_kv(xb, batch, w_kv, cmp_pe, cmp_w1, cmp_b1, cmp_w2, cmp_b2):
    n, d = xb.shape
    g, dk = NSA_KV_GROUPS, NSA_HEAD_DIM
    t = n // batch
    kv = _matmul(xb, w_kv.astype(BF16), tn=768)
    n_c = t // CMP_STRIDE - 1
    ncp = -(-n_c // LANES) * LANES
    rows = batch * n_c * g
    rows_p = -(-rows // 256) * 256
    outs = []
    for i in range(2):
        z = kv[:, i * g * dk:(i + 1) * g * dk].reshape(batch, t // CMP_STRIDE, CMP_STRIDE, g, dk)
        blk = jnp.concatenate([z[:, :-1], z[:, 1:]], axis=2)
        flat = blk.transpose(0, 1, 3, 2, 4).reshape(rows, CMP_LEN * dk)
        flat = jnp.pad(flat, ((0, rows_p - rows), (0, 0)))
        pe = jnp.broadcast_to(cmp_pe[i][:, None, :], (CMP_LEN, 1, dk)).reshape(CMP_LEN * dk)
        c = _cmp_mlp(flat, pe, cmp_w1[i], cmp_b1[i], cmp_w2[i], cmp_b2[i])[:rows]
        c = c.reshape(batch, n_c, g, dk).transpose(0, 2, 1, 3)
        outs.append(jnp.pad(c, ((0, 0), (0, 0), (0, ncp - n_c), (0, 0))))
    return kv.astype(BF16), outs[0], outs[1], n_c


def _gate_combine_body(oc_ref, os_ref, ow_ref, gl_ref, ex_ref, o_ref):
    gates = jax.nn.sigmoid(gl_ref[...])
    acc = None
    for i, r in enumerate((oc_ref, os_ref, ow_ref)):
        term = _dot_split_const(gates, ex_ref[i]) * r[...].astype(F32)
        acc = term if acc is None else acc + term
    o_ref[...] = acc.astype(BF16)


def _gate_combine(o_c, o_s, o_w, glog, nh, tm=256):
    n, hd = o_c.shape
    dk = hd // nh
    ex = np.zeros((3, LANES, hd), np.float32)
    for i in range(3):
        for h in range(nh):
            ex[i, i * nh + h, h * dk:(h + 1) * dk] = 1.0
    row = pl.BlockSpec((tm, hd), lambda i: (i, 0))
    return pl.pallas_call(
        _gate_combine_body,
        grid=(n // tm,),
        in_specs=[row, row, row, pl.BlockSpec((tm, LANES), lambda i: (i, 0)),
                  pl.BlockSpec((3, LANES, hd), lambda i: (0, 0, 0))],
        out_specs=row,
        out_shape=jax.ShapeDtypeStruct((n, hd), BF16),
        compiler_params=_params("parallel"),
        name="nsa_gate_combine",
    )(o_c, o_s, o_w, glog, jnp.asarray(ex, BF16))


def _nsa_layer(xb, batch, shared, w_in, b_gate, w_o):
    kvb, k_cmp, v_cmp, n_c = shared
    n, d = xb.shape
    hd = w_o.shape[0]
    nh = hd // NSA_HEAD_DIM
    q = _matmul(xb, w_in[:, :hd].astype(BF16), out_dtype=BF16,
                out_scale=NSA_HEAD_DIM ** -0.5 * math.log2(math.e))
    wg = jnp.pad(w_in[:, hd:], ((0, 0), (0, LANES - 3 * nh))).astype(BF16)
    bg = jnp.pad(b_gate, (0, LANES - 3 * nh))
    glog = _matmul(xb, wg, bias=bg)
    o_c, o_s, o_w = _nsa_attention(q, kvb, k_cmp, v_cmp, batch, n_c)
    o = _gate_combine(o_c, o_s, o_w, glog, nh)
    return _matmul(o, w_o.astype(BF16))


def kernel(x, ln_g, ln_b, rw_mu, rw_w_rkv, rw_w0, rw_w1, rw_w2, rw_a0, rw_a1, rw_a2, rw_g1, rw_g2, rw_k_k, rw_k_a, rw_r_k, rw_lnx_g, rw_lnx_b, rw_w_o, nsa_w_kv, nsa_cmp_pe, nsa_cmp_w1, nsa_cmp_b1, nsa_cmp_w2, nsa_cmp_b2, nsa_w_in, nsa_b_gate, nsa_w_o, moe_router_w, moe_router_b, moe_w_gu, moe_b_gu, moe_w_down, moe_b_down):
    batch, t, d = x.shape
    depth = ln_g.shape[0]
    n_a = rw_mu.shape[0]
    alpha = (2 * depth) ** 0.25
    h = x.reshape(batch * t, d)
    hb = None
    shared = None
    for layer in range(depth):
        if layer < n_a:
            i = layer
            mix = _rwkv_time_mix(h, batch, rw_mu[i], rw_w_rkv[i], rw_w0[i], rw_w1[i], rw_w2[i], rw_a0[i], rw_a1[i],
                                 rw_a2[i], rw_g1[i], rw_g2[i], rw_k_k[i], rw_k_a[i], rw_r_k[i], rw_lnx_g[i],
                                 rw_lnx_b[i], rw_w_o[i])
        else:
            if shared is None:
                if hb is None:
                    hb = h.astype(BF16)
                shared = _nsa_shared_kv(hb, batch, nsa_w_kv, nsa_cmp_pe, nsa_cmp_w1, nsa_cmp_b1, nsa_cmp_w2,
                                        nsa_cmp_b2)
            j = layer - n_a
            mix = _nsa_layer(hb, batch, shared, nsa_w_in[j], nsa_b_gate[j], nsa_w_o[j])
        h, hb = _add_ln(h, mix, ln_g[layer, 0], ln_b[layer, 0], alpha)
        y4, gates = _moe_ffn(hb, layer, moe_router_w[layer], moe_router_b[layer], moe_w_gu, moe_b_gu, moe_w_down,
                             moe_b_down)
        h, hb = _combine_ln(h, y4, gates, ln_g[layer, 1], ln_b[layer, 1], alpha)
    return h.reshape(batch, t, d)
```

```python
import functools
import math

import numpy as np
import jax
import jax.numpy as jnp
from jax import lax
from jax.experimental import pallas as pl
from jax.experimental.pallas import tpu as pltpu

F32 = jnp.float32
BF16 = jnp.bfloat16

V7X_VMEM_LIMIT_BYTES = 56 * 1024 * 1024
LANES = 128

LN_EPS = 1e-5
RW_HEAD_DIM = 64
RW_GN_EPS = 64e-5
RW_CHUNK = 64
RW_CHUNKS_PER_STEP = 8
NSA_HEAD_DIM = 128
NSA_KV_GROUPS = 2
CMP_STRIDE = 16
CMP_LEN = 32
SEL_BLOCK = 64
N_SEL = 16
WINDOW = 512
Q_BLOCK = 128
SEL_KEY_TILE = 1024
SEL_Q_TILE = 256
SEL_HEADS_PER_DOT = 2
CMP_HEADS_PER_DOT = 4
CMP_SEG_COLS = 256
N_EXPERTS = 32
TOP_K = 4
SWIGLU_LIMIT = 7.0
SWIGLU_ALPHA = 1.702
MOE_ROW_BLOCK = 1024
MOE_F_TILE = 512
MOE_OUT_TILE = 1024
NEG_BIG = -1e30


def _params(*sem):
    return pltpu.CompilerParams(dimension_semantics=sem, vmem_limit_bytes=V7X_VMEM_LIMIT_BYTES)


def _bdot(a, b):
    return jnp.dot(a.astype(BF16), b.astype(BF16), preferred_element_type=F32)


def _bdot_nt(a, b):
    return lax.dot_general(a.astype(BF16), b.astype(BF16), (((1,), (1,)), ((), ())),
                           preferred_element_type=F32)


def _dot_const_split(c, x):
    hi = x.astype(BF16)
    lo = (x - hi.astype(F32)).astype(BF16)
    return (jnp.dot(c, hi, preferred_element_type=F32) + jnp.dot(c, lo, preferred_element_type=F32))


def _dot_split_const(x, c):
    hi = x.astype(BF16)
    lo = (x - hi.astype(F32)).astype(BF16)
    return (jnp.dot(hi, c, preferred_element_type=F32) + jnp.dot(lo, c, preferred_element_type=F32))


def _mm_body(a_ref, w_ref, b_ref, o_ref, *, out_scale):
    acc = jnp.dot(a_ref[...], w_ref[...], preferred_element_type=F32) + b_ref[...]
    if out_scale is not None:
        acc = acc * out_scale
    o_ref[...] = acc.astype(o_ref.dtype)


def _matmul(a, w, bias=None, out_dtype=F32, out_scale=None, tm=512, tn=1024):
    m, k = a.shape
    n = w.shape[1]
    tm = min(tm, m)
    tn = min(tn, n)
    assert m % tm == 0 and n % tn == 0, (m, n, tm, tn)
    if bias is None:
        bias = jnp.zeros((1, n), F32)
    return pl.pallas_call(
        functools.partial(_mm_body, out_scale=out_scale),
        grid=(n // tn, m // tm),
        in_specs=[pl.BlockSpec((tm, k), lambda j, i: (i, 0)),
                  pl.BlockSpec((k, tn), lambda j, i: (0, j)),
                  pl.BlockSpec((1, tn), lambda j, i: (0, j))],
        out_specs=pl.BlockSpec((tm, tn), lambda j, i: (i, j)),
        out_shape=jax.ShapeDtypeStruct((m, n), out_dtype),
        compiler_params=_params("parallel", "parallel"),
        name="dense_matmul",
    )(a, w, bias.reshape(1, n).astype(F32))


def _add_ln_body(x_ref, m_ref, g_ref, b_ref, o_ref, ob_ref, *, alpha):
    z = alpha * x_ref[...] + m_ref[...]
    mu = jnp.mean(z, -1, keepdims=True)
    zc = z - mu
    var = jnp.mean(zc * zc, -1, keepdims=True)
    y = zc * lax.rsqrt(var + LN_EPS) * g_ref[...] + b_ref[...]
    o_ref[...] = y
    ob_ref[...] = y.astype(BF16)


def _add_ln(x, mix, g, b, alpha, tm=256):
    n, d = x.shape
    row = pl.BlockSpec((tm, d), lambda i: (i, 0))
    vec = pl.BlockSpec((1, d), lambda i: (0, 0))
    return pl.pallas_call(
        functools.partial(_add_ln_body, alpha=alpha),
        grid=(n // tm,),
        in_specs=[row, row, vec, vec],
        out_specs=[row, row],
        out_shape=[jax.ShapeDtypeStruct((n, d), F32), jax.ShapeDtypeStruct((n, d), BF16)],
        compiler_params=_params("parallel"),
        name="add_layer_norm",
    )(x, mix, g.reshape(1, d), b.reshape(1, d))


def _rw_mix_body(x_ref, last_ref, mu_ref, *o_refs):
    x = x_ref[...]
    prev = pltpu.roll(x, shift=1, axis=0)
    row = lax.broadcasted_iota(jnp.int32, x.shape, 0)
    prev = jnp.where(row == 0, last_ref[0], prev)
    xx = prev - x
    for i, o_ref in enumerate(o_refs):
        o_ref[...] = (x + xx * mu_ref[i:i + 1, :]).astype(BF16)


def _rw_mix(x2, batch, mu, tm=256):
    n, d = x2.shape
    t = n // batch
    nt = t // tm
    last = x2.reshape(batch, nt, tm, d)[:, :, tm - 1, :]
    last = jnp.concatenate([jnp.zeros((batch, 1, d), F32), last[:, :-1]], axis=1).reshape(batch * nt, 1, d)
    row = pl.BlockSpec((tm, d), lambda i: (i, 0))
    return pl.pallas_call(
        _rw_mix_body,
        grid=(n // tm,),
        in_specs=[row, pl.BlockSpec((1, 1, d), lambda i: (i, 0, 0)), pl.BlockSpec((6, d), lambda i: (0, 0))],
        out_specs=[row] * 6,
        out_shape=[jax.ShapeDtypeStruct((n, d), BF16)] * 6,
        compiler_params=_params("parallel"),
        name="rwkv_token_shift",
    )(x2, last, mu)


def _rw_lowrank_body(xw_ref, xa_ref, xg_ref, w1_ref, w2_ref, w0_ref, a1_ref, a2_ref, a0_ref, g1_ref, g2_ref,
                     lw_ref, a_ref, g_ref):
    z = w0_ref[...] + _bdot(jnp.tanh(jnp.dot(xw_ref[...], w1_ref[...], preferred_element_type=F32)), w2_ref[...])
    w_log = jnp.minimum(z, 0.0) - jnp.log(1.0 + jnp.exp(-jnp.abs(z))) - 0.5
    lw_ref[...] = -jnp.exp(w_log)
    za = a0_ref[...] + _bdot(jnp.dot(xa_ref[...], a1_ref[...], preferred_element_type=F32), a2_ref[...])
    a_ref[...] = jax.nn.sigmoid(za)
    hg = jax.nn.sigmoid(jnp.dot(xg_ref[...], g1_ref[...], preferred_element_type=F32))
    g_ref[...] = _bdot(hg, g2_ref[...])


def _pad_rank(w_in, w_out):
    r = w_in.shape[1]
    rp = -(-r // LANES) * LANES
    return (jnp.pad(w_in, ((0, 0), (0, rp - r))).astype(BF16), jnp.pad(w_out, ((0, rp - r), (0, 0))).astype(BF16))


def _rw_lowrank(xw, xa, xg, w0, w1, w2, a0, a1, a2, g1, g2, tm=256):
    n, d = xw.shape
    w1p, w2p = _pad_rank(w1, w2)
    a1p, a2p = _pad_rank(a1, a2)
    g1p, g2p = _pad_rank(g1, g2)
    row = pl.BlockSpec((tm, d), lambda i: (i, 0))
    full = lambda arr: pl.BlockSpec(arr.shape, lambda i: (0, 0))
    w0r, a0r = w0.reshape(1, d), a0.reshape(1, d)
    return pl.pallas_call(
        _rw_lowrank_body,
        grid=(n // tm,),
        in_specs=[row, row, row, full(w1p), full(w2p), full(w0r), full(a1p), full(a2p), full(a0r), full(g1p), full(g2p)],
        out_specs=[row] * 3,
        out_shape=[jax.ShapeDtypeStruct((n, d), F32)] * 3,
        compiler_params=_params("parallel"),
        name="rwkv_lowrank",
    )(xw, xa, xg, w1p, w2p, w0r, a1p, a2p, a0r, g1p, g2p)


def _wkv_body(r_ref, k_ref, v_ref, lw_ref, a_ref, g_ref, kk_ref, ka_ref, rk_ref, lng_ref, lnb_ref,
              o_ref, s_ref, rp_ref, yq_ref, bonus_ref, gs_ref, *, nchunk):
    L = RW_CHUNK
    H2 = 2 * L

    step = pl.program_id(2)
    slot_w = step % 2
    slot_r = 1 - slot_w

    @pl.when(step == 0)
    def _():
        s_ref[...] = jnp.zeros_like(s_ref)
        rp_ref[1] = jnp.zeros(rp_ref.shape[1:], F32)
        yq_ref[1] = jnp.zeros(yq_ref.shape[1:], F32)
        bonus_ref[1] = jnp.zeros(bonus_ref.shape[1:], F32)
        gs_ref[1] = jnp.zeros(gs_ref.shape[1:], F32)

    lane = lax.broadcasted_iota(jnp.int32, (1, LANES), 1)
    mask0 = (lane < RW_HEAD_DIM).astype(F32)
    mask1 = 1.0 - mask0
    ri = lax.broadcasted_iota(jnp.int32, (H2, H2), 0)
    ci = lax.broadcasted_iota(jnp.int32, (H2, H2), 1)
    same_head = (ri // L) == (ci // L)
    strict = (same_head & (ci < ri)).astype(F32)
    incl = (same_head & (ci <= ri)).astype(F32)
    diag16 = ((ri // 16) == (ci // 16)).astype(F32)
    eye = (ri == ci).astype(F32)
    head_ones = same_head.astype(BF16)
    tl = lax.broadcasted_iota(jnp.int32, (L, L), 0)
    sl = lax.broadcasted_iota(jnp.int32, (L, L), 1)
    tri_incl = (sl <= tl).astype(BF16)

    def stack(x):
        return jnp.concatenate([x * mask0, x * mask1], axis=0)

    k_k = kk_ref[...]
    k_a = ka_ref[...]
    r_k = rk_ref[...]

    chunks = range(nchunk)

    rp_prev = [rp_ref[slot_r, c] for c in chunks]
    yq_prev = [yq_ref[slot_r, c] for c in chunks]
    chain = {"s": s_ref[...], "ys": []}

    def chain_step():
        c = len(chain["ys"])
        if c >= nchunk:
            return
        res = _bdot(rp_prev[c], chain["s"]) + yq_prev[c]
        chain["ys"].append(res[:L] + res[L:H2])
        chain["s"] = res[H2:]
        if c == nchunk - 1:
            s_ref[...] = chain["s"]
            y = jnp.concatenate(chain["ys"], axis=0)
            inv_n = 1.0 / RW_HEAD_DIM
            ym = _dot_split_const(y, head_ones) * inv_n
            yc = y - ym
            yv = _dot_split_const(yc * yc, head_ones) * inv_n
            yn = yc * lax.rsqrt(yv + RW_GN_EPS) * lng_ref[...] + lnb_ref[...]
            o_ref[...] = ((yn + bonus_ref[slot_r]) * gs_ref[slot_r]).astype(BF16)

    def each(fn, *lists):
        return [fn(*xs) for xs in zip(*lists)]

    def mm_stage(fn, *lists):
        out = each(fn, *lists)
        chain_step()
        return out

    def rows_of(x):
        return [x[c * L:(c + 1) * L] for c in chunks]

    r_all = r_ref[...]
    k_all = k_ref[...]
    v_all = v_ref[...]
    ag_all = a_ref[...]
    kk_all = k_all * k_k
    ss_all = _dot_split_const(kk_all * kk_all, head_ones)
    kk_all = kk_all / jnp.maximum(jnp.sqrt(ss_all), 1e-12)
    kmod_all = k_all * (1.0 + (ag_all - 1.0) * k_a)
    bv_all = kk_all * ag_all
    lw_c = rows_of(lw_ref[...])
    cl_c = each(lambda lw: _dot_const_split(tri_incl, lw), lw_c)
    last_c = each(lambda cl: cl[L - 1:L, :], cl_c)
    cl_all = jnp.concatenate(cl_c, axis=0)
    clp_all = cl_all - lw_ref[...]
    end_all = jnp.concatenate(each(lambda cl, la: la - cl, cl_c, last_c), axis=0)
    e_neg = jnp.exp(-cl_all)
    e_end = jnp.exp(end_all)
    at_c = rows_of(-kk_all * jnp.exp(clp_all))
    rt_c = rows_of(r_all * jnp.exp(cl_all))
    bt_c = rows_of(bv_all * e_neg)
    kt_c = rows_of(kmod_all * e_neg)
    be_c = rows_of(bv_all * e_end)
    ke_c = rows_of(kmod_all * e_end)
    v_s = each(stack, rows_of(v_all))
    at_s = each(stack, at_c)
    rt_s = each(stack, rt_c)
    gmat = mm_stage(lambda a_, r_, b_, k_: _bdot_nt(jnp.concatenate([a_, r_], axis=0),
                                                    jnp.concatenate([stack(b_), stack(k_)], axis=0)),
                    at_s, rt_s, bt_c, kt_c)
    a_ab = each(lambda gm: gm[:H2, :H2] * strict, gmat)
    a_ak = each(lambda gm: gm[:H2, H2:] * strict, gmat)
    a_rb = each(lambda gm: gm[H2:, :H2] * incl, gmat)
    a_rk = each(lambda gm: gm[H2:, H2:] * incl, gmat)
    dblk = each(lambda a_: a_ * diag16, a_ab)
    off = each(lambda a_, d_: a_ - d_, a_ab, dblk)
    d2 = mm_stage(_bdot, dblk, dblk)
    d4 = mm_stage(_bdot, d2, d2)
    d8 = mm_stage(_bdot, d4, d4)
    dinv = mm_stage(lambda d_, d2_: _bdot(eye + d_, eye + d2_), dblk, d2)
    dinv = mm_stage(lambda di, d4_: _bdot(di, eye + d4_), dinv, d4)
    dinv = mm_stage(lambda di, d8_: _bdot(di, eye + d8_), dinv, d8)
    e1 = mm_stage(_bdot, dinv, off)
    e2 = mm_stage(_bdot, e1, e1)
    minv = mm_stage(lambda e1_, e2_: _bdot(eye + e1_, eye + e2_), e1, e2)
    minv = mm_stage(_bdot, minv, dinv)
    x_ak = mm_stage(_bdot, a_ak, v_s)
    zu = each(lambda mi, a_, x_: _bdot(mi, jnp.concatenate([a_, x_], axis=1)), minv, at_s, x_ak)
    w2 = each(_bdot, a_rb, zu)
    rkv = each(_bdot, a_rk, v_s)
    pq = each(lambda b_, z_: _bdot(stack(b_).T, z_), be_c, zu)
    kv2 = each(lambda k_, v_: _bdot(stack(k_).T, v_), ke_c, v_s)
    while len(chain["ys"]) < nchunk:
        chain_step()
    for c in chunks:
        rp_ref[slot_w, c] = jnp.concatenate([rt_s[c] + w2[c][:, :LANES],
                                             eye * jnp.exp(last_c[c]) + pq[c][:, :LANES]], axis=0)
        yq_ref[slot_w, c] = jnp.concatenate([w2[c][:, LANES:] + rkv[c], pq[c][:, LANES:] + kv2[c]], axis=0)
    bonus_ref[slot_w] = _dot_split_const(r_all * kmod_all * r_k, head_ones) * v_all
    gs_ref[slot_w] = g_ref[...]


def _wkv(r, k, v, lw, a, g, k_k, k_a, r_k, lnx_g, lnx_b, batch):
    n, d = r.shape
    t = n // batch
    nchunk = RW_CHUNKS_PER_STEP
    tb = RW_CHUNK * nchunk
    while t % tb:
        nchunk //= 2
        tb = RW_CHUNK * nchunk
    nt = t // tb
    row_in = pl.BlockSpec((tb, LANES), lambda b, hp, c: (b * nt + jnp.minimum(c, nt - 1), hp))
    row_out = pl.BlockSpec((tb, LANES), lambda b, hp, c: (b * nt + jnp.maximum(c - 1, 0), hp))
    vec = pl.BlockSpec((1, LANES), lambda b, hp, c: (0, hp))
    sq = pltpu.VMEM((2, nchunk, 2 * LANES, LANES), F32)
    blk = pltpu.VMEM((2, tb, LANES), F32)
    vecs = [z.reshape(1, d) for z in (k_k, k_a, r_k, lnx_g, lnx_b)]
    return pl.pallas_call(
        functools.partial(_wkv_body, nchunk=nchunk),
        grid=(batch, d // LANES, nt + 1),
        in_specs=[row_in] * 6 + [vec] * 5,
        out_specs=row_out,
        out_shape=jax.ShapeDtypeStruct((n, d), BF16),
        scratch_shapes=[pltpu.VMEM((LANES, LANES), F32), sq, sq, blk, blk],
        compiler_params=_params("parallel", "parallel", "arbitrary"),
        name="rwkv_chunked_scan",
    )(r, k, v, lw, a, g, *vecs)


def _rwkv_time_mix(x2, batch, mu, w_rkv, w0, w1, w2, a0, a1, a2, g1, g2, k_k, k_a, r_k, lnx_g, lnx_b, w_o):
    xr, xw, xk, xv, xa, xg = _rw_mix(x2, batch, mu)
    r = _matmul(xr, w_rkv[0].astype(BF16))
    k = _matmul(xk, w_rkv[1].astype(BF16))
    v = _matmul(xv, w_rkv[2].astype(BF16))
    lw, a, g = _rw_lowrank(xw, xa, xg, w0, w1, w2, a0, a1, a2, g1, g2)
    z = _wkv(r, k, v, lw, a, g, k_k, k_a, r_k.reshape(-1), lnx_g, lnx_b, batch)
    return _matmul(z, w_o.astype(BF16))


def _router_body(x_ref, w_ref, b_ref, idx_ref, gate_ref):
    logits = jnp.dot(x_ref[...], w_ref[...], preferred_element_type=F32) + b_ref[...]
    lane = lax.broadcasted_iota(jnp.int32, logits.shape, 1).astype(F32)
    cur = logits
    vals, idxs = [], []
    for _ in range(TOP_K):
        m = jnp.max(cur, axis=-1, keepdims=True)
        i = jnp.min(jnp.where(cur == m, lane, float(LANES)), axis=-1, keepdims=True)
        vals.append(m)
        idxs.append(i)
        cur = jnp.where(lane == i, -3e38, cur)
    es = [jnp.exp(vv - vals[0]) for vv in vals]
    den = es[0]
    for e in es[1:]:
        den = den + e
    idx_out = jnp.zeros(logits.shape, F32)
    gate_out = jnp.zeros(logits.shape, F32)
    for kk in range(TOP_K):
        idx_out = jnp.where(lane == kk, idxs[kk], idx_out)
        gate_out = jnp.where(lane == kk, es[kk] / den, gate_out)
    idx_ref[...] = idx_out.astype(jnp.int32)
    gate_ref[...] = gate_out


def _router(xb, router_w, router_b, tm=512):
    n, d = xb.shape
    e = router_w.shape[1]
    wp = jnp.pad(router_w, ((0, 0), (0, LANES - e))).astype(BF16)
    bp = jnp.concatenate([router_b.astype(F32), jnp.full((LANES - e,), NEG_BIG, F32)]).reshape(1, LANES)
    row = pl.BlockSpec((tm, LANES), lambda i: (i, 0))
    idx, gate = pl.pallas_call(
        _router_body,
        grid=(n // tm,),
        in_specs=[pl.BlockSpec((tm, d), lambda i: (i, 0)), pl.BlockSpec((d, LANES), lambda i: (0, 0)),
                  pl.BlockSpec((1, LANES), lambda i: (0, 0))],
        out_specs=[row, row],
        out_shape=[jax.ShapeDtypeStruct((n, LANES), jnp.int32), jax.ShapeDtypeStruct((n, LANES), F32)],
        compiler_params=_params("parallel"),
        name="moe_router",
    )(xb, wp, bp)
    return idx[:, :TOP_K], gate


def _moe_up_body(it_ref, ct_ref, be_ref, nu_ref, x_ref, wg_ref, wu_ref, bg_ref, bu_ref, o_ref):
    used = it_ref[pl.program_id(0)] < nu_ref[0]

    @pl.when(used)
    def _():
        x = x_ref[...]
        gate = jnp.dot(x, wg_ref[...].astype(BF16), preferred_element_type=F32) + bg_ref[...]
        up = jnp.dot(x, wu_ref[...].astype(BF16), preferred_element_type=F32) + bu_ref[...]
        gate = jnp.minimum(gate, SWIGLU_LIMIT)
        up = jnp.clip(up, -SWIGLU_LIMIT, SWIGLU_LIMIT)
        o_ref[...] = ((up + 1.0) * (gate * jax.nn.sigmoid(gate * SWIGLU_ALPHA))).astype(o_ref.dtype)

    @pl.when(jnp.logical_not(used))
    def _():
        o_ref[...] = jnp.zeros_like(o_ref)


def _moe_down_body(it_ref, ct_ref, be_ref, nu_ref, a_ref, wd_ref, bd_ref, o_ref):
    used = it_ref[pl.program_id(0)] < nu_ref[0]

    @pl.when(used)
    def _():
        y = jnp.dot(a_ref[...], wd_ref[...].astype(BF16), preferred_element_type=F32) + bd_ref[...]
        o_ref[...] = y.astype(o_ref.dtype)

    @pl.when(jnp.logical_not(used))
    def _():
        o_ref[...] = jnp.zeros_like(o_ref)


def _expert_major_steps(blk_e, n_blk, n_col):
    col = jnp.repeat(jnp.arange(n_col, dtype=jnp.int32), n_blk)
    blk = jnp.tile(jnp.arange(n_blk, dtype=jnp.int32), n_col)
    experts = jnp.arange(N_EXPERTS, dtype=jnp.int32)
    first = jnp.sum((blk_e[None, :] < experts[:, None]).astype(jnp.int32), axis=1)
    count = jnp.sum((blk_e[None, :] == experts[:, None]).astype(jnp.int32), axis=1)
    e = blk_e[blk]
    pos = n_col * first[e] + col * count[e] + (blk - first[e])
    zeros = jnp.zeros((n_col * n_blk,), jnp.int32)
    return zeros.at[pos].set(blk), zeros.at[pos].set(col)


def _moe_experts(xs, blk_e, n_used, layer, w_gu, b_gu, w_down, b_down):
    n_rows, d = xs.shape
    depth, ne, _, f2 = w_gu.shape
    fdim = f2 // 2
    tm, tf, tn = MOE_ROW_BLOCK, MOE_F_TILE, MOE_OUT_TILE
    nf, nn = fdim // tf, d // tn
    n_blk = n_rows // tm
    bgu = b_gu.reshape(depth, ne, 1, f2)

    it, ct = _expert_major_steps(blk_e, n_blk, nf)
    act = pl.pallas_call(
        _moe_up_body,
        grid_spec=pltpu.PrefetchScalarGridSpec(
            num_scalar_prefetch=4,
            grid=(nf * n_blk,),
            in_specs=[
                pl.BlockSpec((tm, d), lambda t, it, ct, be, nu: (jnp.minimum(it[t], nu[0] - 1), 0)),
                pl.BlockSpec((None, None, d, tf), lambda t, it, ct, be, nu: (layer, be[it[t]], 0, ct[t])),
                pl.BlockSpec((None, None, d, tf), lambda t, it, ct, be, nu: (layer, be[it[t]], 0, nf + ct[t])),
                pl.BlockSpec((None, None, 1, tf), lambda t, it, ct, be, nu: (layer, be[it[t]], 0, ct[t])),
                pl.BlockSpec((None, None, 1, tf), lambda t, it, ct, be, nu: (layer, be[it[t]], 0, nf + ct[t])),
            ],
            out_specs=pl.BlockSpec((tm, tf), lambda t, it, ct, be, nu: (it[t], ct[t])),
        ),
        out_shape=jax.ShapeDtypeStruct((n_rows, fdim), BF16),
        compiler_params=_params("arbitrary"),
        name="moe_experts_up",
    )(it, ct, blk_e, n_used, xs, w_gu, w_gu, bgu, bgu)

    it2, ct2 = _expert_major_steps(blk_e, n_blk, nn)
    return pl.pallas_call(
        _moe_down_body,
        grid_spec=pltpu.PrefetchScalarGridSpec(
            num_scalar_prefetch=4,
            grid=(nn * n_blk,),
            in_specs=[
                pl.BlockSpec((tm, fdim), lambda t, it, ct, be, nu: (jnp.minimum(it[t], nu[0] - 1), 0)),
                pl.BlockSpec((None, None, fdim, tn), lambda t, it, ct, be, nu: (layer, be[it[t]], 0, ct[t])),
                pl.BlockSpec((None, None, 1, tn), lambda t, it, ct, be, nu: (layer, be[it[t]], 0, ct[t])),
            ],
            out_specs=pl.BlockSpec((tm, tn), lambda t, it, ct, be, nu: (it[t], ct[t])),
        ),
        out_shape=jax.ShapeDtypeStruct((n_rows, d), BF16),
        compiler_params=_params("arbitrary"),
        name="moe_experts_down",
    )(it2, ct2, blk_e, n_used, act, w_down, b_down.reshape(depth, ne, 1, d))


def _combine_ln_body(x_ref, y_ref, gate_ref, g_ref, b_ref, o_ref, ob_ref, *, alpha):
    ffn = y_ref[0].astype(F32) * gate_ref[:, 0:1]
    for kk in range(1, TOP_K):
        ffn = ffn + y_ref[kk].astype(F32) * gate_ref[:, kk:kk + 1]
    z = alpha * x_ref[...] + ffn
    mu = jnp.mean(z, -1, keepdims=True)
    zc = z - mu
    var = jnp.mean(zc * zc, -1, keepdims=True)
    y = zc * lax.rsqrt(var + LN_EPS) * g_ref[...] + b_ref[...]
    o_ref[...] = y
    ob_ref[...] = y.astype(BF16)


def _combine_ln(x, y4, gates, g, b, alpha, tm=256):
    n, d = x.shape
    row = pl.BlockSpec((tm, d), lambda i: (i, 0))
    vec = pl.BlockSpec((1, d), lambda i: (0, 0))
    return pl.pallas_call(
        functools.partial(_combine_ln_body, alpha=alpha),
        grid=(n // tm,),
        in_specs=[row, pl.BlockSpec((TOP_K, tm, d), lambda i: (0, i, 0)),
                  pl.BlockSpec((tm, LANES), lambda i: (i, 0)), vec, vec],
        out_specs=[row, row],
        out_shape=[jax.ShapeDtypeStruct((n, d), F32), jax.ShapeDtypeStruct((n, d), BF16)],
        compiler_params=_params("parallel"),
        name="moe_combine_layer_norm",
    )(x, y4, gates, g.reshape(1, d), b.reshape(1, d))


def _moe_ffn(xb, layer, router_w, router_b, w_gu, b_gu, w_down, b_down):
    n, d = xb.shape
    tm = MOE_ROW_BLOCK
    top_i, gates = _router(xb, router_w, router_b)
    flat_e = top_i.reshape(-1)
    onehot = (flat_e[:, None] == jnp.arange(N_EXPERTS, dtype=jnp.int32)[None, :]).astype(jnp.int32)
    csum = jnp.cumsum(onehot, axis=0)
    rank = jnp.take_along_axis(csum, flat_e[:, None], axis=1)[:, 0] - 1
    counts = csum[-1]
    padded = ((counts + tm - 1) // tm) * tm
    pad_end = jnp.cumsum(padded)
    pad_start = pad_end - padded
    dest = pad_start[flat_e] + rank
    n_rows = -(-(n * TOP_K) // tm) * tm + N_EXPERTS * tm
    n_blk = n_rows // tm
    blk_start = jnp.arange(n_blk, dtype=jnp.int32) * tm
    blk_e = jnp.minimum(jnp.sum((pad_end[None, :] <= blk_start[:, None]).astype(jnp.int32), axis=1),
                        N_EXPERTS - 1).astype(jnp.int32)
    n_used = (pad_end[-1] // tm).astype(jnp.int32).reshape(1)
    flat_tok = jnp.arange(n * TOP_K, dtype=jnp.int32) // TOP_K
    row_tok = (jnp.arange(n_rows, dtype=jnp.int32) % n).at[dest].set(flat_tok)
    xs = jnp.take(xb, row_tok, axis=0, mode='clip')
    ys = _moe_experts(xs, blk_e, n_used, layer, w_gu, b_gu, w_down, b_down)
    dest_kmajor = dest.reshape(n, TOP_K).T.reshape(-1)
    y4 = jnp.take(ys, dest_kmajor, axis=0, mode='clip').reshape(TOP_K, n, d)
    return y4, gates


def _cmp_mlp_body(x_ref, pe_ref, w1_ref, b1_ref, w2_ref, b2_ref, o_ref):
    h = _bdot(x_ref[...] + pe_ref[...], w1_ref[...]) + b1_ref[...]
    h = jax.nn.gelu(h)
    o_ref[...] = (_bdot(h, w2_ref[...]) + b2_ref[...]).astype(o_ref.dtype)


def _cmp_mlp(flat, pe, w1, b1, w2, b2, tm=256):
    m, kd = flat.shape
    hid = w1.shape[1]
    dk = w2.shape[1]
    full = lambda shp: pl.BlockSpec(shp, lambda i: (0, 0))
    return pl.pallas_call(
        _cmp_mlp_body,
        grid=(m // tm,),
        in_specs=[pl.BlockSpec((tm, kd), lambda i: (i, 0)), full((1, kd)), full((kd, hid)), full((1, hid)),
                  full((hid, dk)), full((1, dk))],
        out_specs=pl.BlockSpec((tm, dk), lambda i: (i, 0)),
        out_shape=jax.ShapeDtypeStruct((m, dk), BF16),
        compiler_params=_params("parallel"),
        name="nsa_compress_mlp",
    )(flat, pe.reshape(1, kd), w1.astype(BF16), b1.reshape(1, hid), w2.astype(BF16), b2.reshape(1, dk))


def _group_rows(q_ref, hpg):
    dk = NSA_HEAD_DIM
    return jnp.concatenate([q_ref[:, h * dk:(h + 1) * dk] for h in range(hpg)], axis=0)


def _softmax_rows(s, mask):
    s = jnp.where(mask, s, NEG_BIG)
    m = jnp.max(s, axis=-1, keepdims=True)
    e = jnp.where(mask, jnp.exp2(s - m), 0.0)
    den = jnp.sum(e, axis=-1, keepdims=True)
    return e / jnp.where(den > 0, den, 1.0)


def _exp2_rows(s_rows, bias, cols):
    s_h = [s_rows[:, cj] + bias[:, cj] for cj in cols]
    mx = s_h[0]
    for s_hj in s_h[1:]:
        mx = jnp.maximum(mx, s_hj)
    m = jnp.broadcast_to(jnp.max(mx, axis=-1, keepdims=True), mx.shape)
    return jnp.concatenate([jnp.exp2(s_hj - m).astype(BF16) for s_hj in s_h], axis=1)


def _nsa_cmp_body(q_ref, kc_ref, vc_ref, ov_ref, *rest, hpg, n_c, n_sel, qb0):
    o_ref, sel_ref = rest[-2:]
    qb = pl.program_id(2) + qb0
    t0 = qb * Q_BLOCK
    ncols = kc_ref.shape[0]
    n_s = ov_ref.shape[1]
    dk = NSA_HEAD_DIM
    tq_c = t0 + lax.broadcasted_iota(jnp.int32, (Q_BLOCK, ncols), 0)
    cid = lax.broadcasted_iota(jnp.int32, (Q_BLOCK, ncols), 1)
    bias = jnp.where((cid * CMP_STRIDE + (CMP_LEN - 1) <= tq_c) & (cid < n_c), 0.0, NEG_BIG)
    t_row = t0 + lax.broadcasted_iota(jnp.int32, (Q_BLOCK, LANES), 0)
    row_live = t_row >= CMP_LEN - 1
    cols = [slice(j * LANES, (j + 1) * LANES) for j in range(ncols // LANES)]
    k_tile = kc_ref[...]
    v_aug = jnp.concatenate([vc_ref[...], jnp.ones((ncols, LANES), BF16)], axis=1)
    ov = ov_ref[...]
    hg = min(CMP_HEADS_PER_DOT, hpg)
    imp = jnp.zeros((Q_BLOCK, n_s), F32)
    for g0 in range(0, hpg, hg):
        qg = jnp.concatenate([q_ref[:, h * dk:(h + 1) * dk] for h in range(g0, g0 + hg)], axis=0)
        s_g = _bdot_nt(qg, k_tile)
        e = jnp.concatenate([_exp2_rows(s_g[hl * Q_BLOCK:(hl + 1) * Q_BLOCK], bias, cols) for hl in range(hg)], axis=0)
        od = jnp.dot(e, v_aug, preferred_element_type=F32)
        ih = jnp.dot(e, ov, preferred_element_type=F32)
        for hl in range(hg):
            rows = slice(hl * Q_BLOCK, (hl + 1) * Q_BLOCK)
            inv = jnp.where(row_live, 1.0 / od[rows, dk:], 0.0)
            o_ref[:, (g0 + hl) * dk:(g0 + hl + 1) * dk] = (od[rows, :dk] * inv).astype(o_ref.dtype)
            inv_s = inv[:, :n_s] if n_s <= LANES else jnp.concatenate([inv] * (n_s // LANES), axis=1)
            imp = imp + ih[rows] * inv_s
    imp_t = imp.T
    tq = t0 + lax.broadcasted_iota(jnp.int32, (n_s, Q_BLOCK), 1)
    sid_i = lax.broadcasted_iota(jnp.int32, (n_s, Q_BLOCK), 0)
    cur = tq // SEL_BLOCK
    forced = (sid_i == 0) | (sid_i == cur) | (sid_i == cur - 1)
    score = jnp.where(sid_i * SEL_BLOCK <= tq, jnp.where(forced, 1e30, imp_t), -1.0)
    sid = sid_i.astype(F32)
    sel = jnp.zeros((n_s, Q_BLOCK), F32)
    for _ in range(n_sel):
        m = jnp.max(score, axis=0, keepdims=True)
        first = jnp.min(jnp.where(score == m, sid, float(n_s)), axis=0, keepdims=True)
        hit = sid == first
        sel = jnp.where(hit, 1.0, sel)
        score = jnp.where(hit, -2.0, score)
    sel_ref[...] = sel.T.astype(BF16)


def _nsa_win_body(q_ref, *refs, hpg, nwb):
    k_refs, v_refs, o_ref = refs[:nwb], refs[nwb:2 * nwb], refs[2 * nwb]
    qb = pl.program_id(2)
    t0 = qb * Q_BLOCK
    dk = NSA_HEAD_DIM
    nk = nwb * Q_BLOCK
    kcat = jnp.concatenate([r[...] for r in k_refs], axis=0)
    v_aug = jnp.concatenate([r[...] for r in v_refs], axis=0)
    v_aug = jnp.concatenate([v_aug, jnp.ones((nk, LANES), BF16)], axis=1)
    t = t0 + lax.broadcasted_iota(jnp.int32, (Q_BLOCK, nk), 0)
    kpos = t0 - WINDOW + lax.broadcasted_iota(jnp.int32, (Q_BLOCK, nk), 1)
    bias = jnp.where((kpos <= t) & (kpos > t - WINDOW) & (kpos >= 0), 0.0, NEG_BIG)
    cols = [slice(j * LANES, (j + 1) * LANES) for j in range(nk // LANES)]
    hg = min(CMP_HEADS_PER_DOT, hpg)
    for g0 in range(0, hpg, hg):
        qg = jnp.concatenate([q_ref[:, h * dk:(h + 1) * dk] for h in range(g0, g0 + hg)], axis=0)
        s_g = _bdot_nt(qg, kcat)
        e = jnp.concatenate([_exp2_rows(s_g[hl * Q_BLOCK:(hl + 1) * Q_BLOCK], bias, cols) for hl in range(hg)], axis=0)
        od = jnp.dot(e, v_aug, preferred_element_type=F32)
        for hl in range(hg):
            rows = slice(hl * Q_BLOCK, (hl + 1) * Q_BLOCK)
            o_ref[:, (g0 + hl) * dk:(g0 + hl + 1) * dk] = (od[rows, :dk] / od[rows, dk:]).astype(o_ref.dtype)


def _nsa_sel_body(qb_ref, kb_ref, q_ref, k_ref, v_ref, sel_ref, ex_ref, o_ref, m_ref, acc_ref, *, hpg):
    step = pl.program_id(2)
    qb = qb_ref[step]
    kb = kb_ref[step]
    qn = q_ref.shape[0]
    t0 = qb * qn
    tk = k_ref.shape[0]
    dk = NSA_HEAD_DIM

    @pl.when(kb == 0)
    def _():
        m_ref[...] = jnp.full_like(m_ref, NEG_BIG)
        acc_ref[...] = jnp.zeros_like(acc_ref)

    picked = jnp.dot(sel_ref[...], ex_ref[...], preferred_element_type=F32)
    tq = t0 + lax.broadcasted_iota(jnp.int32, (qn, tk), 0)
    kpos = kb * tk + lax.broadcasted_iota(jnp.int32, (qn, tk), 1)
    bias = jnp.where((picked > 0.5) & (kpos <= tq), 0.0, NEG_BIG)
    cols = [slice(j * LANES, (j + 1) * LANES) for j in range(tk // LANES)]
    k_tile = k_ref[...]
    v_aug = jnp.concatenate([v_ref[...], jnp.ones((tk, LANES), BF16)], axis=1)
    hg = SEL_HEADS_PER_DOT

    def scores(g0):
        qg = jnp.concatenate([q_ref[:, h * dk:(h + 1) * dk] for h in range(g0, g0 + hg)], axis=0)
        return _bdot_nt(qg, k_tile)

    s_next = scores(0)
    for g0 in range(0, hpg, hg):
        s_g = s_next
        if g0 + hg < hpg:
            s_next = scores(g0 + hg)
        p_rows, alphas = [], []
        for hl in range(hg):
            rows = slice((g0 + hl) * qn, (g0 + hl + 1) * qn)
            s_h = [s_g[hl * qn:(hl + 1) * qn, cj] + bias[:, cj] for cj in cols]
            mx = s_h[0]
            for s_hj in s_h[1:]:
                mx = jnp.maximum(mx, s_hj)
            m_old = m_ref[rows, :]
            m_new = jnp.maximum(m_old, jnp.broadcast_to(jnp.max(mx, axis=-1, keepdims=True), m_old.shape))
            m_ref[rows, :] = m_new
            alphas.append(jnp.exp2(m_old - m_new))
            p_rows.append(jnp.concatenate([jnp.exp2(s_hj - m_new).astype(BF16) for s_hj in s_h], axis=1))
        pv = jnp.dot(jnp.concatenate(p_rows, axis=0), v_aug, preferred_element_type=F32)
        alpha = jnp.concatenate(alphas, axis=0)
        grows = slice(g0 * qn, (g0 + hg) * qn)
        acc_ref[grows, :dk] = alpha * acc_ref[grows, :dk] + pv[:, :dk]
        acc_ref[grows, dk:] = alpha * acc_ref[grows, dk:] + pv[:, dk:]

    @pl.when(kb == (t0 + qn - 1) // tk)
    def _():
        den = acc_ref[:, dk:]
        o = acc_ref[:, :dk] / jnp.where(den > 0, den, 1.0)
        for h in range(hpg):
            o_ref[:, h * dk:(h + 1) * dk] = o[h * qn:(h + 1) * qn].astype(o_ref.dtype)


def _nsa_attention(q, kvb, k_cmp, v_cmp, batch, n_c):
    n, hd = q.shape
    dk, g = NSA_HEAD_DIM, NSA_KV_GROUPS
    hpg = hd // dk // g
    t = n // batch
    nqb = t // Q_BLOCK
    n_s = t // SEL_BLOCK
    n_sel = min(N_SEL, n_s)
    ncp = k_cmp.shape[2]
    gw = hpg * dk

    c_lo = np.arange(ncp) * CMP_STRIDE
    s_lo = np.arange(n_s) * SEL_BLOCK
    overlap = ((c_lo[:, None] < s_lo[None, :] + SEL_BLOCK) & (c_lo[:, None] + CMP_LEN > s_lo[None, :])
               & (np.arange(ncp)[:, None] < n_c))
    overlap = jnp.asarray(overlap, BF16)

    qspec = pl.BlockSpec((Q_BLOCK, gw), lambda b, gi, qb: (b * nqb + qb, gi))
    seg_qb = CMP_SEG_COLS * CMP_STRIDE // Q_BLOCK
    o_c = sel = None
    for qb0 in range(0, nqb, seg_qb):
        nq = min(seg_qb, nqb - qb0)
        ncols = min(ncp, -(-((qb0 + nq) * Q_BLOCK // CMP_STRIDE) // LANES) * LANES)
        oq = pl.BlockSpec((Q_BLOCK, gw), lambda b, gi, qb, qb0=qb0: (b * nqb + qb0 + qb, gi))
        in_specs = [oq,
                    pl.BlockSpec((None, None, ncols, dk), lambda b, gi, qb: (b, gi, 0, 0)),
                    pl.BlockSpec((None, None, ncols, dk), lambda b, gi, qb: (b, gi, 0, 0)),
                    pl.BlockSpec((ncols, n_s), lambda b, gi, qb: (0, 0))]
        args = [q, k_cmp, v_cmp, overlap]
        aliases = {}
        if o_c is not None:
            in_specs += [pl.BlockSpec(memory_space=pl.ANY), pl.BlockSpec(memory_space=pl.ANY)]
            args += [o_c, sel]
            aliases = {4: 0, 5: 1}
        o_c, sel = pl.pallas_call(
            functools.partial(_nsa_cmp_body, hpg=hpg, n_c=n_c, n_sel=n_sel, qb0=qb0),
            grid=(batch, g, nq),
            in_specs=in_specs,
            out_specs=[oq, pl.BlockSpec((None, None, Q_BLOCK, n_s), lambda b, gi, qb, qb0=qb0: (b, gi, qb0 + qb, 0))],
            out_shape=[jax.ShapeDtypeStruct((n, hd), BF16), jax.ShapeDtypeStruct((batch, g, t, n_s), BF16)],
            input_output_aliases=aliases,
            compiler_params=_params("parallel", "parallel", "parallel"),
            name="nsa_compressed_select",
        )(*args)

    nwb = WINDOW // Q_BLOCK + 1
    kcol, vcol = 4 * g, 5 * g

    def kv_spec(col, j):
        return pl.BlockSpec((Q_BLOCK, dk),
                            lambda b, gi, qb: (b * nqb + jnp.maximum(qb - (nwb - 1) + j, 0), col + gi))

    o_w = pl.pallas_call(
        functools.partial(_nsa_win_body, hpg=hpg, nwb=nwb),
        grid=(batch, g, nqb),
        in_specs=[qspec] + [kv_spec(kcol, j) for j in range(nwb)] + [kv_spec(vcol, j) for j in range(nwb)],
        out_specs=qspec,
        out_shape=jax.ShapeDtypeStruct((n, hd), BF16),
        compiler_params=_params("parallel", "parallel", "parallel"),
        name="nsa_window",
    )(q, *([kvb] * (2 * nwb)))

    tk = min(SEL_KEY_TILE, t)
    nkb = t // tk
    qn = min(SEL_Q_TILE, t)
    nqt = t // qn
    steps = [(qb, kb) for qb in range(nqt) for kb in range((qb * qn + qn - 1) // tk + 1)]
    qb_tab = jnp.asarray([s_[0] for s_ in steps], jnp.int32)
    kb_tab = jnp.asarray([s_[1] for s_ in steps], jnp.int32)
    expand = jnp.asarray(np.arange(n_s)[:, None] == (np.arange(t)[None, :] // SEL_BLOCK), BF16)
    kscol, vscol = 2 * g, 3 * g
    grid_spec = pltpu.PrefetchScalarGridSpec(
        num_scalar_prefetch=2,
        grid=(batch, g, len(steps)),
        in_specs=[
            pl.BlockSpec((qn, gw), lambda b, gi, s_, qt, kt: (b * nqt + qt[s_], gi)),
            pl.BlockSpec((tk, dk), lambda b, gi, s_, qt, kt: (b * nkb + kt[s_], kscol + gi)),
            pl.BlockSpec((tk, dk), lambda b, gi, s_, qt, kt: (b * nkb + kt[s_], vscol + gi)),
            pl.BlockSpec((None, None, qn, n_s), lambda b, gi, s_, qt, kt: (b, gi, qt[s_], 0)),
            pl.BlockSpec((n_s, tk), lambda b, gi, s_, qt, kt: (0, kt[s_])),
        ],
        out_specs=pl.BlockSpec((qn, gw), lambda b, gi, s_, qt, kt: (b * nqt + qt[s_], gi)),
        scratch_shapes=[pltpu.VMEM((hpg * qn, LANES), F32), pltpu.VMEM((hpg * qn, dk + LANES), F32)],
    )
    o_s = pl.pallas_call(
        functools.partial(_nsa_sel_body, hpg=hpg),
        grid_spec=grid_spec,
        out_shape=jax.ShapeDtypeStruct((n, hd), BF16),
        compiler_params=_params("parallel", "parallel", "arbitrary"),
        name="nsa_selected",
    )(qb_tab, kb_tab, q, kvb, kvb, sel, expand)
    return o_c, o_s, o_w


def _nsa_shared_kv(xb, batch, w_kv, cmp_pe, cmp_w1, cmp_b1, cmp_w2, cmp_b2):
    n, d = xb.shape
    g, dk = NSA_KV_GROUPS, NSA_HEAD_DIM
    t = n // batch
    kv = _matmul(xb, w_kv.astype(BF16), tn=768)
    n_c = t // CMP_STRIDE - 1
    ncp = -(-n_c // LANES) * LANES
    rows = batch * n_c * g
    rows_p = -(-rows // 256) * 256
    outs = []
    for i in range(2):
        z = kv[:, i * g * dk:(i + 1) * g * dk].reshape(batch, t // CMP_STRIDE, CMP_STRIDE, g, dk)
        blk = jnp.concatenate([z[:, :-1], z[:, 1:]], axis=2)
        flat = blk.transpose(0, 1, 3, 2, 4).reshape(rows, CMP_LEN * dk)
        flat = jnp.pad(flat, ((0, rows_p - rows), (0, 0)))
        pe = jnp.broadcast_to(cmp_pe[i][:, None, :], (CMP_LEN, 1, dk)).reshape(CMP_LEN * dk)
        c = _cmp_mlp(flat, pe, cmp_w1[i], cmp_b1[i], cmp_w2[i], cmp_b2[i])[:rows]
        c = c.reshape(batch, n_c, g, dk).transpose(0, 2, 1, 3)
        outs.append(jnp.pad(c, ((0, 0), (0, 0), (0, ncp - n_c), (0, 0))))
    return kv.astype(BF16), outs[0], outs[1], n_c


def _gate_combine_body(oc_ref, os_ref, ow_ref, gl_ref, ex_ref, o_ref):
    gates = jax.nn.sigmoid(gl_ref[...])
    acc = None
    for i, r in enumerate((oc_ref, os_ref, ow_ref)):
        term = _dot_split_const(gates, ex_ref[i]) * r[...].astype(F32)
        acc = term if acc is None else acc + term
    o_ref[...] = acc.astype(BF16)


def _gate_combine(o_c, o_s, o_w, glog, nh, tm=256):
    n, hd = o_c.shape
    dk = hd // nh
    ex = np.zeros((3, LANES, hd), np.float32)
    for i in range(3):
        for h in range(nh):
            ex[i, i * nh + h, h * dk:(h + 1) * dk] = 1.0
    row = pl.BlockSpec((tm, hd), lambda i: (i, 0))
    return pl.pallas_call(
        _gate_combine_body,
        grid=(n // tm,),
        in_specs=[row, row, row, pl.BlockSpec((tm, LANES), lambda i: (i, 0)),
                  pl.BlockSpec((3, LANES, hd), lambda i: (0, 0, 0))],
        out_specs=row,
        out_shape=jax.ShapeDtypeStruct((n, hd), BF16),
        compiler_params=_params("parallel"),
        name="nsa_gate_combine",
    )(o_c, o_s, o_w, glog, jnp.asarray(ex, BF16))


def _nsa_layer(xb, batch, shared, w_in, b_gate, w_o):
    kvb, k_cmp, v_cmp, n_c = shared
    n, d = xb.shape
    hd = w_o.shape[0]
    nh = hd // NSA_HEAD_DIM
    q = _matmul(xb, w_in[:, :hd].astype(BF16), out_dtype=BF16,
                out_scale=NSA_HEAD_DIM ** -0.5 * math.log2(math.e))
    wg = jnp.pad(w_in[:, hd:], ((0, 0), (0, LANES - 3 * nh))).astype(BF16)
    bg = jnp.pad(b_gate, (0, LANES - 3 * nh))
    glog = _matmul(xb, wg, bias=bg)
    o_c, o_s, o_w = _nsa_attention(q, kvb, k_cmp, v_cmp, batch, n_c)
    o = _gate_combine(o_c, o_s, o_w, glog, nh)
    return _matmul(o, w_o.astype(BF16))


def kernel(x, ln_g, ln_b, rw_mu, rw_w_rkv, rw_w0, rw_w1, rw_w2, rw_a0, rw_a1, rw_a2, rw_g1, rw_g2, rw_k_k, rw_k_a, rw_r_k, rw_lnx_g, rw_lnx_b, rw_w_o, nsa_w_kv, nsa_cmp_pe, nsa_cmp_w1, nsa_cmp_b1, nsa_cmp_w2, nsa_cmp_b2, nsa_w_in, nsa_b_gate, nsa_w_o, moe_router_w, moe_router_b, moe_w_gu, moe_b_gu, moe_w_down, moe_b_down):
    batch, t, d = x.shape
    depth = ln_g.shape[0]
    n_a = rw_mu.shape[0]
    alpha = (2 * depth) ** 0.25
    h = x.reshape(batch * t, d)
    hb = None
    shared = None
    for layer in range(depth):
        if layer < n_a:
            i = layer
            mix = _rwkv_time_mix(h, batch, rw_mu[i], rw_w_rkv[i], rw_w0[i], rw_w1[i], rw_w2[i], rw_a0[i], rw_a1[i],
                                 rw_a2[i], rw_g1[i], rw_g2[i], rw_k_k[i], rw_k_a[i], rw_r_k[i], rw_lnx_g[i],
                                 rw_lnx_b[i], rw_w_o[i])
        else:
            if shared is None:
                if hb is None:
                    hb = h.astype(BF16)
                shared = _nsa_shared_kv(hb, batch, nsa_w_kv, nsa_cmp_pe, nsa_cmp_w1, nsa_cmp_b1, nsa_cmp_w2,
                                        nsa_cmp_b2)
            j = layer - n_a
            mix = _nsa_layer(hb, batch, shared, nsa_w_in[j], nsa_b_gate[j], nsa_w_o[j])
        h, hb = _add_ln(h, mix, ln_g[layer, 0], ln_b[layer, 0], alpha)
        y4, gates = _moe_ffn(hb, layer, moe_router_w[layer], moe_router_b[layer], moe_w_gu, moe_b_gu, moe_w_down,
                             moe_b_down)
        h, hb = _combine_ln(h, y4, gates, ln_g[layer, 1], ln_b[layer, 1], alpha)
    return h.reshape(batch, t, d)
```

```python
import functools
import math

import numpy as np
import jax
import jax.numpy as jnp
from jax import lax
from jax.experimental import pallas as pl
from jax.experimental.pallas import tpu as pltpu

F32 = jnp.float32
BF16 = jnp.bfloat16

V7X_VMEM_LIMIT_BYTES = 56 * 1024 * 1024
LANES = 128

LN_EPS = 1e-5
RW_HEAD_DIM = 64
RW_GN_EPS = 64e-5
RW_CHUNK = 64
RW_CHUNKS_PER_STEP = 8
NSA_HEAD_DIM = 128
NSA_KV_GROUPS = 2
CMP_STRIDE = 16
CMP_LEN = 32
SEL_BLOCK = 64
N_SEL = 16
WINDOW = 512
Q_BLOCK = 128
SEL_KEY_TILE = 1024
SEL_Q_TILE = 256
SEL_HEADS_PER_DOT = 2
CMP_HEADS_PER_DOT = 4
CMP_SEG_COLS = 256
N_EXPERTS = 32
TOP_K = 4
SWIGLU_LIMIT = 7.0
SWIGLU_ALPHA = 1.702
MOE_ROW_BLOCK = 1024
MOE_F_TILE = 512
MOE_OUT_TILE = 1024
NEG_BIG = -1e30


def _params(*sem):
    return pltpu.CompilerParams(dimension_semantics=sem, vmem_limit_bytes=V7X_VMEM_LIMIT_BYTES)


def _bdot(a, b):
    return jnp.dot(a.astype(BF16), b.astype(BF16), preferred_element_type=F32)


def _bdot_nt(a, b):
    return lax.dot_general(a.astype(BF16), b.astype(BF16), (((1,), (1,)), ((), ())),
                           preferred_element_type=F32)


def _dot_const_split(c, x):
    hi = x.astype(BF16)
    lo = (x - hi.astype(F32)).astype(BF16)
    return (jnp.dot(c, hi, preferred_element_type=F32) + jnp.dot(c, lo, preferred_element_type=F32))


def _dot_split_const(x, c):
    hi = x.astype(BF16)
    lo = (x - hi.astype(F32)).astype(BF16)
    return (jnp.dot(hi, c, preferred_element_type=F32) + jnp.dot(lo, c, preferred_element_type=F32))


def _mm_body(a_ref, w_ref, b_ref, o_ref, *, out_scale):
    acc = jnp.dot(a_ref[...], w_ref[...], preferred_element_type=F32) + b_ref[...]
    if out_scale is not None:
        acc = acc * out_scale
    o_ref[...] = acc.astype(o_ref.dtype)


def _matmul(a, w, bias=None, out_dtype=F32, out_scale=None, tm=512, tn=1024):
    m, k = a.shape
    n = w.shape[1]
    tm = min(tm, m)
    tn = min(tn, n)
    assert m % tm == 0 and n % tn == 0, (m, n, tm, tn)
    if bias is None:
        bias = jnp.zeros((1, n), F32)
    return pl.pallas_call(
        functools.partial(_mm_body, out_scale=out_scale),
        grid=(n // tn, m // tm),
        in_specs=[pl.BlockSpec((tm, k), lambda j, i: (i, 0)),
                  pl.BlockSpec((k, tn), lambda j, i: (0, j)),
                  pl.BlockSpec((1, tn), lambda j, i: (0, j))],
        out_specs=pl.BlockSpec((tm, tn), lambda j, i: (i, j)),
        out_shape=jax.ShapeDtypeStruct((m, n), out_dtype),
        compiler_params=_params("parallel", "parallel"),
        name="dense_matmul",
    )(a, w, bias.reshape(1, n).astype(F32))


def _layer_norm_rows(z, g, b):
    mu = jnp.mean(z, -1, keepdims=True)
    zc = z - mu
    var = jnp.mean(zc * zc, -1, keepdims=True)
    return zc * lax.rsqrt(var + LN_EPS) * g + b


def _proj_ln_body(a_ref, w_ref, x_ref, g_ref, b_ref, o_ref, ob_ref, *, alpha):
    z = alpha * x_ref[...] + jnp.dot(a_ref[...], w_ref[...], preferred_element_type=F32)
    y = _layer_norm_rows(z, g_ref[...], b_ref[...])
    o_ref[...] = y
    ob_ref[...] = y.astype(BF16)


def _proj_ln(a, w, x, g, b, alpha, tm=512):
    n, d = x.shape
    k = w.shape[0]
    tm = min(tm, n)
    row = pl.BlockSpec((tm, d), lambda i: (i, 0))
    vec = pl.BlockSpec((1, d), lambda i: (0, 0))
    return pl.pallas_call(
        functools.partial(_proj_ln_body, alpha=alpha),
        grid=(n // tm,),
        in_specs=[pl.BlockSpec((tm, k), lambda i: (i, 0)), pl.BlockSpec((k, d), lambda i: (0, 0)), row, vec, vec],
        out_specs=[row, row],
        out_shape=[jax.ShapeDtypeStruct((n, d), F32), jax.ShapeDtypeStruct((n, d), BF16)],
        compiler_params=_params("parallel"),
        name="proj_residual_layer_norm",
    )(a, w, x, g.reshape(1, d), b.reshape(1, d))


def _rw_mix_lowrank_body(x_ref, last_ref, mu_ref, w1_ref, w2_ref, w0_ref, a1_ref, a2_ref, a0_ref, g1_ref, g2_ref,
                         xr_ref, xk_ref, xv_ref, lw_ref, a_ref, g_ref):
    x = x_ref[...]
    prev = pltpu.roll(x, shift=1, axis=0)
    row = lax.broadcasted_iota(jnp.int32, x.shape, 0)
    prev = jnp.where(row == 0, last_ref[0], prev)
    xx = prev - x
    mixed = lambda i: (x + xx * mu_ref[i:i + 1, :]).astype(BF16)
    xr_ref[...] = mixed(0)
    xk_ref[...] = mixed(2)
    xv_ref[...] = mixed(3)
    z = w0_ref[...] + _bdot(jnp.tanh(jnp.dot(mixed(1), w1_ref[...], preferred_element_type=F32)), w2_ref[...])
    w_log = jnp.minimum(z, 0.0) - jnp.log(1.0 + jnp.exp(-jnp.abs(z))) - 0.5
    lw_ref[...] = -jnp.exp(w_log)
    za = a0_ref[...] + _bdot(jnp.dot(mixed(4), a1_ref[...], preferred_element_type=F32), a2_ref[...])
    a_ref[...] = jax.nn.sigmoid(za)
    hg = jax.nn.sigmoid(jnp.dot(mixed(5), g1_ref[...], preferred_element_type=F32))
    g_ref[...] = _bdot(hg, g2_ref[...])


def _pad_rank(w_in, w_out):
    r = w_in.shape[1]
    rp = -(-r // LANES) * LANES
    return (jnp.pad(w_in, ((0, 0), (0, rp - r))).astype(BF16), jnp.pad(w_out, ((0, rp - r), (0, 0))).astype(BF16))


def _rw_mix_lowrank(x2, batch, mu, w0, w1, w2, a0, a1, a2, g1, g2, tm=256):
    n, d = x2.shape
    t = n // batch
    nt = t // tm
    last = x2.reshape(batch, nt, tm, d)[:, :, tm - 1, :]
    last = jnp.concatenate([jnp.zeros((batch, 1, d), F32), last[:, :-1]], axis=1).reshape(batch * nt, 1, d)
    w1p, w2p = _pad_rank(w1, w2)
    a1p, a2p = _pad_rank(a1, a2)
    g1p, g2p = _pad_rank(g1, g2)
    row = pl.BlockSpec((tm, d), lambda i: (i, 0))
    full = lambda arr: pl.BlockSpec(arr.shape, lambda i: (0, 0))
    w0r, a0r = w0.reshape(1, d), a0.reshape(1, d)
    return pl.pallas_call(
        _rw_mix_lowrank_body,
        grid=(n // tm,),
        in_specs=[row, pl.BlockSpec((1, 1, d), lambda i: (i, 0, 0)), full(mu), full(w1p), full(w2p), full(w0r),
                  full(a1p), full(a2p), full(a0r), full(g1p), full(g2p)],
        out_specs=[row] * 6,
        out_shape=[jax.ShapeDtypeStruct((n, d), BF16)] * 3 + [jax.ShapeDtypeStruct((n, d), F32)] * 3,
        compiler_params=_params("parallel"),
        name="rwkv_shift_lowrank",
    )(x2, last, mu, w1p, w2p, w0r, a1p, a2p, a0r, g1p, g2p)


def _wkv_body(r_ref, k_ref, v_ref, lw_ref, a_ref, g_ref, kk_ref, ka_ref, rk_ref, lng_ref, lnb_ref,
              o_ref, s_ref, rp_ref, yq_ref, bonus_ref, gs_ref, *, nchunk):
    L = RW_CHUNK
    H2 = 2 * L

    step = pl.program_id(2)
    slot_w = step % 2
    slot_r = 1 - slot_w

    @pl.when(step == 0)
    def _():
        s_ref[...] = jnp.zeros_like(s_ref)
        rp_ref[1] = jnp.zeros(rp_ref.shape[1:], F32)
        yq_ref[1] = jnp.zeros(yq_ref.shape[1:], F32)
        bonus_ref[1] = jnp.zeros(bonus_ref.shape[1:], F32)
        gs_ref[1] = jnp.zeros(gs_ref.shape[1:], F32)

    lane = lax.broadcasted_iota(jnp.int32, (1, LANES), 1)
    mask0 = (lane < RW_HEAD_DIM).astype(F32)
    mask1 = 1.0 - mask0
    ri = lax.broadcasted_iota(jnp.int32, (H2, H2), 0)
    ci = lax.broadcasted_iota(jnp.int32, (H2, H2), 1)
    same_head = (ri // L) == (ci // L)
    strict = (same_head & (ci < ri)).astype(F32)
    incl = (same_head & (ci <= ri)).astype(F32)
    diag16 = ((ri // 16) == (ci // 16)).astype(F32)
    eye = (ri == ci).astype(F32)
    head_ones = same_head.astype(BF16)
    tl = lax.broadcasted_iota(jnp.int32, (L, L), 0)
    sl = lax.broadcasted_iota(jnp.int32, (L, L), 1)
    tri_incl = (sl <= tl).astype(BF16)

    def stack(x):
        return jnp.concatenate([x * mask0, x * mask1], axis=0)

    k_k = kk_ref[...]
    k_a = ka_ref[...]
    r_k = rk_ref[...]

    chunks = range(nchunk)

    rp_prev = [rp_ref[slot_r, c] for c in chunks]
    yq_prev = [yq_ref[slot_r, c] for c in chunks]
    chain = {"s": s_ref[...], "ys": []}

    def chain_step():
        c = len(chain["ys"])
        if c >= nchunk:
            return
        res = _bdot(rp_prev[c], chain["s"]) + yq_prev[c]
        chain["ys"].append(res[:L] + res[L:H2])
        chain["s"] = res[H2:]
        if c == nchunk - 1:
            s_ref[...] = chain["s"]
            y = jnp.concatenate(chain["ys"], axis=0)
            inv_n = 1.0 / RW_HEAD_DIM
            ym = _dot_split_const(y, head_ones) * inv_n
            yc = y - ym
            yv = _dot_split_const(yc * yc, head_ones) * inv_n
            yn = yc * lax.rsqrt(yv + RW_GN_EPS) * lng_ref[...] + lnb_ref[...]
            o_ref[...] = ((yn + bonus_ref[slot_r]) * gs_ref[slot_r]).astype(BF16)

    def each(fn, *lists):
        return [fn(*xs) for xs in zip(*lists)]

    def mm_stage(fn, *lists):
        out = each(fn, *lists)
        chain_step()
        return out

    def rows_of(x):
        return [x[c * L:(c + 1) * L] for c in chunks]

    r_all = r_ref[...]
    k_all = k_ref[...]
    v_all = v_ref[...]
    ag_all = a_ref[...]
    kk_all = k_all * k_k
    ss_all = _dot_split_const(kk_all * kk_all, head_ones)
    kk_all = kk_all / jnp.maximum(jnp.sqrt(ss_all), 1e-12)
    kmod_all = k_all * (1.0 + (ag_all - 1.0) * k_a)
    bv_all = kk_all * ag_all
    lw_c = rows_of(lw_ref[...])
    cl_c = each(lambda lw: _dot_const_split(tri_incl, lw), lw_c)
    last_c = each(lambda cl: cl[L - 1:L, :], cl_c)
    cl_all = jnp.concatenate(cl_c, axis=0)
    clp_all = cl_all - lw_ref[...]
    end_all = jnp.concatenate(each(lambda cl, la: la - cl, cl_c, last_c), axis=0)
    e_neg = jnp.exp(-cl_all)
    e_end = jnp.exp(end_all)
    at_c = rows_of(-kk_all * jnp.exp(clp_all))
    rt_c = rows_of(r_all * jnp.exp(cl_all))
    bt_c = rows_of(bv_all * e_neg)
    kt_c = rows_of(kmod_all * e_neg)
    be_c = rows_of(bv_all * e_end)
    ke_c = rows_of(kmod_all * e_end)
    v_s = each(stack, rows_of(v_all))
    at_s = each(stack, at_c)
    rt_s = each(stack, rt_c)
    gmat = mm_stage(lambda a_, r_, b_, k_: _bdot_nt(jnp.concatenate([a_, r_], axis=0),
                                                    jnp.concatenate([stack(b_), stack(k_)], axis=0)),
                    at_s, rt_s, bt_c, kt_c)
    a_ab = each(lambda gm: gm[:H2, :H2] * strict, gmat)
    a_ak = each(lambda gm: gm[:H2, H2:] * strict, gmat)
    a_rb = each(lambda gm: gm[H2:, :H2] * incl, gmat)
    a_rk = each(lambda gm: gm[H2:, H2:] * incl, gmat)
    dblk = each(lambda a_: a_ * diag16, a_ab)
    off = each(lambda a_, d_: a_ - d_, a_ab, dblk)
    d2 = mm_stage(_bdot, dblk, dblk)
    d4 = mm_stage(_bdot, d2, d2)
    d8 = mm_stage(_bdot, d4, d4)
    dinv = mm_stage(lambda d_, d2_: _bdot(eye + d_, eye + d2_), dblk, d2)
    dinv = mm_stage(lambda di, d4_: _bdot(di, eye + d4_), dinv, d4)
    dinv = mm_stage(lambda di, d8_: _bdot(di, eye + d8_), dinv, d8)
    e1 = mm_stage(_bdot, dinv, off)
    e2 = mm_stage(_bdot, e1, e1)
    minv = mm_stage(lambda e1_, e2_: _bdot(eye + e1_, eye + e2_), e1, e2)
    minv = mm_stage(_bdot, minv, dinv)
    x_ak = mm_stage(_bdot, a_ak, v_s)
    zu = each(lambda mi, a_, x_: _bdot(mi, jnp.concatenate([a_, x_], axis=1)), minv, at_s, x_ak)
    w2 = each(_bdot, a_rb, zu)
    rkv = each(_bdot, a_rk, v_s)
    pq = each(lambda b_, z_: _bdot(stack(b_).T, z_), be_c, zu)
    kv2 = each(lambda k_, v_: _bdot(stack(k_).T, v_), ke_c, v_s)
    while len(chain["ys"]) < nchunk:
        chain_step()
    for c in chunks:
        rp_ref[slot_w, c] = jnp.concatenate([rt_s[c] + w2[c][:, :LANES],
                                             eye * jnp.exp(last_c[c]) + pq[c][:, :LANES]], axis=0)
        yq_ref[slot_w, c] = jnp.concatenate([w2[c][:, LANES:] + rkv[c], pq[c][:, LANES:] + kv2[c]], axis=0)
    bonus_ref[slot_w] = _dot_split_const(r_all * kmod_all * r_k, head_ones) * v_all
    gs_ref[slot_w] = g_ref[...]


def _wkv(r, k, v, lw, a, g, k_k, k_a, r_k, lnx_g, lnx_b, batch):
    n, d = r.shape
    t = n // batch
    nchunk = RW_CHUNKS_PER_STEP
    tb = RW_CHUNK * nchunk
    while t % tb:
        nchunk //= 2
        tb = RW_CHUNK * nchunk
    nt = t // tb
    row_in = pl.BlockSpec((tb, LANES), lambda b, hp, c: (b * nt + jnp.minimum(c, nt - 1), hp))
    row_out = pl.BlockSpec((tb, LANES), lambda b, hp, c: (b * nt + jnp.maximum(c - 1, 0), hp))
    vec = pl.BlockSpec((1, LANES), lambda b, hp, c: (0, hp))
    sq = pltpu.VMEM((2, nchunk, 2 * LANES, LANES), F32)
    blk = pltpu.VMEM((2, tb, LANES), F32)
    vecs = [z.reshape(1, d) for z in (k_k, k_a, r_k, lnx_g, lnx_b)]
    return pl.pallas_call(
        functools.partial(_wkv_body, nchunk=nchunk),
        grid=(batch, d // LANES, nt + 1),
        in_specs=[row_in] * 6 + [vec] * 5,
        out_specs=row_out,
        out_shape=jax.ShapeDtypeStruct((n, d), BF16),
        scratch_shapes=[pltpu.VMEM((LANES, LANES), F32), sq, sq, blk, blk],
        compiler_params=_params("parallel", "parallel", "arbitrary"),
        name="rwkv_chunked_scan",
    )(r, k, v, lw, a, g, *vecs)


def _rwkv_time_mix(x2, batch, mu, w_rkv, w0, w1, w2, a0, a1, a2, g1, g2, k_k, k_a, r_k, lnx_g, lnx_b, w_o):
    xr, xk, xv, lw, a, g = _rw_mix_lowrank(x2, batch, mu, w0, w1, w2, a0, a1, a2, g1, g2)
    r = _matmul(xr, w_rkv[0].astype(BF16))
    k = _matmul(xk, w_rkv[1].astype(BF16))
    v = _matmul(xv, w_rkv[2].astype(BF16))
    z = _wkv(r, k, v, lw, a, g, k_k, k_a, r_k.reshape(-1), lnx_g, lnx_b, batch)
    return z, w_o.astype(BF16)


def _router_body(x_ref, w_ref, b_ref, idx_ref, gate_ref):
    logits = jnp.dot(x_ref[...], w_ref[...], preferred_element_type=F32) + b_ref[...]
    lane = lax.broadcasted_iota(jnp.int32, logits.shape, 1).astype(F32)
    cur = logits
    vals, idxs = [], []
    for _ in range(TOP_K):
        m = jnp.max(cur, axis=-1, keepdims=True)
        i = jnp.min(jnp.where(cur == m, lane, float(LANES)), axis=-1, keepdims=True)
        vals.append(m)
        idxs.append(i)
        cur = jnp.where(lane == i, -3e38, cur)
    es = [jnp.exp(vv - vals[0]) for vv in vals]
    den = es[0]
    for e in es[1:]:
        den = den + e
    idx_out = jnp.zeros(logits.shape, F32)
    gate_out = jnp.zeros(logits.shape, F32)
    for kk in range(TOP_K):
        idx_out = jnp.where(lane == kk, idxs[kk], idx_out)
        gate_out = jnp.where(lane == kk, es[kk] / den, gate_out)
    idx_ref[...] = idx_out.astype(jnp.int32)
    gate_ref[...] = gate_out


def _router(xb, router_w, router_b, tm=512):
    n, d = xb.shape
    e = router_w.shape[1]
    wp = jnp.pad(router_w, ((0, 0), (0, LANES - e))).astype(BF16)
    bp = jnp.concatenate([router_b.astype(F32), jnp.full((LANES - e,), NEG_BIG, F32)]).reshape(1, LANES)
    row = pl.BlockSpec((tm, LANES), lambda i: (i, 0))
    idx, gate = pl.pallas_call(
        _router_body,
        grid=(n // tm,),
        in_specs=[pl.BlockSpec((tm, d), lambda i: (i, 0)), pl.BlockSpec((d, LANES), lambda i: (0, 0)),
                  pl.BlockSpec((1, LANES), lambda i: (0, 0))],
        out_specs=[row, row],
        out_shape=[jax.ShapeDtypeStruct((n, LANES), jnp.int32), jax.ShapeDtypeStruct((n, LANES), F32)],
        compiler_params=_params("parallel"),
        name="moe_router",
    )(xb, wp, bp)
    return idx[:, :TOP_K], gate


def _moe_up_body(it_ref, ct_ref, be_ref, nu_ref, x_ref, wg_ref, wu_ref, bg_ref, bu_ref, o_ref):
    used = it_ref[pl.program_id(0)] < nu_ref[0]

    @pl.when(used)
    def _():
        x = x_ref[...]
        gate = jnp.dot(x, wg_ref[...].astype(BF16), preferred_element_type=F32) + bg_ref[...]
        up = jnp.dot(x, wu_ref[...].astype(BF16), preferred_element_type=F32) + bu_ref[...]
        gate = jnp.minimum(gate, SWIGLU_LIMIT)
        up = jnp.clip(up, -SWIGLU_LIMIT, SWIGLU_LIMIT)
        o_ref[...] = ((up + 1.0) * (gate * jax.nn.sigmoid(gate * SWIGLU_ALPHA))).astype(o_ref.dtype)

    @pl.when(jnp.logical_not(used))
    def _():
        o_ref[...] = jnp.zeros_like(o_ref)


def _moe_down_body(it_ref, ct_ref, be_ref, nu_ref, a_ref, wd_ref, bd_ref, o_ref):
    used = it_ref[pl.program_id(0)] < nu_ref[0]

    @pl.when(used)
    def _():
        y = jnp.dot(a_ref[...], wd_ref[...].astype(BF16), preferred_element_type=F32) + bd_ref[...]
        o_ref[...] = y.astype(o_ref.dtype)

    @pl.when(jnp.logical_not(used))
    def _():
        o_ref[...] = jnp.zeros_like(o_ref)


def _expert_major_steps(blk_e, n_blk, n_col):
    col = jnp.repeat(jnp.arange(n_col, dtype=jnp.int32), n_blk)
    blk = jnp.tile(jnp.arange(n_blk, dtype=jnp.int32), n_col)
    experts = jnp.arange(N_EXPERTS, dtype=jnp.int32)
    first = jnp.sum((blk_e[None, :] < experts[:, None]).astype(jnp.int32), axis=1)
    count = jnp.sum((blk_e[None, :] == experts[:, None]).astype(jnp.int32), axis=1)
    e = blk_e[blk]
    pos = n_col * first[e] + col * count[e] + (blk - first[e])
    zeros = jnp.zeros((n_col * n_blk,), jnp.int32)
    return zeros.at[pos].set(blk), zeros.at[pos].set(col)


def _moe_experts(xs, blk_e, n_used, layer, w_gu, b_gu, w_down, b_down):
    n_rows, d = xs.shape
    depth, ne, _, f2 = w_gu.shape
    fdim = f2 // 2
    tm, tf, tn = MOE_ROW_BLOCK, min(MOE_F_TILE, fdim), min(MOE_OUT_TILE, d)
    nf, nn = fdim // tf, d // tn
    n_blk = n_rows // tm
    bgu = b_gu.reshape(depth, ne, 1, f2)

    it, ct = _expert_major_steps(blk_e, n_blk, nf)
    act = pl.pallas_call(
        _moe_up_body,
        grid_spec=pltpu.PrefetchScalarGridSpec(
            num_scalar_prefetch=4,
            grid=(nf * n_blk,),
            in_specs=[
                pl.BlockSpec((tm, d), lambda t, it, ct, be, nu: (jnp.minimum(it[t], nu[0] - 1), 0)),
                pl.BlockSpec((None, None, d, tf), lambda t, it, ct, be, nu: (layer, be[it[t]], 0, ct[t])),
                pl.BlockSpec((None, None, d, tf), lambda t, it, ct, be, nu: (layer, be[it[t]], 0, nf + ct[t])),
                pl.BlockSpec((None, None, 1, tf), lambda t, it, ct, be, nu: (layer, be[it[t]], 0, ct[t])),
                pl.BlockSpec((None, None, 1, tf), lambda t, it, ct, be, nu: (layer, be[it[t]], 0, nf + ct[t])),
            ],
            out_specs=pl.BlockSpec((tm, tf), lambda t, it, ct, be, nu: (it[t], ct[t])),
        ),
        out_shape=jax.ShapeDtypeStruct((n_rows, fdim), BF16),
        compiler_params=_params("arbitrary"),
        name="moe_experts_up",
    )(it, ct, blk_e, n_used, xs, w_gu, w_gu, bgu, bgu)

    it2, ct2 = _expert_major_steps(blk_e, n_blk, nn)
    return pl.pallas_call(
        _moe_down_body,
        grid_spec=pltpu.PrefetchScalarGridSpec(
            num_scalar_prefetch=4,
            grid=(nn * n_blk,),
            in_specs=[
                pl.BlockSpec((tm, fdim), lambda t, it, ct, be, nu: (jnp.minimum(it[t], nu[0] - 1), 0)),
                pl.BlockSpec((None, None, fdim, tn), lambda t, it, ct, be, nu: (layer, be[it[t]], 0, ct[t])),
                pl.BlockSpec((None, None, 1, tn), lambda t, it, ct, be, nu: (layer, be[it[t]], 0, ct[t])),
            ],
            out_specs=pl.BlockSpec((tm, tn), lambda t, it, ct, be, nu: (it[t], ct[t])),
        ),
        out_shape=jax.ShapeDtypeStruct((n_rows, d), BF16),
        compiler_params=_params("arbitrary"),
        name="moe_experts_down",
    )(it2, ct2, blk_e, n_used, act, w_down, b_down.reshape(depth, ne, 1, d))


def _combine_ln_body(x_ref, y_ref, gate_ref, g_ref, b_ref, o_ref, ob_ref, *, alpha):
    ffn = y_ref[0].astype(F32) * gate_ref[:, 0:1]
    for kk in range(1, TOP_K):
        ffn = ffn + y_ref[kk].astype(F32) * gate_ref[:, kk:kk + 1]
    z = alpha * x_ref[...] + ffn
    mu = jnp.mean(z, -1, keepdims=True)
    zc = z - mu
    var = jnp.mean(zc * zc, -1, keepdims=True)
    y = zc * lax.rsqrt(var + LN_EPS) * g_ref[...] + b_ref[...]
    o_ref[...] = y
    ob_ref[...] = y.astype(BF16)


def _combine_ln(x, y4, gates, g, b, alpha, tm=256):
    n, d = x.shape
    row = pl.BlockSpec((tm, d), lambda i: (i, 0))
    vec = pl.BlockSpec((1, d), lambda i: (0, 0))
    return pl.pallas_call(
        functools.partial(_combine_ln_body, alpha=alpha),
        grid=(n // tm,),
        in_specs=[row, pl.BlockSpec((TOP_K, tm, d), lambda i: (0, i, 0)),
                  pl.BlockSpec((tm, LANES), lambda i: (i, 0)), vec, vec],
        out_specs=[row, row],
        out_shape=[jax.ShapeDtypeStruct((n, d), F32), jax.ShapeDtypeStruct((n, d), BF16)],
        compiler_params=_params("parallel"),
        name="moe_combine_layer_norm",
    )(x, y4, gates, g.reshape(1, d), b.reshape(1, d))


def _moe_ffn(xb, layer, router_w, router_b, w_gu, b_gu, w_down, b_down):
    n, d = xb.shape
    tm = MOE_ROW_BLOCK
    top_i, gates = _router(xb, router_w, router_b)
    flat_e = top_i.reshape(-1)
    onehot = (flat_e[:, None] == jnp.arange(N_EXPERTS, dtype=jnp.int32)[None, :]).astype(jnp.int32)
    csum = jnp.cumsum(onehot, axis=0)
    rank = jnp.take_along_axis(csum, flat_e[:, None], axis=1)[:, 0] - 1
    counts = csum[-1]
    padded = ((counts + tm - 1) // tm) * tm
    pad_end = jnp.cumsum(padded)
    pad_start = pad_end - padded
    dest = pad_start[flat_e] + rank
    n_rows = -(-(n * TOP_K) // tm) * tm + N_EXPERTS * tm
    n_blk = n_rows // tm
    blk_start = jnp.arange(n_blk, dtype=jnp.int32) * tm
    blk_e = jnp.minimum(jnp.sum((pad_end[None, :] <= blk_start[:, None]).astype(jnp.int32), axis=1),
                        N_EXPERTS - 1).astype(jnp.int32)
    n_used = (pad_end[-1] // tm).astype(jnp.int32).reshape(1)
    flat_tok = jnp.arange(n * TOP_K, dtype=jnp.int32) // TOP_K
    row_tok = (jnp.arange(n_rows, dtype=jnp.int32) % n).at[dest].set(flat_tok)
    xs = jnp.take(xb, row_tok, axis=0, mode='clip')
    ys = _moe_experts(xs, blk_e, n_used, layer, w_gu, b_gu, w_down, b_down)
    dest_kmajor = dest.reshape(n, TOP_K).T.reshape(-1)
    y4 = jnp.take(ys, dest_kmajor, axis=0, mode='clip').reshape(TOP_K, n, d)
    return y4, gates


def _cmp_mlp_body(x_ref, pe_ref, w1_ref, b1_ref, w2_ref, b2_ref, o_ref):
    h = _bdot(x_ref[...] + pe_ref[...], w1_ref[...]) + b1_ref[...]
    h = jax.nn.gelu(h)
    o_ref[...] = (_bdot(h, w2_ref[...]) + b2_ref[...]).astype(o_ref.dtype)


def _cmp_mlp(flat, pe, w1, b1, w2, b2, tm=256):
    m, kd = flat.shape
    hid = w1.shape[1]
    dk = w2.shape[1]
    full = lambda shp: pl.BlockSpec(shp, lambda i: (0, 0))
    return pl.pallas_call(
        _cmp_mlp_body,
        grid=(m // tm,),
        in_specs=[pl.BlockSpec((tm, kd), lambda i: (i, 0)), full((1, kd)), full((kd, hid)), full((1, hid)),
                  full((hid, dk)), full((1, dk))],
        out_specs=pl.BlockSpec((tm, dk), lambda i: (i, 0)),
        out_shape=jax.ShapeDtypeStruct((m, dk), BF16),
        compiler_params=_params("parallel"),
        name="nsa_compress_mlp",
    )(flat, pe.reshape(1, kd), w1.astype(BF16), b1.reshape(1, hid), w2.astype(BF16), b2.reshape(1, dk))


def _group_rows(q_ref, hpg):
    dk = NSA_HEAD_DIM
    return jnp.concatenate([q_ref[:, h * dk:(h + 1) * dk] for h in range(hpg)], axis=0)


def _softmax_rows(s, mask):
    s = jnp.where(mask, s, NEG_BIG)
    m = jnp.max(s, axis=-1, keepdims=True)
    e = jnp.where(mask, jnp.exp2(s - m), 0.0)
    den = jnp.sum(e, axis=-1, keepdims=True)
    return e / jnp.where(den > 0, den, 1.0)


def _exp2_rows(s_rows, bias, cols):
    s_h = [s_rows[:, cj] + bias[:, cj] for cj in cols]
    mx = s_h[0]
    for s_hj in s_h[1:]:
        mx = jnp.maximum(mx, s_hj)
    m = jnp.broadcast_to(jnp.max(mx, axis=-1, keepdims=True), mx.shape)
    return jnp.concatenate([jnp.exp2(s_hj - m).astype(BF16) for s_hj in s_h], axis=1)


def _nsa_cmp_body(q_ref, kc_ref, vc_ref, ov_ref, *rest, hpg, n_c, n_sel, qb0):
    o_ref, sel_ref = rest[-2:]
    qb = pl.program_id(2) + qb0
    t0 = qb * Q_BLOCK
    ncols = kc_ref.shape[0]
    n_s = ov_ref.shape[1]
    dk = NSA_HEAD_DIM
    tq_c = t0 + lax.broadcasted_iota(jnp.int32, (Q_BLOCK, ncols), 0)
    cid = lax.broadcasted_iota(jnp.int32, (Q_BLOCK, ncols), 1)
    bias = jnp.where((cid * CMP_STRIDE + (CMP_LEN - 1) <= tq_c) & (cid < n_c), 0.0, NEG_BIG)
    t_row = t0 + lax.broadcasted_iota(jnp.int32, (Q_BLOCK, LANES), 0)
    row_live = t_row >= CMP_LEN - 1
    cols = [slice(j * LANES, (j + 1) * LANES) for j in range(ncols // LANES)]
    k_tile = kc_ref[...]
    v_aug = jnp.concatenate([vc_ref[...], jnp.ones((ncols, LANES), BF16)], axis=1)
    ov = ov_ref[...]
    hg = min(CMP_HEADS_PER_DOT, hpg)
    imp = jnp.zeros((Q_BLOCK, n_s), F32)
    for g0 in range(0, hpg, hg):
        qg = jnp.concatenate([q_ref[:, h * dk:(h + 1) * dk] for h in range(g0, g0 + hg)], axis=0)
        s_g = _bdot_nt(qg, k_tile)
        e = jnp.concatenate([_exp2_rows(s_g[hl * Q_BLOCK:(hl + 1) * Q_BLOCK], bias, cols) for hl in range(hg)], axis=0)
        od = jnp.dot(e, v_aug, preferred_element_type=F32)
        ih = jnp.dot(e, ov, preferred_element_type=F32)
        for hl in range(hg):
            rows = slice(hl * Q_BLOCK, (hl + 1) * Q_BLOCK)
            inv = jnp.where(row_live, 1.0 / od[rows, dk:], 0.0)
            o_ref[:, (g0 + hl) * dk:(g0 + hl + 1) * dk] = (od[rows, :dk] * inv).astype(o_ref.dtype)
            inv_s = inv[:, :n_s] if n_s <= LANES else jnp.concatenate([inv] * (n_s // LANES), axis=1)
            imp = imp + ih[rows] * inv_s
    imp_t = imp.T
    tq = t0 + lax.broadcasted_iota(jnp.int32, (n_s, Q_BLOCK), 1)
    sid_i = lax.broadcasted_iota(jnp.int32, (n_s, Q_BLOCK), 0)
    cur = tq // SEL_BLOCK
    forced = (sid_i == 0) | (sid_i == cur) | (sid_i == cur - 1)
    score = jnp.where(sid_i * SEL_BLOCK <= tq, jnp.where(forced, 1e30, imp_t), -1.0)
    sid = sid_i.astype(F32)
    sel = jnp.zeros((n_s, Q_BLOCK), F32)
    for _ in range(n_sel):
        m = jnp.max(score, axis=0, keepdims=True)
        first = jnp.min(jnp.where(score == m, sid, float(n_s)), axis=0, keepdims=True)
        hit = sid == first
        sel = jnp.where(hit, 1.0, sel)
        score = jnp.where(hit, -2.0, score)
    sel_ref[...] = sel.T.astype(BF16)


def _nsa_win_body(q_ref, *refs, hpg, nwb):
    k_refs, v_refs, o_ref = refs[:nwb], refs[nwb:2 * nwb], refs[2 * nwb]
    qb = pl.program_id(2)
    t0 = qb * Q_BLOCK
    dk = NSA_HEAD_DIM
    nk = nwb * Q_BLOCK
    kcat = jnp.concatenate([r[...] for r in k_refs], axis=0)
    v_aug = jnp.concatenate([r[...] for r in v_refs], axis=0)
    v_aug = jnp.concatenate([v_aug, jnp.ones((nk, LANES), BF16)], axis=1)
    t = t0 + lax.broadcasted_iota(jnp.int32, (Q_BLOCK, nk), 0)
    kpos = t0 - WINDOW + lax.broadcasted_iota(jnp.int32, (Q_BLOCK, nk), 1)
    bias = jnp.where((kpos <= t) & (kpos > t - WINDOW) & (kpos >= 0), 0.0, NEG_BIG)
    cols = [slice(j * LANES, (j + 1) * LANES) for j in range(nk // LANES)]
    hg = min(CMP_HEADS_PER_DOT, hpg)
    for g0 in range(0, hpg, hg):
        qg = jnp.concatenate([q_ref[:, h * dk:(h + 1) * dk] for h in range(g0, g0 + hg)], axis=0)
        s_g = _bdot_nt(qg, kcat)
        e = jnp.concatenate([_exp2_rows(s_g[hl * Q_BLOCK:(hl + 1) * Q_BLOCK], bias, cols) for hl in range(hg)], axis=0)
        od = jnp.dot(e, v_aug, preferred_element_type=F32)
        for hl in range(hg):
            rows = slice(hl * Q_BLOCK, (hl + 1) * Q_BLOCK)
            o_ref[:, (g0 + hl) * dk:(g0 + hl + 1) * dk] = (od[rows, :dk] / od[rows, dk:]).astype(o_ref.dtype)


def _nsa_sel_body(qb_ref, kb_ref, q_ref, k_ref, v_ref, sel_ref, ex_ref, o_ref, m_ref, acc_ref, *, hpg):
    step = pl.program_id(2)
    qb = qb_ref[step]
    kb = kb_ref[step]
    qn = q_ref.shape[0]
    t0 = qb * qn
    tk = k_ref.shape[0]
    dk = NSA_HEAD_DIM

    @pl.when(kb == 0)
    def _():
        m_ref[...] = jnp.full_like(m_ref, NEG_BIG)
        acc_ref[...] = jnp.zeros_like(acc_ref)

    picked = jnp.dot(sel_ref[...], ex_ref[...], preferred_element_type=F32)
    tq = t0 + lax.broadcasted_iota(jnp.int32, (qn, tk), 0)
    kpos = kb * tk + lax.broadcasted_iota(jnp.int32, (qn, tk), 1)
    bias = jnp.where((picked > 0.5) & (kpos <= tq), 0.0, NEG_BIG)
    cols = [slice(j * LANES, (j + 1) * LANES) for j in range(tk // LANES)]
    k_tile = k_ref[...]
    v_aug = jnp.concatenate([v_ref[...], jnp.ones((tk, LANES), BF16)], axis=1)
    hg = SEL_HEADS_PER_DOT

    def scores(g0):
        qg = jnp.concatenate([q_ref[:, h * dk:(h + 1) * dk] for h in range(g0, g0 + hg)], axis=0)
        return _bdot_nt(qg, k_tile)

    s_next = scores(0)
    for g0 in range(0, hpg, hg):
        s_g = s_next
        if g0 + hg < hpg:
            s_next = scores(g0 + hg)
        p_rows, alphas = [], []
        for hl in range(hg):
            rows = slice((g0 + hl) * qn, (g0 + hl + 1) * qn)
            s_h = [s_g[hl * qn:(hl + 1) * qn, cj] + bias[:, cj] for cj in cols]
            mx = s_h[0]
            for s_hj in s_h[1:]:
                mx = jnp.maximum(mx, s_hj)
            m_old = m_ref[rows, :]
            m_new = jnp.maximum(m_old, jnp.broadcast_to(jnp.max(mx, axis=-1, keepdims=True), m_old.shape))
            m_ref[rows, :] = m_new
            alphas.append(jnp.exp2(m_old - m_new))
            p_rows.append(jnp.concatenate([jnp.exp2(s_hj - m_new).astype(BF16) for s_hj in s_h], axis=1))
        pv = jnp.dot(jnp.concatenate(p_rows, axis=0), v_aug, preferred_element_type=F32)
        alpha = jnp.concatenate(alphas, axis=0)
        grows = slice(g0 * qn, (g0 + hg) * qn)
        acc_ref[grows, :dk] = alpha * acc_ref[grows, :dk] + pv[:, :dk]
        acc_ref[grows, dk:] = alpha * acc_ref[grows, dk:] + pv[:, dk:]

    @pl.when(kb == (t0 + qn - 1) // tk)
    def _():
        den = acc_ref[:, dk:]
        o = acc_ref[:, :dk] / jnp.where(den > 0, den, 1.0)
        for h in range(hpg):
            o_ref[:, h * dk:(h + 1) * dk] = o[h * qn:(h + 1) * qn].astype(o_ref.dtype)


def _nsa_attention(q, kvb, k_cmp, v_cmp, batch, n_c):
    n, hd = q.shape
    dk, g = NSA_HEAD_DIM, NSA_KV_GROUPS
    hpg = hd // dk // g
    t = n // batch
    nqb = t // Q_BLOCK
    n_s = t // SEL_BLOCK
    n_sel = min(N_SEL, n_s)
    ncp = k_cmp.shape[2]
    gw = hpg * dk

    c_lo = np.arange(ncp) * CMP_STRIDE
    s_lo = np.arange(n_s) * SEL_BLOCK
    overlap = ((c_lo[:, None] < s_lo[None, :] + SEL_BLOCK) & (c_lo[:, None] + CMP_LEN > s_lo[None, :])
               & (np.arange(ncp)[:, None] < n_c))
    overlap = jnp.asarray(overlap, BF16)

    qspec = pl.BlockSpec((Q_BLOCK, gw), lambda b, gi, qb: (b * nqb + qb, gi))
    seg_qb = CMP_SEG_COLS * CMP_STRIDE // Q_BLOCK
    o_c = sel = None
    for qb0 in range(0, nqb, seg_qb):
        nq = min(seg_qb, nqb - qb0)
        ncols = min(ncp, -(-((qb0 + nq) * Q_BLOCK // CMP_STRIDE) // LANES) * LANES)
        oq = pl.BlockSpec((Q_BLOCK, gw), lambda b, gi, qb, qb0=qb0: (b * nqb + qb0 + qb, gi))
        in_specs = [oq,
                    pl.BlockSpec((None, None, ncols, dk), lambda b, gi, qb: (b, gi, 0, 0)),
                    pl.BlockSpec((None, None, ncols, dk), lambda b, gi, qb: (b, gi, 0, 0)),
                    pl.BlockSpec((ncols, n_s), lambda b, gi, qb: (0, 0))]
        args = [q, k_cmp, v_cmp, overlap]
        aliases = {}
        if o_c is not None:
            in_specs += [pl.BlockSpec(memory_space=pl.ANY), pl.BlockSpec(memory_space=pl.ANY)]
            args += [o_c, sel]
            aliases = {4: 0, 5: 1}
        o_c, sel = pl.pallas_call(
            functools.partial(_nsa_cmp_body, hpg=hpg, n_c=n_c, n_sel=n_sel, qb0=qb0),
            grid=(batch, g, nq),
            in_specs=in_specs,
            out_specs=[oq, pl.BlockSpec((None, None, Q_BLOCK, n_s), lambda b, gi, qb, qb0=qb0: (b, gi, qb0 + qb, 0))],
            out_shape=[jax.ShapeDtypeStruct((n, hd), BF16), jax.ShapeDtypeStruct((batch, g, t, n_s), BF16)],
            input_output_aliases=aliases,
            compiler_params=_params("parallel", "parallel", "parallel"),
            name="nsa_compressed_select",
        )(*args)

    nwb = WINDOW // Q_BLOCK + 1
    kcol, vcol = 4 * g, 5 * g

    def kv_spec(col, j):
        return pl.BlockSpec((Q_BLOCK, dk),
                            lambda b, gi, qb: (b * nqb + jnp.maximum(qb - (nwb - 1) + j, 0), col + gi))

    o_w = pl.pallas_call(
        functools.partial(_nsa_win_body, hpg=hpg, nwb=nwb),
        grid=(batch, g, nqb),
        in_specs=[qspec] + [kv_spec(kcol, j) for j in range(nwb)] + [kv_spec(vcol, j) for j in range(nwb)],
        out_specs=qspec,
        out_shape=jax.ShapeDtypeStruct((n, hd), BF16),
        compiler_params=_params("parallel", "parallel", "parallel"),
        name="nsa_window",
    )(q, *([kvb] * (2 * nwb)))

    tk = min(SEL_KEY_TILE, t)
    nkb = t // tk
    qn = min(SEL_Q_TILE, t)
    nqt = t // qn
    steps = [(qb, kb) for qb in range(nqt) for kb in range((qb * qn + qn - 1) // tk + 1)]
    qb_tab = jnp.asarray([s_[0] for s_ in steps], jnp.int32)
    kb_tab = jnp.asarray([s_[1] for s_ in steps], jnp.int32)
    expand = jnp.asarray(np.arange(n_s)[:, None] == (np.arange(t)[None, :] // SEL_BLOCK), BF16)
    kscol, vscol = 2 * g, 3 * g
    grid_spec = pltpu.PrefetchScalarGridSpec(
        num_scalar_prefetch=2,
        grid=(batch, g, len(steps)),
        in_specs=[
            pl.BlockSpec((qn, gw), lambda b, gi, s_, qt, kt: (b * nqt + qt[s_], gi)),
            pl.BlockSpec((tk, dk), lambda b, gi, s_, qt, kt: (b * nkb + kt[s_], kscol + gi)),
            pl.BlockSpec((tk, dk), lambda b, gi, s_, qt, kt: (b * nkb + kt[s_], vscol + gi)),
            pl.BlockSpec((None, None, qn, n_s), lambda b, gi, s_, qt, kt: (b, gi, qt[s_], 0)),
            pl.BlockSpec((n_s, tk), lambda b, gi, s_, qt, kt: (0, kt[s_])),
        ],
        out_specs=pl.BlockSpec((qn, gw), lambda b, gi, s_, qt, kt: (b * nqt + qt[s_], gi)),
        scratch_shapes=[pltpu.VMEM((hpg * qn, LANES), F32), pltpu.VMEM((hpg * qn, dk + LANES), F32)],
    )
    o_s = pl.pallas_call(
        functools.partial(_nsa_sel_body, hpg=hpg),
        grid_spec=grid_spec,
        out_shape=jax.ShapeDtypeStruct((n, hd), BF16),
        compiler_params=_params("parallel", "parallel", "arbitrary"),
        name="nsa_selected",
    )(qb_tab, kb_tab, q, kvb, kvb, sel, expand)
    return o_c, o_s, o_w


def _nsa_shared_kv(xb, batch, w_kv, cmp_pe, cmp_w1, cmp_b1, cmp_w2, cmp_b2):
    n, d = xb.shape
    g, dk = NSA_KV_GROUPS, NSA_HEAD_DIM
    t = n // batch
    kv = _matmul(xb, w_kv.astype(BF16), tn=768)
    n_c = t // CMP_STRIDE - 1
    ncp = -(-n_c // LANES) * LANES
    rows = batch * n_c * g
    rows_p = -(-rows // 256) * 256
    outs = []
    for i in range(2):
        z = kv[:, i * g * dk:(i + 1) * g * dk].reshape(batch, t // CMP_STRIDE, CMP_STRIDE, g, dk)
        blk = jnp.concatenate([z[:, :-1], z[:, 1:]], axis=2)
        flat = blk.transpose(0, 1, 3, 2, 4).reshape(rows, CMP_LEN * dk)
        flat = jnp.pad(flat, ((0, rows_p - rows), (0, 0)))
        pe = jnp.broadcast_to(cmp_pe[i][:, None, :], (CMP_LEN, 1, dk)).reshape(CMP_LEN * dk)
        c = _cmp_mlp(flat, pe, cmp_w1[i], cmp_b1[i], cmp_w2[i], cmp_b2[i])[:rows]
        c = c.reshape(batch, n_c, g, dk).transpose(0, 2, 1, 3)
        outs.append(jnp.pad(c, ((0, 0), (0, 0), (0, ncp - n_c), (0, 0))))
    return kv.astype(BF16), outs[0], outs[1], n_c


def _gate_combine_body(oc_ref, os_ref, ow_ref, gl_ref, ex_ref, o_ref):
    gates = jax.nn.sigmoid(gl_ref[...])
    acc = None
    for i, r in enumerate((oc_ref, os_ref, ow_ref)):
        term = _dot_split_const(gates, ex_ref[i]) * r[...].astype(F32)
        acc = term if acc is None else acc + term
    o_ref[...] = acc.astype(BF16)


def _gate_combine(o_c, o_s, o_w, glog, nh, tm=256):
    n, hd = o_c.shape
    dk = hd // nh
    ex = np.zeros((3, LANES, hd), np.float32)
    for i in range(3):
        for h in range(nh):
            ex[i, i * nh + h, h * dk:(h + 1) * dk] = 1.0
    row = pl.BlockSpec((tm, hd), lambda i: (i, 0))
    return pl.pallas_call(
        _gate_combine_body,
        grid=(n // tm,),
        in_specs=[row, row, row, pl.BlockSpec((tm, LANES), lambda i: (i, 0)),
                  pl.BlockSpec((3, LANES, hd), lambda i: (0, 0, 0))],
        out_specs=row,
        out_shape=jax.ShapeDtypeStruct((n, hd), BF16),
        compiler_params=_params("parallel"),
        name="nsa_gate_combine",
    )(o_c, o_s, o_w, glog, jnp.asarray(ex, BF16))


def _nsa_layer(xb, batch, shared, w_in, b_gate, w_o):
    kvb, k_cmp, v_cmp, n_c = shared
    n, d = xb.shape
    hd = w_o.shape[0]
    nh = hd // NSA_HEAD_DIM
    q = _matmul(xb, w_in[:, :hd].astype(BF16), out_dtype=BF16,
                out_scale=NSA_HEAD_DIM ** -0.5 * math.log2(math.e))
    wg = jnp.pad(w_in[:, hd:], ((0, 0), (0, LANES - 3 * nh))).astype(BF16)
    bg = jnp.pad(b_gate, (0, LANES - 3 * nh))
    glog = _matmul(xb, wg, bias=bg)
    o_c, o_s, o_w = _nsa_attention(q, kvb, k_cmp, v_cmp, batch, n_c)
    o = _gate_combine(o_c, o_s, o_w, glog, nh)
    return o, w_o.astype(BF16)


def kernel(x, ln_g, ln_b, rw_mu, rw_w_rkv, rw_w0, rw_w1, rw_w2, rw_a0, rw_a1, rw_a2, rw_g1, rw_g2, rw_k_k, rw_k_a, rw_r_k, rw_lnx_g, rw_lnx_b, rw_w_o, nsa_w_kv, nsa_cmp_pe, nsa_cmp_w1, nsa_cmp_b1, nsa_cmp_w2, nsa_cmp_b2, nsa_w_in, nsa_b_gate, nsa_w_o, moe_router_w, moe_router_b, moe_w_gu, moe_b_gu, moe_w_down, moe_b_down):
    batch, t, d = x.shape
    depth = ln_g.shape[0]
    n_a = rw_mu.shape[0]
    alpha = (2 * depth) ** 0.25
    h = x.reshape(batch * t, d)
    hb = None
    shared = None
    for layer in range(depth):
        if layer < n_a:
            i = layer
            mixed, w_o = _rwkv_time_mix(h, batch, rw_mu[i], rw_w_rkv[i], rw_w0[i], rw_w1[i], rw_w2[i], rw_a0[i],
                                        rw_a1[i], rw_a2[i], rw_g1[i], rw_g2[i], rw_k_k[i], rw_k_a[i], rw_r_k[i],
                                        rw_lnx_g[i], rw_lnx_b[i], rw_w_o[i])
        else:
            if shared is None:
                if hb is None:
                    hb = h.astype(BF16)
                shared = _nsa_shared_kv(hb, batch, nsa_w_kv, nsa_cmp_pe, nsa_cmp_w1, nsa_cmp_b1, nsa_cmp_w2,
                                        nsa_cmp_b2)
            j = layer - n_a
            mixed, w_o = _nsa_layer(hb, batch, shared, nsa_w_in[j], nsa_b_gate[j], nsa_w_o[j])
        h, hb = _proj_ln(mixed, w_o, h, ln_g[layer, 0], ln_b[layer, 0], alpha)
        y4, gates = _moe_ffn(hb, layer, moe_router_w[layer], moe_router_b[layer], moe_w_gu, moe_b_gu, moe_w_down,
                             moe_b_down)
        h, hb = _combine_ln(h, y4, gates, ln_g[layer, 1], ln_b[layer, 1], alpha)
    return h.reshape(batch, t, d)
```

```python
import functools
import math

import numpy as np
import jax
import jax.numpy as jnp
from jax import lax
from jax.experimental import pallas as pl
from jax.experimental.pallas import tpu as pltpu

F32 = jnp.float32
BF16 = jnp.bfloat16

V7X_VMEM_LIMIT_BYTES = 56 * 1024 * 1024
LANES = 128

LN_EPS = 1e-5
RW_HEAD_DIM = 64
RW_GN_EPS = 64e-5
RW_CHUNK = 64
RW_CHUNKS_PER_STEP = 8
NSA_HEAD_DIM = 128
NSA_KV_GROUPS = 2
CMP_STRIDE = 16
CMP_LEN = 32
SEL_BLOCK = 64
N_SEL = 16
WINDOW = 512
Q_BLOCK = 128
SEL_KEY_TILE = 1024
SEL_Q_TILE = 256
SEL_HEADS_PER_DOT = 2
CMP_HEADS_PER_DOT = 4
CMP_SEG_COLS = 256
N_EXPERTS = 32
TOP_K = 4
SWIGLU_LIMIT = 7.0
SWIGLU_ALPHA = 1.702
MOE_ROW_BLOCK = 1024
MOE_F_TILE = 512
MOE_OUT_TILE = 1024
NEG_BIG = -1e30


def _params(*sem):
    return pltpu.CompilerParams(dimension_semantics=sem, vmem_limit_bytes=V7X_VMEM_LIMIT_BYTES)


def _bdot(a, b):
    return jnp.dot(a.astype(BF16), b.astype(BF16), preferred_element_type=F32)


def _bdot_nt(a, b):
    return lax.dot_general(a.astype(BF16), b.astype(BF16), (((1,), (1,)), ((), ())),
                           preferred_element_type=F32)


def _dot_const_split(c, x):
    hi = x.astype(BF16)
    lo = (x - hi.astype(F32)).astype(BF16)
    return (jnp.dot(c, hi, preferred_element_type=F32) + jnp.dot(c, lo, preferred_element_type=F32))


def _dot_split_const(x, c):
    hi = x.astype(BF16)
    lo = (x - hi.astype(F32)).astype(BF16)
    return (jnp.dot(hi, c, preferred_element_type=F32) + jnp.dot(lo, c, preferred_element_type=F32))


def _mm_body(a_ref, w_ref, b_ref, o_ref, *, out_scale):
    acc = jnp.dot(a_ref[...], w_ref[...], preferred_element_type=F32) + b_ref[...]
    if out_scale is not None:
        acc = acc * out_scale
    o_ref[...] = acc.astype(o_ref.dtype)


def _matmul(a, w, bias=None, out_dtype=F32, out_scale=None, tm=512, tn=1024):
    m, k = a.shape
    n = w.shape[1]
    tm = min(tm, m)
    tn = min(tn, n)
    assert m % tm == 0 and n % tn == 0, (m, n, tm, tn)
    if bias is None:
        bias = jnp.zeros((1, n), F32)
    return pl.pallas_call(
        functools.partial(_mm_body, out_scale=out_scale),
        grid=(n // tn, m // tm),
        in_specs=[pl.BlockSpec((tm, k), lambda j, i: (i, 0)),
                  pl.BlockSpec((k, tn), lambda j, i: (0, j)),
                  pl.BlockSpec((1, tn), lambda j, i: (0, j))],
        out_specs=pl.BlockSpec((tm, tn), lambda j, i: (i, j)),
        out_shape=jax.ShapeDtypeStruct((m, n), out_dtype),
        compiler_params=_params("parallel", "parallel"),
        name="dense_matmul",
    )(a, w, bias.reshape(1, n).astype(F32))


def _layer_norm_rows(z, g, b):
    mu = jnp.mean(z, -1, keepdims=True)
    zc = z - mu
    var = jnp.mean(zc * zc, -1, keepdims=True)
    return zc * lax.rsqrt(var + LN_EPS) * g + b


def _proj_ln_body(a_ref, w_ref, x_ref, g_ref, b_ref, o_ref, ob_ref, *, alpha):
    z = alpha * x_ref[...] + jnp.dot(a_ref[...], w_ref[...], preferred_element_type=F32)
    y = _layer_norm_rows(z, g_ref[...], b_ref[...])
    o_ref[...] = y
    ob_ref[...] = y.astype(BF16)


def _proj_ln(a, w, x, g, b, alpha, tm=512):
    n, d = x.shape
    k = w.shape[0]
    tm = min(tm, n)
    row = pl.BlockSpec((tm, d), lambda i: (i, 0))
    vec = pl.BlockSpec((1, d), lambda i: (0, 0))
    return pl.pallas_call(
        functools.partial(_proj_ln_body, alpha=alpha),
        grid=(n // tm,),
        in_specs=[pl.BlockSpec((tm, k), lambda i: (i, 0)), pl.BlockSpec((k, d), lambda i: (0, 0)), row, vec, vec],
        out_specs=[row, row],
        out_shape=[jax.ShapeDtypeStruct((n, d), F32), jax.ShapeDtypeStruct((n, d), BF16)],
        compiler_params=_params("parallel"),
        name="proj_residual_layer_norm",
    )(a, w, x, g.reshape(1, d), b.reshape(1, d))


def _rw_mix_lowrank_body(x_ref, last_ref, mu_ref, w1_ref, w2_ref, w0_ref, a1_ref, a2_ref, a0_ref, g1_ref, g2_ref,
                         xr_ref, xk_ref, xv_ref, lw_ref, a_ref, g_ref):
    x = x_ref[...]
    prev = pltpu.roll(x, shift=1, axis=0)
    row = lax.broadcasted_iota(jnp.int32, x.shape, 0)
    prev = jnp.where(row == 0, last_ref[0], prev)
    xx = prev - x
    mixed = lambda i: (x + xx * mu_ref[i:i + 1, :]).astype(BF16)
    xr_ref[...] = mixed(0)
    xk_ref[...] = mixed(2)
    xv_ref[...] = mixed(3)
    z = w0_ref[...] + _bdot(jnp.tanh(jnp.dot(mixed(1), w1_ref[...], preferred_element_type=F32)), w2_ref[...])
    w_log = jnp.minimum(z, 0.0) - jnp.log(1.0 + jnp.exp(-jnp.abs(z))) - 0.5
    lw_ref[...] = -jnp.exp(w_log)
    za = a0_ref[...] + _bdot(jnp.dot(mixed(4), a1_ref[...], preferred_element_type=F32), a2_ref[...])
    a_ref[...] = jax.nn.sigmoid(za)
    hg = jax.nn.sigmoid(jnp.dot(mixed(5), g1_ref[...], preferred_element_type=F32))
    g_ref[...] = _bdot(hg, g2_ref[...])


def _pad_rank(w_in, w_out):
    r = w_in.shape[1]
    rp = -(-r // LANES) * LANES
    return (jnp.pad(w_in, ((0, 0), (0, rp - r))).astype(BF16), jnp.pad(w_out, ((0, rp - r), (0, 0))).astype(BF16))


def _rw_mix_lowrank(x2, batch, mu, w0, w1, w2, a0, a1, a2, g1, g2, tm=256):
    n, d = x2.shape
    t = n // batch
    nt = t // tm
    last = x2.reshape(batch, nt, tm, d)[:, :, tm - 1, :]
    last = jnp.concatenate([jnp.zeros((batch, 1, d), F32), last[:, :-1]], axis=1).reshape(batch * nt, 1, d)
    w1p, w2p = _pad_rank(w1, w2)
    a1p, a2p = _pad_rank(a1, a2)
    g1p, g2p = _pad_rank(g1, g2)
    row = pl.BlockSpec((tm, d), lambda i: (i, 0))
    full = lambda arr: pl.BlockSpec(arr.shape, lambda i: (0, 0))
    w0r, a0r = w0.reshape(1, d), a0.reshape(1, d)
    return pl.pallas_call(
        _rw_mix_lowrank_body,
        grid=(n // tm,),
        in_specs=[row, pl.BlockSpec((1, 1, d), lambda i: (i, 0, 0)), full(mu), full(w1p), full(w2p), full(w0r),
                  full(a1p), full(a2p), full(a0r), full(g1p), full(g2p)],
        out_specs=[row] * 6,
        out_shape=[jax.ShapeDtypeStruct((n, d), BF16)] * 3 + [jax.ShapeDtypeStruct((n, d), F32)] * 3,
        compiler_params=_params("parallel"),
        name="rwkv_shift_lowrank",
    )(x2, last, mu, w1p, w2p, w0r, a1p, a2p, a0r, g1p, g2p)


def _wkv_body(r_ref, k_ref, v_ref, lw_ref, a_ref, g_ref, kk_ref, ka_ref, rk_ref, lng_ref, lnb_ref,
              o_ref, s_ref, rp_ref, yq_ref, bonus_ref, gs_ref, *, nchunk):
    L = RW_CHUNK
    H2 = 2 * L

    step = pl.program_id(2)
    slot_w = step % 2
    slot_r = 1 - slot_w

    @pl.when(step == 0)
    def _():
        s_ref[...] = jnp.zeros_like(s_ref)
        rp_ref[1] = jnp.zeros(rp_ref.shape[1:], F32)
        yq_ref[1] = jnp.zeros(yq_ref.shape[1:], F32)
        bonus_ref[1] = jnp.zeros(bonus_ref.shape[1:], F32)
        gs_ref[1] = jnp.zeros(gs_ref.shape[1:], F32)

    lane = lax.broadcasted_iota(jnp.int32, (1, LANES), 1)
    mask0 = (lane < RW_HEAD_DIM).astype(F32)
    mask1 = 1.0 - mask0
    ri = lax.broadcasted_iota(jnp.int32, (H2, H2), 0)
    ci = lax.broadcasted_iota(jnp.int32, (H2, H2), 1)
    same_head = (ri // L) == (ci // L)
    strict = (same_head & (ci < ri)).astype(F32)
    incl = (same_head & (ci <= ri)).astype(F32)
    diag16 = ((ri // 16) == (ci // 16)).astype(F32)
    eye = (ri == ci).astype(F32)
    head_ones = same_head.astype(BF16)
    tl = lax.broadcasted_iota(jnp.int32, (L, L), 0)
    sl = lax.broadcasted_iota(jnp.int32, (L, L), 1)
    tri_incl = (sl <= tl).astype(BF16)

    def stack(x):
        return jnp.concatenate([x * mask0, x * mask1], axis=0)

    k_k = kk_ref[...]
    k_a = ka_ref[...]
    r_k = rk_ref[...]

    chunks = range(nchunk)

    rp_prev = [rp_ref[slot_r, c] for c in chunks]
    yq_prev = [yq_ref[slot_r, c] for c in chunks]
    chain = {"s": s_ref[...], "ys": []}

    def chain_step():
        c = len(chain["ys"])
        if c >= nchunk:
            return
        res = _bdot(rp_prev[c], chain["s"]) + yq_prev[c]
        chain["ys"].append(res[:L] + res[L:H2])
        chain["s"] = res[H2:]
        if c == nchunk - 1:
            s_ref[...] = chain["s"]
            y = jnp.concatenate(chain["ys"], axis=0)
            inv_n = 1.0 / RW_HEAD_DIM
            ym = _dot_split_const(y, head_ones) * inv_n
            yc = y - ym
            yv = _dot_split_const(yc * yc, head_ones) * inv_n
            yn = yc * lax.rsqrt(yv + RW_GN_EPS) * lng_ref[...] + lnb_ref[...]
            o_ref[...] = ((yn + bonus_ref[slot_r]) * gs_ref[slot_r]).astype(BF16)

    def each(fn, *lists):
        return [fn(*xs) for xs in zip(*lists)]

    def mm_stage(fn, *lists):
        out = each(fn, *lists)
        chain_step()
        return out

    def rows_of(x):
        return [x[c * L:(c + 1) * L] for c in chunks]

    r_all = r_ref[...]
    k_all = k_ref[...]
    v_all = v_ref[...]
    ag_all = a_ref[...]
    kk_all = k_all * k_k
    ss_all = _dot_split_const(kk_all * kk_all, head_ones)
    kk_all = kk_all / jnp.maximum(jnp.sqrt(ss_all), 1e-12)
    kmod_all = k_all * (1.0 + (ag_all - 1.0) * k_a)
    bv_all = kk_all * ag_all
    lw_c = rows_of(lw_ref[...])
    cl_c = each(lambda lw: _dot_const_split(tri_incl, lw), lw_c)
    last_c = each(lambda cl: cl[L - 1:L, :], cl_c)
    cl_all = jnp.concatenate(cl_c, axis=0)
    clp_all = cl_all - lw_ref[...]
    end_all = jnp.concatenate(each(lambda cl, la: la - cl, cl_c, last_c), axis=0)
    e_neg = jnp.exp(-cl_all)
    e_end = jnp.exp(end_all)
    at_c = rows_of(-kk_all * jnp.exp(clp_all))
    rt_c = rows_of(r_all * jnp.exp(cl_all))
    bt_c = rows_of(bv_all * e_neg)
    kt_c = rows_of(kmod_all * e_neg)
    be_c = rows_of(bv_all * e_end)
    ke_c = rows_of(kmod_all * e_end)
    v_s = each(stack, rows_of(v_all))
    at_s = each(stack, at_c)
    rt_s = each(stack, rt_c)
    gmat = mm_stage(lambda a_, r_, b_, k_: _bdot_nt(jnp.concatenate([a_, r_], axis=0),
                                                    jnp.concatenate([stack(b_), stack(k_)], axis=0)),
                    at_s, rt_s, bt_c, kt_c)
    a_ab = each(lambda gm: gm[:H2, :H2] * strict, gmat)
    a_ak = each(lambda gm: gm[:H2, H2:] * strict, gmat)
    a_rb = each(lambda gm: gm[H2:, :H2] * incl, gmat)
    a_rk = each(lambda gm: gm[H2:, H2:] * incl, gmat)
    dblk = each(lambda a_: a_ * diag16, a_ab)
    off = each(lambda a_, d_: a_ - d_, a_ab, dblk)
    d2 = mm_stage(_bdot, dblk, dblk)
    d4 = mm_stage(_bdot, d2, d2)
    d8 = mm_stage(_bdot, d4, d4)
    dinv = mm_stage(lambda d_, d2_: _bdot(eye + d_, eye + d2_), dblk, d2)
    dinv = mm_stage(lambda di, d4_: _bdot(di, eye + d4_), dinv, d4)
    dinv = mm_stage(lambda di, d8_: _bdot(di, eye + d8_), dinv, d8)
    e1 = mm_stage(_bdot, dinv, off)
    e2 = mm_stage(_bdot, e1, e1)
    minv = mm_stage(lambda e1_, e2_: _bdot(eye + e1_, eye + e2_), e1, e2)
    minv = mm_stage(_bdot, minv, dinv)
    x_ak = mm_stage(_bdot, a_ak, v_s)
    zu = each(lambda mi, a_, x_: _bdot(mi, jnp.concatenate([a_, x_], axis=1)), minv, at_s, x_ak)
    w2 = each(_bdot, a_rb, zu)
    rkv = each(_bdot, a_rk, v_s)
    pq = each(lambda b_, z_: _bdot(stack(b_).T, z_), be_c, zu)
    kv2 = each(lambda k_, v_: _bdot(stack(k_).T, v_), ke_c, v_s)
    while len(chain["ys"]) < nchunk:
        chain_step()
    for c in chunks:
        rp_ref[slot_w, c] = jnp.concatenate([rt_s[c] + w2[c][:, :LANES],
                                             eye * jnp.exp(last_c[c]) + pq[c][:, :LANES]], axis=0)
        yq_ref[slot_w, c] = jnp.concatenate([w2[c][:, LANES:] + rkv[c], pq[c][:, LANES:] + kv2[c]], axis=0)
    bonus_ref[slot_w] = _dot_split_const(r_all * kmod_all * r_k, head_ones) * v_all
    gs_ref[slot_w] = g_ref[...]


def _wkv(r, k, v, lw, a, g, k_k, k_a, r_k, lnx_g, lnx_b, batch):
    n, d = r.shape
    t = n // batch
    nchunk = RW_CHUNKS_PER_STEP
    tb = RW_CHUNK * nchunk
    while t % tb:
        nchunk //= 2
        tb = RW_CHUNK * nchunk
    nt = t // tb
    row_in = pl.BlockSpec((tb, LANES), lambda b, hp, c: (b * nt + jnp.minimum(c, nt - 1), hp))
    row_out = pl.BlockSpec((tb, LANES), lambda b, hp, c: (b * nt + jnp.maximum(c - 1, 0), hp))
    vec = pl.BlockSpec((1, LANES), lambda b, hp, c: (0, hp))
    sq = pltpu.VMEM((2, nchunk, 2 * LANES, LANES), F32)
    blk = pltpu.VMEM((2, tb, LANES), F32)
    vecs = [z.reshape(1, d) for z in (k_k, k_a, r_k, lnx_g, lnx_b)]
    return pl.pallas_call(
        functools.partial(_wkv_body, nchunk=nchunk),
        grid=(batch, d // LANES, nt + 1),
        in_specs=[row_in] * 6 + [vec] * 5,
        out_specs=row_out,
        out_shape=jax.ShapeDtypeStruct((n, d), BF16),
        scratch_shapes=[pltpu.VMEM((LANES, LANES), F32), sq, sq, blk, blk],
        compiler_params=_params("parallel", "parallel", "arbitrary"),
        name="rwkv_chunked_scan",
    )(r, k, v, lw, a, g, *vecs)


def _rwkv_time_mix(x2, batch, mu, w_rkv, w0, w1, w2, a0, a1, a2, g1, g2, k_k, k_a, r_k, lnx_g, lnx_b, w_o):
    xr, xk, xv, lw, a, g = _rw_mix_lowrank(x2, batch, mu, w0, w1, w2, a0, a1, a2, g1, g2)
    r = _matmul(xr, w_rkv[0].astype(BF16))
    k = _matmul(xk, w_rkv[1].astype(BF16))
    v = _matmul(xv, w_rkv[2].astype(BF16))
    z = _wkv(r, k, v, lw, a, g, k_k, k_a, r_k.reshape(-1), lnx_g, lnx_b, batch)
    return z, w_o.astype(BF16)


def _router_body(x_ref, w_ref, b_ref, idx_ref, gate_ref):
    logits = jnp.dot(x_ref[...], w_ref[...], preferred_element_type=F32) + b_ref[...]
    lane = lax.broadcasted_iota(jnp.int32, logits.shape, 1).astype(F32)
    cur = logits
    vals, idxs = [], []
    for _ in range(TOP_K):
        m = jnp.max(cur, axis=-1, keepdims=True)
        i = jnp.min(jnp.where(cur == m, lane, float(LANES)), axis=-1, keepdims=True)
        vals.append(m)
        idxs.append(i)
        cur = jnp.where(lane == i, -3e38, cur)
    es = [jnp.exp(vv - vals[0]) for vv in vals]
    den = es[0]
    for e in es[1:]:
        den = den + e
    idx_out = jnp.zeros(logits.shape, F32)
    gate_out = jnp.zeros(logits.shape, F32)
    for kk in range(TOP_K):
        idx_out = jnp.where(lane == kk, idxs[kk], idx_out)
        gate_out = jnp.where(lane == kk, es[kk] / den, gate_out)
    idx_ref[...] = idx_out.astype(jnp.int32)
    gate_ref[...] = gate_out


def _router(xb, router_w, router_b, tm=512):
    n, d = xb.shape
    e = router_w.shape[1]
    wp = jnp.pad(router_w, ((0, 0), (0, LANES - e))).astype(BF16)
    bp = jnp.concatenate([router_b.astype(F32), jnp.full((LANES - e,), NEG_BIG, F32)]).reshape(1, LANES)
    row = pl.BlockSpec((tm, LANES), lambda i: (i, 0))
    idx, gate = pl.pallas_call(
        _router_body,
        grid=(n // tm,),
        in_specs=[pl.BlockSpec((tm, d), lambda i: (i, 0)), pl.BlockSpec((d, LANES), lambda i: (0, 0)),
                  pl.BlockSpec((1, LANES), lambda i: (0, 0))],
        out_specs=[row, row],
        out_shape=[jax.ShapeDtypeStruct((n, LANES), jnp.int32), jax.ShapeDtypeStruct((n, LANES), F32)],
        compiler_params=_params("parallel"),
        name="moe_router",
    )(xb, wp, bp)
    return idx[:, :TOP_K], gate


def _moe_up_body(it_ref, ct_ref, be_ref, nu_ref, x_ref, wg_ref, wu_ref, bg_ref, bu_ref, o_ref):
    used = it_ref[pl.program_id(0)] < nu_ref[0]

    @pl.when(used)
    def _():
        x = x_ref[...]
        gate = jnp.dot(x, wg_ref[...].astype(BF16), preferred_element_type=F32) + bg_ref[...]
        up = jnp.dot(x, wu_ref[...].astype(BF16), preferred_element_type=F32) + bu_ref[...]
        gate = jnp.minimum(gate, SWIGLU_LIMIT)
        up = jnp.clip(up, -SWIGLU_LIMIT, SWIGLU_LIMIT)
        o_ref[...] = ((up + 1.0) * (gate * jax.nn.sigmoid(gate * SWIGLU_ALPHA))).astype(o_ref.dtype)

    @pl.when(jnp.logical_not(used))
    def _():
        o_ref[...] = jnp.zeros_like(o_ref)


def _moe_down_body(it_ref, ct_ref, be_ref, nu_ref, a_ref, wd_ref, bd_ref, o_ref):
    used = it_ref[pl.program_id(0)] < nu_ref[0]

    @pl.when(used)
    def _():
        y = jnp.dot(a_ref[...], wd_ref[...].astype(BF16), preferred_element_type=F32) + bd_ref[...]
        o_ref[...] = y.astype(o_ref.dtype)

    @pl.when(jnp.logical_not(used))
    def _():
        o_ref[...] = jnp.zeros_like(o_ref)


def _expert_major_steps(blk_e, n_blk, n_col):
    col = jnp.repeat(jnp.arange(n_col, dtype=jnp.int32), n_blk)
    blk = jnp.tile(jnp.arange(n_blk, dtype=jnp.int32), n_col)
    experts = jnp.arange(N_EXPERTS, dtype=jnp.int32)
    first = jnp.sum((blk_e[None, :] < experts[:, None]).astype(jnp.int32), axis=1)
    count = jnp.sum((blk_e[None, :] == experts[:, None]).astype(jnp.int32), axis=1)
    e = blk_e[blk]
    pos = n_col * first[e] + col * count[e] + (blk - first[e])
    zeros = jnp.zeros((n_col * n_blk,), jnp.int32)
    return zeros.at[pos].set(blk), zeros.at[pos].set(col)


def _moe_experts(xs, blk_e, n_used, layer, w_gu, b_gu, w_down, b_down):
    n_rows, d = xs.shape
    depth, ne, _, f2 = w_gu.shape
    fdim = f2 // 2
    tm, tf, tn = MOE_ROW_BLOCK, min(MOE_F_TILE, fdim), min(MOE_OUT_TILE, d)
    nf, nn = fdim // tf, d // tn
    n_blk = n_rows // tm
    bgu = b_gu.reshape(depth, ne, 1, f2)

    it, ct = _expert_major_steps(blk_e, n_blk, nf)
    act = pl.pallas_call(
        _moe_up_body,
        grid_spec=pltpu.PrefetchScalarGridSpec(
            num_scalar_prefetch=4,
            grid=(nf * n_blk,),
            in_specs=[
                pl.BlockSpec((tm, d), lambda t, it, ct, be, nu: (jnp.minimum(it[t], nu[0] - 1), 0)),
                pl.BlockSpec((None, None, d, tf), lambda t, it, ct, be, nu: (layer, be[it[t]], 0, ct[t])),
                pl.BlockSpec((None, None, d, tf), lambda t, it, ct, be, nu: (layer, be[it[t]], 0, nf + ct[t])),
                pl.BlockSpec((None, None, 1, tf), lambda t, it, ct, be, nu: (layer, be[it[t]], 0, ct[t])),
                pl.BlockSpec((None, None, 1, tf), lambda t, it, ct, be, nu: (layer, be[it[t]], 0, nf + ct[t])),
            ],
            out_specs=pl.BlockSpec((tm, tf), lambda t, it, ct, be, nu: (it[t], ct[t])),
        ),
        out_shape=jax.ShapeDtypeStruct((n_rows, fdim), BF16),
        compiler_params=_params("arbitrary"),
        name="moe_experts_up",
    )(it, ct, blk_e, n_used, xs, w_gu, w_gu, bgu, bgu)

    it2, ct2 = _expert_major_steps(blk_e, n_blk, nn)
    return pl.pallas_call(
        _moe_down_body,
        grid_spec=pltpu.PrefetchScalarGridSpec(
            num_scalar_prefetch=4,
            grid=(nn * n_blk,),
            in_specs=[
                pl.BlockSpec((tm, fdim), lambda t, it, ct, be, nu: (jnp.minimum(it[t], nu[0] - 1), 0)),
                pl.BlockSpec((None, None, fdim, tn), lambda t, it, ct, be, nu: (layer, be[it[t]], 0, ct[t])),
                pl.BlockSpec((None, None, 1, tn), lambda t, it, ct, be, nu: (layer, be[it[t]], 0, ct[t])),
            ],
            out_specs=pl.BlockSpec((tm, tn), lambda t, it, ct, be, nu: (it[t], ct[t])),
        ),
        out_shape=jax.ShapeDtypeStruct((n_rows, d), BF16),
        compiler_params=_params("arbitrary"),
        name="moe_experts_down",
    )(it2, ct2, blk_e, n_used, act, w_down, b_down.reshape(depth, ne, 1, d))


def _combine_ln_body(x_ref, y_ref, gate_ref, g_ref, b_ref, o_ref, ob_ref, *, alpha):
    ffn = y_ref[0].astype(F32) * gate_ref[:, 0:1]
    for kk in range(1, TOP_K):
        ffn = ffn + y_ref[kk].astype(F32) * gate_ref[:, kk:kk + 1]
    z = alpha * x_ref[...] + ffn
    mu = jnp.mean(z, -1, keepdims=True)
    zc = z - mu
    var = jnp.mean(zc * zc, -1, keepdims=True)
    y = zc * lax.rsqrt(var + LN_EPS) * g_ref[...] + b_ref[...]
    o_ref[...] = y
    ob_ref[...] = y.astype(BF16)


def _combine_ln(x, y4, gates, g, b, alpha, tm=256):
    n, d = x.shape
    row = pl.BlockSpec((tm, d), lambda i: (i, 0))
    vec = pl.BlockSpec((1, d), lambda i: (0, 0))
    return pl.pallas_call(
        functools.partial(_combine_ln_body, alpha=alpha),
        grid=(n // tm,),
        in_specs=[row, pl.BlockSpec((TOP_K, tm, d), lambda i: (0, i, 0)),
                  pl.BlockSpec((tm, LANES), lambda i: (i, 0)), vec, vec],
        out_specs=[row, row],
        out_shape=[jax.ShapeDtypeStruct((n, d), F32), jax.ShapeDtypeStruct((n, d), BF16)],
        compiler_params=_params("parallel"),
        name="moe_combine_layer_norm",
    )(x, y4, gates, g.reshape(1, d), b.reshape(1, d))


def _moe_ffn(xb, layer, router_w, router_b, w_gu, b_gu, w_down, b_down):
    n, d = xb.shape
    tm = MOE_ROW_BLOCK
    top_i, gates = _router(xb, router_w, router_b)
    flat_e = top_i.reshape(-1)
    onehot = (flat_e[:, None] == jnp.arange(N_EXPERTS, dtype=jnp.int32)[None, :]).astype(jnp.int32)
    csum = jnp.cumsum(onehot, axis=0)
    rank = jnp.take_along_axis(csum, flat_e[:, None], axis=1)[:, 0] - 1
    counts = csum[-1]
    padded = ((counts + tm - 1) // tm) * tm
    pad_end = jnp.cumsum(padded)
    pad_start = pad_end - padded
    dest = pad_start[flat_e] + rank
    n_rows = -(-(n * TOP_K) // tm) * tm + N_EXPERTS * tm
    n_blk = n_rows // tm
    blk_start = jnp.arange(n_blk, dtype=jnp.int32) * tm
    blk_e = jnp.minimum(jnp.sum((pad_end[None, :] <= blk_start[:, None]).astype(jnp.int32), axis=1),
                        N_EXPERTS - 1).astype(jnp.int32)
    n_used = (pad_end[-1] // tm).astype(jnp.int32).reshape(1)
    flat_tok = jnp.arange(n * TOP_K, dtype=jnp.int32) // TOP_K
    row_tok = (jnp.arange(n_rows, dtype=jnp.int32) % n).at[dest].set(flat_tok)
    xs = jnp.take(xb, row_tok, axis=0, mode='clip')
    ys = _moe_experts(xs, blk_e, n_used, layer, w_gu, b_gu, w_down, b_down)
    dest_kmajor = dest.reshape(n, TOP_K).T.reshape(-1)
    y4 = jnp.take(ys, dest_kmajor, axis=0, mode='clip').reshape(TOP_K, n, d)
    return y4, gates


def _cmp_mlp_body(x_ref, pe_ref, w1_ref, b1_ref, w2_ref, b2_ref, o_ref):
    h = _bdot(x_ref[...] + pe_ref[...], w1_ref[...]) + b1_ref[...]
    h = jax.nn.gelu(h)
    o_ref[...] = (_bdot(h, w2_ref[...]) + b2_ref[...]).astype(o_ref.dtype)


def _cmp_mlp(flat, pe, w1, b1, w2, b2, tm=256):
    m, kd = flat.shape
    hid = w1.shape[1]
    dk = w2.shape[1]
    full = lambda shp: pl.BlockSpec(shp, lambda i: (0, 0))
    return pl.pallas_call(
        _cmp_mlp_body,
        grid=(m // tm,),
        in_specs=[pl.BlockSpec((tm, kd), lambda i: (i, 0)), full((1, kd)), full((kd, hid)), full((1, hid)),
                  full((hid, dk)), full((1, dk))],
        out_specs=pl.BlockSpec((tm, dk), lambda i: (i, 0)),
        out_shape=jax.ShapeDtypeStruct((m, dk), BF16),
        compiler_params=_params("parallel"),
        name="nsa_compress_mlp",
    )(flat, pe.reshape(1, kd), w1.astype(BF16), b1.reshape(1, hid), w2.astype(BF16), b2.reshape(1, dk))


def _group_rows(q_ref, hpg):
    dk = NSA_HEAD_DIM
    return jnp.concatenate([q_ref[:, h * dk:(h + 1) * dk] for h in range(hpg)], axis=0)


def _softmax_rows(s, mask):
    s = jnp.where(mask, s, NEG_BIG)
    m = jnp.max(s, axis=-1, keepdims=True)
    e = jnp.where(mask, jnp.exp2(s - m), 0.0)
    den = jnp.sum(e, axis=-1, keepdims=True)
    return e / jnp.where(den > 0, den, 1.0)


def _exp2_rows(s_rows, bias, cols):
    s_h = [s_rows[:, cj] + bias[:, cj] for cj in cols]
    mx = s_h[0]
    for s_hj in s_h[1:]:
        mx = jnp.maximum(mx, s_hj)
    m = jnp.broadcast_to(jnp.max(mx, axis=-1, keepdims=True), mx.shape)
    return jnp.concatenate([jnp.exp2(s_hj - m).astype(BF16) for s_hj in s_h], axis=1)


def _nsa_cmp_body(q_ref, kc_ref, vc_ref, ov_ref, *rest, hpg, n_c, n_sel, qb0):
    o_ref, sel_ref = rest[-2:]
    qb = pl.program_id(2) + qb0
    t0 = qb * Q_BLOCK
    ncols = kc_ref.shape[0]
    n_s = ov_ref.shape[1]
    dk = NSA_HEAD_DIM
    tq_c = t0 + lax.broadcasted_iota(jnp.int32, (Q_BLOCK, ncols), 0)
    cid = lax.broadcasted_iota(jnp.int32, (Q_BLOCK, ncols), 1)
    bias = jnp.where((cid * CMP_STRIDE + (CMP_LEN - 1) <= tq_c) & (cid < n_c), 0.0, NEG_BIG)
    t_row = t0 + lax.broadcasted_iota(jnp.int32, (Q_BLOCK, LANES), 0)
    row_live = t_row >= CMP_LEN - 1
    cols = [slice(j * LANES, (j + 1) * LANES) for j in range(ncols // LANES)]
    k_tile = kc_ref[...]
    v_aug = jnp.concatenate([vc_ref[...], jnp.ones((ncols, LANES), BF16)], axis=1)
    ov = ov_ref[...]
    hg = min(CMP_HEADS_PER_DOT, hpg)
    imp = jnp.zeros((Q_BLOCK, n_s), F32)
    for g0 in range(0, hpg, hg):
        qg = jnp.concatenate([q_ref[:, h * dk:(h + 1) * dk] for h in range(g0, g0 + hg)], axis=0)
        s_g = _bdot_nt(qg, k_tile)
        e = jnp.concatenate([_exp2_rows(s_g[hl * Q_BLOCK:(hl + 1) * Q_BLOCK], bias, cols) for hl in range(hg)], axis=0)
        od = jnp.dot(e, v_aug, preferred_element_type=F32)
        ih = jnp.dot(e, ov, preferred_element_type=F32)
        for hl in range(hg):
            rows = slice(hl * Q_BLOCK, (hl + 1) * Q_BLOCK)
            inv = jnp.where(row_live, 1.0 / od[rows, dk:], 0.0)
            o_ref[:, (g0 + hl) * dk:(g0 + hl + 1) * dk] = (od[rows, :dk] * inv).astype(o_ref.dtype)
            inv_s = inv[:, :n_s] if n_s <= LANES else jnp.concatenate([inv] * (n_s // LANES), axis=1)
            imp = imp + ih[rows] * inv_s
    imp_t = imp.T
    tq = t0 + lax.broadcasted_iota(jnp.int32, (n_s, Q_BLOCK), 1)
    sid_i = lax.broadcasted_iota(jnp.int32, (n_s, Q_BLOCK), 0)
    cur = tq // SEL_BLOCK
    forced = (sid_i == 0) | (sid_i == cur) | (sid_i == cur - 1)
    score = jnp.where(sid_i * SEL_BLOCK <= tq, jnp.where(forced, 1e30, imp_t), -1.0)
    sid = sid_i.astype(F32)
    sel = jnp.zeros((n_s, Q_BLOCK), F32)
    for _ in range(n_sel):
        m = jnp.max(score, axis=0, keepdims=True)
        first = jnp.min(jnp.where(score == m, sid, float(n_s)), axis=0, keepdims=True)
        hit = sid == first
        sel = jnp.where(hit, 1.0, sel)
        score = jnp.where(hit, -2.0, score)
    sel_ref[...] = sel.T.astype(BF16)


def _nsa_win_body(q_ref, *refs, hpg, nwb):
    k_refs, v_refs, o_ref = refs[:nwb], refs[nwb:2 * nwb], refs[2 * nwb]
    qb = pl.program_id(2)
    t0 = qb * Q_BLOCK
    dk = NSA_HEAD_DIM
    nk = nwb * Q_BLOCK
    kcat = jnp.concatenate([r[...] for r in k_refs], axis=0)
    v_aug = jnp.concatenate([r[...] for r in v_refs], axis=0)
    v_aug = jnp.concatenate([v_aug, jnp.ones((nk, LANES), BF16)], axis=1)
    t = t0 + lax.broadcasted_iota(jnp.int32, (Q_BLOCK, nk), 0)
    kpos = t0 - WINDOW + lax.broadcasted_iota(jnp.int32, (Q_BLOCK, nk), 1)
    bias = jnp.where((kpos <= t) & (kpos > t - WINDOW) & (kpos >= 0), 0.0, NEG_BIG)
    cols = [slice(j * LANES, (j + 1) * LANES) for j in range(nk // LANES)]
    hg = min(CMP_HEADS_PER_DOT, hpg)
    for g0 in range(0, hpg, hg):
        qg = jnp.concatenate([q_ref[:, h * dk:(h + 1) * dk] for h in range(g0, g0 + hg)], axis=0)
        s_g = _bdot_nt(qg, kcat)
        e = jnp.concatenate([_exp2_rows(s_g[hl * Q_BLOCK:(hl + 1) * Q_BLOCK], bias, cols) for hl in range(hg)], axis=0)
        od = jnp.dot(e, v_aug, preferred_element_type=F32)
        for hl in range(hg):
            rows = slice(hl * Q_BLOCK, (hl + 1) * Q_BLOCK)
            o_ref[:, (g0 + hl) * dk:(g0 + hl + 1) * dk] = (od[rows, :dk] / od[rows, dk:]).astype(o_ref.dtype)


def _nsa_sel_body(qb_ref, kb_ref, q_ref, k_ref, v_ref, sel_ref, pick_ref, mk_ref, o_ref, m_ref, acc_ref, *, hpg):
    step = pl.program_id(2)
    qb = qb_ref[step]
    kb = kb_ref[step]
    qn = q_ref.shape[0]
    t0 = qb * qn
    tk = k_ref.shape[0]
    dk = NSA_HEAD_DIM

    @pl.when(kb == 0)
    def _():
        m_ref[...] = jnp.full_like(m_ref, NEG_BIG)
        acc_ref[...] = jnp.zeros_like(acc_ref)

    unpicked = 1.0 - jnp.dot(sel_ref[...], pick_ref[...], preferred_element_type=F32)
    q_mask = unpicked.astype(BF16)
    cols = [slice(j * LANES, (j + 1) * LANES) for j in range(tk // LANES)]
    k_ext = jnp.concatenate([k_ref[...], mk_ref[...]], axis=1)
    v_aug = jnp.concatenate([v_ref[...], jnp.ones((tk, LANES), BF16)], axis=1)
    hg = SEL_HEADS_PER_DOT
    last = kb == (t0 + qn - 1) // tk

    def scores(g0):
        qg = jnp.concatenate([jnp.concatenate([q_ref[:, h * dk:(h + 1) * dk], q_mask], axis=1)
                              for h in range(g0, g0 + hg)], axis=0)
        return _bdot_nt(qg, k_ext)

    def process(causal_bias):
        s_next = scores(0)
        for g0 in range(0, hpg, hg):
            s_g = s_next
            if g0 + hg < hpg:
                s_next = scores(g0 + hg)
            p_rows, alphas = [], []
            for hl in range(hg):
                rows = slice((g0 + hl) * qn, (g0 + hl + 1) * qn)
                s_h = [s_g[hl * qn:(hl + 1) * qn, cj] for cj in cols]
                if causal_bias is not None:
                    s_h = [s_hj + causal_bias[:, cj] for s_hj, cj in zip(s_h, cols)]
                mx = s_h[0]
                for s_hj in s_h[1:]:
                    mx = jnp.maximum(mx, s_hj)
                m_old = m_ref[rows, :]
                m_new = jnp.maximum(m_old, jnp.broadcast_to(jnp.max(mx, axis=-1, keepdims=True), m_old.shape))
                m_ref[rows, :] = m_new
                alphas.append(jnp.exp2(m_old - m_new))
                p_rows.append(jnp.concatenate([jnp.exp2(s_hj - m_new).astype(BF16) for s_hj in s_h], axis=1))
            pv = jnp.dot(jnp.concatenate(p_rows, axis=0), v_aug, preferred_element_type=F32)
            alpha = jnp.concatenate(alphas, axis=0)
            grows = slice(g0 * qn, (g0 + hg) * qn)
            acc_ref[grows, :dk] = alpha * acc_ref[grows, :dk] + pv[:, :dk]
            acc_ref[grows, dk:] = alpha * acc_ref[grows, dk:] + pv[:, dk:]

    @pl.when(jnp.logical_not(last))
    def _():
        process(None)

    @pl.when(last)
    def _():
        tq = t0 + lax.broadcasted_iota(jnp.int32, (qn, tk), 0)
        kpos = kb * tk + lax.broadcasted_iota(jnp.int32, (qn, tk), 1)
        process(jnp.where(kpos <= tq, 0.0, NEG_BIG))
        den = acc_ref[:, dk:]
        o = acc_ref[:, :dk] / jnp.where(den > 0, den, 1.0)
        for h in range(hpg):
            o_ref[:, h * dk:(h + 1) * dk] = o[h * qn:(h + 1) * qn].astype(o_ref.dtype)


def _nsa_attention(q, kvb, k_cmp, v_cmp, batch, n_c):
    n, hd = q.shape
    dk, g = NSA_HEAD_DIM, NSA_KV_GROUPS
    hpg = hd // dk // g
    t = n // batch
    nqb = t // Q_BLOCK
    n_s = t // SEL_BLOCK
    n_sel = min(N_SEL, n_s)
    ncp = k_cmp.shape[2]
    gw = hpg * dk

    c_lo = np.arange(ncp) * CMP_STRIDE
    s_lo = np.arange(n_s) * SEL_BLOCK
    overlap = ((c_lo[:, None] < s_lo[None, :] + SEL_BLOCK) & (c_lo[:, None] + CMP_LEN > s_lo[None, :])
               & (np.arange(ncp)[:, None] < n_c))
    overlap = jnp.asarray(overlap, BF16)

    qspec = pl.BlockSpec((Q_BLOCK, gw), lambda b, gi, qb: (b * nqb + qb, gi))
    seg_qb = CMP_SEG_COLS * CMP_STRIDE // Q_BLOCK
    o_c = sel = None
    for qb0 in range(0, nqb, seg_qb):
        nq = min(seg_qb, nqb - qb0)
        ncols = min(ncp, -(-((qb0 + nq) * Q_BLOCK // CMP_STRIDE) // LANES) * LANES)
        oq = pl.BlockSpec((Q_BLOCK, gw), lambda b, gi, qb, qb0=qb0: (b * nqb + qb0 + qb, gi))
        in_specs = [oq,
                    pl.BlockSpec((None, None, ncols, dk), lambda b, gi, qb: (b, gi, 0, 0)),
                    pl.BlockSpec((None, None, ncols, dk), lambda b, gi, qb: (b, gi, 0, 0)),
                    pl.BlockSpec((ncols, n_s), lambda b, gi, qb: (0, 0))]
        args = [q, k_cmp, v_cmp, overlap]
        aliases = {}
        if o_c is not None:
            in_specs += [pl.BlockSpec(memory_space=pl.ANY), pl.BlockSpec(memory_space=pl.ANY)]
            args += [o_c, sel]
            aliases = {4: 0, 5: 1}
        o_c, sel = pl.pallas_call(
            functools.partial(_nsa_cmp_body, hpg=hpg, n_c=n_c, n_sel=n_sel, qb0=qb0),
            grid=(batch, g, nq),
            in_specs=in_specs,
            out_specs=[oq, pl.BlockSpec((None, None, Q_BLOCK, n_s), lambda b, gi, qb, qb0=qb0: (b, gi, qb0 + qb, 0))],
            out_shape=[jax.ShapeDtypeStruct((n, hd), BF16), jax.ShapeDtypeStruct((batch, g, t, n_s), BF16)],
            input_output_aliases=aliases,
            compiler_params=_params("parallel", "parallel", "parallel"),
            name="nsa_compressed_select",
        )(*args)

    nwb = WINDOW // Q_BLOCK + 1
    kcol, vcol = 4 * g, 5 * g

    def kv_spec(col, j):
        return pl.BlockSpec((Q_BLOCK, dk),
                            lambda b, gi, qb: (b * nqb + jnp.maximum(qb - (nwb - 1) + j, 0), col + gi))

    o_w = pl.pallas_call(
        functools.partial(_nsa_win_body, hpg=hpg, nwb=nwb),
        grid=(batch, g, nqb),
        in_specs=[qspec] + [kv_spec(kcol, j) for j in range(nwb)] + [kv_spec(vcol, j) for j in range(nwb)],
        out_specs=qspec,
        out_shape=jax.ShapeDtypeStruct((n, hd), BF16),
        compiler_params=_params("parallel", "parallel", "parallel"),
        name="nsa_window",
    )(q, *([kvb] * (2 * nwb)))

    tk = min(SEL_KEY_TILE, t)
    nkb = t // tk
    qn = min(SEL_Q_TILE, t)
    nqt = t // qn
    steps = [(qb, kb) for qb in range(nqt) for kb in range((qb * qn + qn - 1) // tk + 1)]
    qb_tab = jnp.asarray([s_[0] for s_ in steps], jnp.int32)
    kb_tab = jnp.asarray([s_[1] for s_ in steps], jnp.int32)
    spt = tk // SEL_BLOCK
    assert spt <= LANES
    pick = np.zeros((nkb, n_s, LANES), np.float32)
    for kb_ in range(nkb):
        pick[kb_, kb_ * spt + np.arange(spt), np.arange(spt)] = 1.0
    mk = np.zeros((tk, LANES), np.float32)
    mk[np.arange(tk), np.arange(tk) // SEL_BLOCK] = -(2.0 ** 100)
    pick, mk = jnp.asarray(pick, BF16), jnp.asarray(mk, BF16)
    kscol, vscol = 2 * g, 3 * g
    grid_spec = pltpu.PrefetchScalarGridSpec(
        num_scalar_prefetch=2,
        grid=(batch, g, len(steps)),
        in_specs=[
            pl.BlockSpec((qn, gw), lambda b, gi, s_, qt, kt: (b * nqt + qt[s_], gi)),
            pl.BlockSpec((tk, dk), lambda b, gi, s_, qt, kt: (b * nkb + kt[s_], kscol + gi)),
            pl.BlockSpec((tk, dk), lambda b, gi, s_, qt, kt: (b * nkb + kt[s_], vscol + gi)),
            pl.BlockSpec((None, None, qn, n_s), lambda b, gi, s_, qt, kt: (b, gi, qt[s_], 0)),
            pl.BlockSpec((None, n_s, LANES), lambda b, gi, s_, qt, kt: (kt[s_], 0, 0)),
            pl.BlockSpec((tk, LANES), lambda b, gi, s_, qt, kt: (0, 0)),
        ],
        out_specs=pl.BlockSpec((qn, gw), lambda b, gi, s_, qt, kt: (b * nqt + qt[s_], gi)),
        scratch_shapes=[pltpu.VMEM((hpg * qn, LANES), F32), pltpu.VMEM((hpg * qn, dk + LANES), F32)],
    )
    o_s = pl.pallas_call(
        functools.partial(_nsa_sel_body, hpg=hpg),
        grid_spec=grid_spec,
        out_shape=jax.ShapeDtypeStruct((n, hd), BF16),
        compiler_params=_params("parallel", "parallel", "arbitrary"),
        name="nsa_selected",
    )(qb_tab, kb_tab, q, kvb, kvb, sel, pick, mk)
    return o_c, o_s, o_w


def _nsa_shared_kv(xb, batch, w_kv, cmp_pe, cmp_w1, cmp_b1, cmp_w2, cmp_b2):
    n, d = xb.shape
    g, dk = NSA_KV_GROUPS, NSA_HEAD_DIM
    t = n // batch
    kv = _matmul(xb, w_kv.astype(BF16), tn=768)
    n_c = t // CMP_STRIDE - 1
    ncp = -(-n_c // LANES) * LANES
    rows = batch * n_c * g
    rows_p = -(-rows // 256) * 256
    outs = []
    for i in range(2):
        z = kv[:, i * g * dk:(i + 1) * g * dk].reshape(batch, t // CMP_STRIDE, CMP_STRIDE, g, dk)
        blk = jnp.concatenate([z[:, :-1], z[:, 1:]], axis=2)
        flat = blk.transpose(0, 1, 3, 2, 4).reshape(rows, CMP_LEN * dk)
        flat = jnp.pad(flat, ((0, rows_p - rows), (0, 0)))
        pe = jnp.broadcast_to(cmp_pe[i][:, None, :], (CMP_LEN, 1, dk)).reshape(CMP_LEN * dk)
        c = _cmp_mlp(flat, pe, cmp_w1[i], cmp_b1[i], cmp_w2[i], cmp_b2[i])[:rows]
        c = c.reshape(batch, n_c, g, dk).transpose(0, 2, 1, 3)
        outs.append(jnp.pad(c, ((0, 0), (0, 0), (0, ncp - n_c), (0, 0))))
    return kv.astype(BF16), outs[0], outs[1], n_c


def _gate_combine_body(oc_ref, os_ref, ow_ref, gl_ref, ex_ref, o_ref):
    gates = jax.nn.sigmoid(gl_ref[...])
    acc = None
    for i, r in enumerate((oc_ref, os_ref, ow_ref)):
        term = _dot_split_const(gates, ex_ref[i]) * r[...].astype(F32)
        acc = term if acc is None else acc + term
    o_ref[...] = acc.astype(BF16)


def _gate_combine(o_c, o_s, o_w, glog, nh, tm=256):
    n, hd = o_c.shape
    dk = hd // nh
    ex = np.zeros((3, LANES, hd), np.float32)
    for i in range(3):
        for h in range(nh):
            ex[i, i * nh + h, h * dk:(h + 1) * dk] = 1.0
    row = pl.BlockSpec((tm, hd), lambda i: (i, 0))
    return pl.pallas_call(
        _gate_combine_body,
        grid=(n // tm,),
        in_specs=[row, row, row, pl.BlockSpec((tm, LANES), lambda i: (i, 0)),
                  pl.BlockSpec((3, LANES, hd), lambda i: (0, 0, 0))],
        out_specs=row,
        out_shape=jax.ShapeDtypeStruct((n, hd), BF16),
        compiler_params=_params("parallel"),
        name="nsa_gate_combine",
    )(o_c, o_s, o_w, glog, jnp.asarray(ex, BF16))


def _nsa_layer(xb, batch, shared, w_in, b_gate, w_o):
    kvb, k_cmp, v_cmp, n_c = shared
    n, d = xb.shape
    hd = w_o.shape[0]
    nh = hd // NSA_HEAD_DIM
    q = _matmul(xb, w_in[:, :hd].astype(BF16), out_dtype=BF16,
                out_scale=NSA_HEAD_DIM ** -0.5 * math.log2(math.e))
    wg = jnp.pad(w_in[:, hd:], ((0, 0), (0, LANES - 3 * nh))).astype(BF16)
    bg = jnp.pad(b_gate, (0, LANES - 3 * nh))
    glog = _matmul(xb, wg, bias=bg)
    o_c, o_s, o_w = _nsa_attention(q, kvb, k_cmp, v_cmp, batch, n_c)
    o = _gate_combine(o_c, o_s, o_w, glog, nh)
    return o, w_o.astype(BF16)


def kernel(x, ln_g, ln_b, rw_mu, rw_w_rkv, rw_w0, rw_w1, rw_w2, rw_a0, rw_a1, rw_a2, rw_g1, rw_g2, rw_k_k, rw_k_a, rw_r_k, rw_lnx_g, rw_lnx_b, rw_w_o, nsa_w_kv, nsa_cmp_pe, nsa_cmp_w1, nsa_cmp_b1, nsa_cmp_w2, nsa_cmp_b2, nsa_w_in, nsa_b_gate, nsa_w_o, moe_router_w, moe_router_b, moe_w_gu, moe_b_gu, moe_w_down, moe_b_down):
    batch, t, d = x.shape
    depth = ln_g.shape[0]
    n_a = rw_mu.shape[0]
    alpha = (2 * depth) ** 0.25
    h = x.reshape(batch * t, d)
    hb = None
    shared = None
    for layer in range(depth):
        if layer < n_a:
            i = layer
            mixed, w_o = _rwkv_time_mix(h, batch, rw_mu[i], rw_w_rkv[i], rw_w0[i], rw_w1[i], rw_w2[i], rw_a0[i],
                                        rw_a1[i], rw_a2[i], rw_g1[i], rw_g2[i], rw_k_k[i], rw_k_a[i], rw_r_k[i],
                                        rw_lnx_g[i], rw_lnx_b[i], rw_w_o[i])
        else:
            if shared is None:
                if hb is None:
                    hb = h.astype(BF16)
                shared = _nsa_shared_kv(hb, batch, nsa_w_kv, nsa_cmp_pe, nsa_cmp_w1, nsa_cmp_b1, nsa_cmp_w2,
                                        nsa_cmp_b2)
            j = layer - n_a
            mixed, w_o = _nsa_layer(hb, batch, shared, nsa_w_in[j], nsa_b_gate[j], nsa_w_o[j])
        h, hb = _proj_ln(mixed, w_o, h, ln_g[layer, 0], ln_b[layer, 0], alpha)
        y4, gates = _moe_ffn(hb, layer, moe_router_w[layer], moe_router_b[layer], moe_w_gu, moe_b_gu, moe_w_down,
                             moe_b_down)
        h, hb = _combine_ln(h, y4, gates, ln_g[layer, 1], ln_b[layer, 1], alpha)
    return h.reshape(batch, t, d)
```

```python
import functools
import math

import numpy as np
import jax
import jax.numpy as jnp
from jax import lax
from jax.experimental import pallas as pl
from jax.experimental.pallas import tpu as pltpu

F32 = jnp.float32
BF16 = jnp.bfloat16

V7X_VMEM_LIMIT_BYTES = 56 * 1024 * 1024
LANES = 128

LN_EPS = 1e-5
RW_HEAD_DIM = 64
RW_GN_EPS = 64e-5
RW_CHUNK = 64
RW_CHUNKS_PER_STEP = 8
NSA_HEAD_DIM = 128
NSA_KV_GROUPS = 2
CMP_STRIDE = 16
CMP_LEN = 32
SEL_BLOCK = 64
N_SEL = 16
WINDOW = 512
Q_BLOCK = 128
SEL_KEY_TILE = 1024
SEL_Q_TILE = 256
SEL_HEADS_PER_DOT = 2
CMP_HEADS_PER_DOT = 4
CMP_SEG_COLS = 256
N_EXPERTS = 32
TOP_K = 4
SWIGLU_LIMIT = 7.0
SWIGLU_ALPHA = 1.702
MOE_ROW_BLOCK = 1024
MOE_F_TILE = 512
MOE_OUT_TILE = 1024
NEG_BIG = -1e30


def _params(*sem):
    return pltpu.CompilerParams(dimension_semantics=sem, vmem_limit_bytes=V7X_VMEM_LIMIT_BYTES)


def _bdot(a, b):
    return jnp.dot(a.astype(BF16), b.astype(BF16), preferred_element_type=F32)


def _bdot_nt(a, b):
    return lax.dot_general(a.astype(BF16), b.astype(BF16), (((1,), (1,)), ((), ())),
                           preferred_element_type=F32)


def _dot_const_split(c, x):
    hi = x.astype(BF16)
    lo = (x - hi.astype(F32)).astype(BF16)
    return (jnp.dot(c, hi, preferred_element_type=F32) + jnp.dot(c, lo, preferred_element_type=F32))


def _dot_split_const(x, c):
    hi = x.astype(BF16)
    lo = (x - hi.astype(F32)).astype(BF16)
    return (jnp.dot(hi, c, preferred_element_type=F32) + jnp.dot(lo, c, preferred_element_type=F32))


def _mm_body(a_ref, w_ref, b_ref, o_ref, *, out_scale):
    acc = jnp.dot(a_ref[...], w_ref[...], preferred_element_type=F32) + b_ref[...]
    if out_scale is not None:
        acc = acc * out_scale
    o_ref[...] = acc.astype(o_ref.dtype)


def _matmul(a, w, bias=None, out_dtype=F32, out_scale=None, tm=512, tn=1024):
    m, k = a.shape
    n = w.shape[1]
    tm = min(tm, m)
    tn = min(tn, n)
    assert m % tm == 0 and n % tn == 0, (m, n, tm, tn)
    if bias is None:
        bias = jnp.zeros((1, n), F32)
    return pl.pallas_call(
        functools.partial(_mm_body, out_scale=out_scale),
        grid=(n // tn, m // tm),
        in_specs=[pl.BlockSpec((tm, k), lambda j, i: (i, 0)),
                  pl.BlockSpec((k, tn), lambda j, i: (0, j)),
                  pl.BlockSpec((1, tn), lambda j, i: (0, j))],
        out_specs=pl.BlockSpec((tm, tn), lambda j, i: (i, j)),
        out_shape=jax.ShapeDtypeStruct((m, n), out_dtype),
        compiler_params=_params("parallel", "parallel"),
        name="dense_matmul",
    )(a, w, bias.reshape(1, n).astype(F32))


def _layer_norm_rows(z, g, b):
    mu = jnp.mean(z, -1, keepdims=True)
    zc = z - mu
    var = jnp.mean(zc * zc, -1, keepdims=True)
    return zc * lax.rsqrt(var + LN_EPS) * g + b


def _proj_ln_body(a_ref, w_ref, x_ref, g_ref, b_ref, o_ref, ob_ref, *, alpha):
    z = alpha * x_ref[...] + jnp.dot(a_ref[...], w_ref[...], preferred_element_type=F32)
    y = _layer_norm_rows(z, g_ref[...], b_ref[...])
    o_ref[...] = y
    ob_ref[...] = y.astype(BF16)


def _proj_ln(a, w, x, g, b, alpha, tm=512):
    n, d = x.shape
    k = w.shape[0]
    tm = min(tm, n)
    row = pl.BlockSpec((tm, d), lambda i: (i, 0))
    vec = pl.BlockSpec((1, d), lambda i: (0, 0))
    return pl.pallas_call(
        functools.partial(_proj_ln_body, alpha=alpha),
        grid=(n // tm,),
        in_specs=[pl.BlockSpec((tm, k), lambda i: (i, 0)), pl.BlockSpec((k, d), lambda i: (0, 0)), row, vec, vec],
        out_specs=[row, row],
        out_shape=[jax.ShapeDtypeStruct((n, d), F32), jax.ShapeDtypeStruct((n, d), BF16)],
        compiler_params=_params("parallel"),
        name="proj_residual_layer_norm",
    )(a, w, x, g.reshape(1, d), b.reshape(1, d))


def _rw_mix_lowrank_body(x_ref, last_ref, mu_ref, w1_ref, w2_ref, w0_ref, a1_ref, a2_ref, a0_ref, g1_ref, g2_ref,
                         xr_ref, xk_ref, xv_ref, lw_ref, a_ref, g_ref):
    x = x_ref[...]
    prev = pltpu.roll(x, shift=1, axis=0)
    row = lax.broadcasted_iota(jnp.int32, x.shape, 0)
    prev = jnp.where(row == 0, last_ref[0], prev)
    xx = prev - x
    mixed = lambda i: (x + xx * mu_ref[i:i + 1, :]).astype(BF16)
    xr_ref[...] = mixed(0)
    xk_ref[...] = mixed(2)
    xv_ref[...] = mixed(3)
    z = w0_ref[...] + _bdot(jnp.tanh(jnp.dot(mixed(1), w1_ref[...], preferred_element_type=F32)), w2_ref[...])
    w_log = jnp.minimum(z, 0.0) - jnp.log(1.0 + jnp.exp(-jnp.abs(z))) - 0.5
    lw_ref[...] = -jnp.exp(w_log)
    za = a0_ref[...] + _bdot(jnp.dot(mixed(4), a1_ref[...], preferred_element_type=F32), a2_ref[...])
    a_ref[...] = jax.nn.sigmoid(za)
    hg = jax.nn.sigmoid(jnp.dot(mixed(5), g1_ref[...], preferred_element_type=F32))
    g_ref[...] = _bdot(hg, g2_ref[...])


def _pad_rank(w_in, w_out):
    r = w_in.shape[1]
    rp = -(-r // LANES) * LANES
    return (jnp.pad(w_in, ((0, 0), (0, rp - r))).astype(BF16), jnp.pad(w_out, ((0, rp - r), (0, 0))).astype(BF16))


def _rw_mix_lowrank(x2, batch, mu, w0, w1, w2, a0, a1, a2, g1, g2, tm=256):
    n, d = x2.shape
    t = n // batch
    nt = t // tm
    last = x2.reshape(batch, nt, tm, d)[:, :, tm - 1, :]
    last = jnp.concatenate([jnp.zeros((batch, 1, d), F32), last[:, :-1]], axis=1).reshape(batch * nt, 1, d)
    w1p, w2p = _pad_rank(w1, w2)
    a1p, a2p = _pad_rank(a1, a2)
    g1p, g2p = _pad_rank(g1, g2)
    row = pl.BlockSpec((tm, d), lambda i: (i, 0))
    full = lambda arr: pl.BlockSpec(arr.shape, lambda i: (0, 0))
    w0r, a0r = w0.reshape(1, d), a0.reshape(1, d)
    return pl.pallas_call(
        _rw_mix_lowrank_body,
        grid=(n // tm,),
        in_specs=[row, pl.BlockSpec((1, 1, d), lambda i: (i, 0, 0)), full(mu), full(w1p), full(w2p), full(w0r),
                  full(a1p), full(a2p), full(a0r), full(g1p), full(g2p)],
        out_specs=[row] * 6,
        out_shape=[jax.ShapeDtypeStruct((n, d), BF16)] * 3 + [jax.ShapeDtypeStruct((n, d), F32)] * 3,
        compiler_params=_params("parallel"),
        name="rwkv_shift_lowrank",
    )(x2, last, mu, w1p, w2p, w0r, a1p, a2p, a0r, g1p, g2p)


def _wkv_body(r_ref, k_ref, v_ref, lw_ref, a_ref, g_ref, kk_ref, ka_ref, rk_ref, lng_ref, lnb_ref,
              o_ref, s_ref, rp_ref, yq_ref, bonus_ref, gs_ref, *, nchunk):
    L = RW_CHUNK
    H2 = 2 * L

    step = pl.program_id(2)
    slot_w = step % 2
    slot_r = 1 - slot_w

    @pl.when(step == 0)
    def _():
        s_ref[...] = jnp.zeros_like(s_ref)
        rp_ref[1] = jnp.zeros(rp_ref.shape[1:], F32)
        yq_ref[1] = jnp.zeros(yq_ref.shape[1:], F32)
        bonus_ref[1] = jnp.zeros(bonus_ref.shape[1:], F32)
        gs_ref[1] = jnp.zeros(gs_ref.shape[1:], F32)

    lane = lax.broadcasted_iota(jnp.int32, (1, LANES), 1)
    mask0 = (lane < RW_HEAD_DIM).astype(F32)
    mask1 = 1.0 - mask0
    ri = lax.broadcasted_iota(jnp.int32, (H2, H2), 0)
    ci = lax.broadcasted_iota(jnp.int32, (H2, H2), 1)
    same_head = (ri // L) == (ci // L)
    strict = (same_head & (ci < ri)).astype(F32)
    incl = (same_head & (ci <= ri)).astype(F32)
    diag16 = ((ri // 16) == (ci // 16)).astype(F32)
    eye = (ri == ci).astype(F32)
    head_ones = same_head.astype(BF16)
    tl = lax.broadcasted_iota(jnp.int32, (L, L), 0)
    sl = lax.broadcasted_iota(jnp.int32, (L, L), 1)
    tri_incl = (sl <= tl).astype(BF16)

    def stack(x):
        return jnp.concatenate([x * mask0, x * mask1], axis=0)

    k_k = kk_ref[...]
    k_a = ka_ref[...]
    r_k = rk_ref[...]

    chunks = range(nchunk)

    rp_prev = [rp_ref[slot_r, c] for c in chunks]
    yq_prev = [yq_ref[slot_r, c] for c in chunks]
    chain = {"s": s_ref[...], "ys": []}

    def chain_step():
        c = len(chain["ys"])
        if c >= nchunk:
            return
        res = _bdot(rp_prev[c], chain["s"]) + yq_prev[c]
        chain["ys"].append(res[:L] + res[L:H2])
        chain["s"] = res[H2:]
        if c == nchunk - 1:
            s_ref[...] = chain["s"]
            y = jnp.concatenate(chain["ys"], axis=0)
            inv_n = 1.0 / RW_HEAD_DIM
            ym = _dot_split_const(y, head_ones) * inv_n
            yc = y - ym
            yv = _dot_split_const(yc * yc, head_ones) * inv_n
            yn = yc * lax.rsqrt(yv + RW_GN_EPS) * lng_ref[...] + lnb_ref[...]
            o_ref[...] = ((yn + bonus_ref[slot_r]) * gs_ref[slot_r]).astype(BF16)

    def each(fn, *lists):
        return [fn(*xs) for xs in zip(*lists)]

    def mm_stage(fn, *lists):
        out = each(fn, *lists)
        chain_step()
        return out

    def rows_of(x):
        return [x[c * L:(c + 1) * L] for c in chunks]

    r_all = r_ref[...]
    k_all = k_ref[...]
    v_all = v_ref[...]
    ag_all = a_ref[...]
    kk_all = k_all * k_k
    ss_all = _dot_split_const(kk_all * kk_all, head_ones)
    kk_all = kk_all / jnp.maximum(jnp.sqrt(ss_all), 1e-12)
    kmod_all = k_all * (1.0 + (ag_all - 1.0) * k_a)
    bv_all = kk_all * ag_all
    lw_c = rows_of(lw_ref[...])
    cl_c = each(lambda lw: _dot_const_split(tri_incl, lw), lw_c)
    last_c = each(lambda cl: cl[L - 1:L, :], cl_c)
    cl_all = jnp.concatenate(cl_c, axis=0)
    clp_all = cl_all - lw_ref[...]
    end_all = jnp.concatenate(each(lambda cl, la: la - cl, cl_c, last_c), axis=0)
    e_neg = jnp.exp(-cl_all)
    e_end = jnp.exp(end_all)
    at_c = rows_of(-kk_all * jnp.exp(clp_all))
    rt_c = rows_of(r_all * jnp.exp(cl_all))
    bt_c = rows_of(bv_all * e_neg)
    kt_c = rows_of(kmod_all * e_neg)
    be_c = rows_of(bv_all * e_end)
    ke_c = rows_of(kmod_all * e_end)
    v_s = each(stack, rows_of(v_all))
    at_s = each(stack, at_c)
    rt_s = each(stack, rt_c)
    gmat = mm_stage(lambda a_, r_, b_, k_: _bdot_nt(jnp.concatenate([a_, r_], axis=0),
                                                    jnp.concatenate([stack(b_), stack(k_)], axis=0)),
                    at_s, rt_s, bt_c, kt_c)
    a_ab = each(lambda gm: gm[:H2, :H2] * strict, gmat)
    a_ak = each(lambda gm: gm[:H2, H2:] * strict, gmat)
    a_rb = each(lambda gm: gm[H2:, :H2] * incl, gmat)
    a_rk = each(lambda gm: gm[H2:, H2:] * incl, gmat)
    dblk = each(lambda a_: a_ * diag16, a_ab)
    off = each(lambda a_, d_: a_ - d_, a_ab, dblk)
    d2 = mm_stage(_bdot, dblk, dblk)
    d4 = mm_stage(_bdot, d2, d2)
    d8 = mm_stage(_bdot, d4, d4)
    dinv = mm_stage(lambda d_, d2_: _bdot(eye + d_, eye + d2_), dblk, d2)
    dinv = mm_stage(lambda di, d4_: _bdot(di, eye + d4_), dinv, d4)
    dinv = mm_stage(lambda di, d8_: _bdot(di, eye + d8_), dinv, d8)
    e1 = mm_stage(_bdot, dinv, off)
    e2 = mm_stage(_bdot, e1, e1)
    minv = mm_stage(lambda e1_, e2_: _bdot(eye + e1_, eye + e2_), e1, e2)
    minv = mm_stage(_bdot, minv, dinv)
    x_ak = mm_stage(_bdot, a_ak, v_s)
    zu = each(lambda mi, a_, x_: _bdot(mi, jnp.concatenate([a_, x_], axis=1)), minv, at_s, x_ak)
    w2 = each(_bdot, a_rb, zu)
    rkv = each(_bdot, a_rk, v_s)
    pq = each(lambda b_, z_: _bdot(stack(b_).T, z_), be_c, zu)
    kv2 = each(lambda k_, v_: _bdot(stack(k_).T, v_), ke_c, v_s)
    while len(chain["ys"]) < nchunk:
        chain_step()
    for c in chunks:
        rp_ref[slot_w, c] = jnp.concatenate([rt_s[c] + w2[c][:, :LANES],
                                             eye * jnp.exp(last_c[c]) + pq[c][:, :LANES]], axis=0)
        yq_ref[slot_w, c] = jnp.concatenate([w2[c][:, LANES:] + rkv[c], pq[c][:, LANES:] + kv2[c]], axis=0)
    bonus_ref[slot_w] = _dot_split_const(r_all * kmod_all * r_k, head_ones) * v_all
    gs_ref[slot_w] = g_ref[...]


def _wkv(r, k, v, lw, a, g, k_k, k_a, r_k, lnx_g, lnx_b, batch):
    n, d = r.shape
    t = n // batch
    nchunk = RW_CHUNKS_PER_STEP
    tb = RW_CHUNK * nchunk
    while t % tb:
        nchunk //= 2
        tb = RW_CHUNK * nchunk
    nt = t // tb
    row_in = pl.BlockSpec((tb, LANES), lambda b, hp, c: (b * nt + jnp.minimum(c, nt - 1), hp))
    row_out = pl.BlockSpec((tb, LANES), lambda b, hp, c: (b * nt + jnp.maximum(c - 1, 0), hp))
    vec = pl.BlockSpec((1, LANES), lambda b, hp, c: (0, hp))
    sq = pltpu.VMEM((2, nchunk, 2 * LANES, LANES), F32)
    blk = pltpu.VMEM((2, tb, LANES), F32)
    vecs = [z.reshape(1, d) for z in (k_k, k_a, r_k, lnx_g, lnx_b)]
    return pl.pallas_call(
        functools.partial(_wkv_body, nchunk=nchunk),
        grid=(batch, d // LANES, nt + 1),
        in_specs=[row_in] * 6 + [vec] * 5,
        out_specs=row_out,
        out_shape=jax.ShapeDtypeStruct((n, d), BF16),
        scratch_shapes=[pltpu.VMEM((LANES, LANES), F32), sq, sq, blk, blk],
        compiler_params=_params("parallel", "parallel", "arbitrary"),
        name="rwkv_chunked_scan",
    )(r, k, v, lw, a, g, *vecs)


def _rwkv_time_mix(x2, batch, mu, w_rkv, w0, w1, w2, a0, a1, a2, g1, g2, k_k, k_a, r_k, lnx_g, lnx_b, w_o):
    xr, xk, xv, lw, a, g = _rw_mix_lowrank(x2, batch, mu, w0, w1, w2, a0, a1, a2, g1, g2)
    r = _matmul(xr, w_rkv[0].astype(BF16))
    k = _matmul(xk, w_rkv[1].astype(BF16))
    v = _matmul(xv, w_rkv[2].astype(BF16))
    z = _wkv(r, k, v, lw, a, g, k_k, k_a, r_k.reshape(-1), lnx_g, lnx_b, batch)
    return z, w_o.astype(BF16)


def _router_body(x_ref, w_ref, b_ref, idx_ref, gate_ref):
    logits = jnp.dot(x_ref[...], w_ref[...], preferred_element_type=F32) + b_ref[...]
    lane = lax.broadcasted_iota(jnp.int32, logits.shape, 1).astype(F32)
    cur = logits
    vals, idxs = [], []
    for _ in range(TOP_K):
        m = jnp.max(cur, axis=-1, keepdims=True)
        i = jnp.min(jnp.where(cur == m, lane, float(LANES)), axis=-1, keepdims=True)
        vals.append(m)
        idxs.append(i)
        cur = jnp.where(lane == i, -3e38, cur)
    es = [jnp.exp(vv - vals[0]) for vv in vals]
    den = es[0]
    for e in es[1:]:
        den = den + e
    idx_out = jnp.zeros(logits.shape, F32)
    gate_out = jnp.zeros(logits.shape, F32)
    for kk in range(TOP_K):
        idx_out = jnp.where(lane == kk, idxs[kk], idx_out)
        gate_out = jnp.where(lane == kk, es[kk] / den, gate_out)
    idx_ref[...] = idx_out.astype(jnp.int32)
    gate_ref[...] = gate_out


def _router(xb, router_w, router_b, tm=512):
    n, d = xb.shape
    e = router_w.shape[1]
    wp = jnp.pad(router_w, ((0, 0), (0, LANES - e))).astype(BF16)
    bp = jnp.concatenate([router_b.astype(F32), jnp.full((LANES - e,), NEG_BIG, F32)]).reshape(1, LANES)
    row = pl.BlockSpec((tm, LANES), lambda i: (i, 0))
    idx, gate = pl.pallas_call(
        _router_body,
        grid=(n // tm,),
        in_specs=[pl.BlockSpec((tm, d), lambda i: (i, 0)), pl.BlockSpec((d, LANES), lambda i: (0, 0)),
                  pl.BlockSpec((1, LANES), lambda i: (0, 0))],
        out_specs=[row, row],
        out_shape=[jax.ShapeDtypeStruct((n, LANES), jnp.int32), jax.ShapeDtypeStruct((n, LANES), F32)],
        compiler_params=_params("parallel"),
        name="moe_router",
    )(xb, wp, bp)
    return idx[:, :TOP_K], gate


def _moe_up_body(it_ref, ct_ref, be_ref, nu_ref, x_ref, wg_ref, wu_ref, bg_ref, bu_ref, o_ref):
    used = it_ref[pl.program_id(0)] < nu_ref[0]

    @pl.when(used)
    def _():
        x = x_ref[...]
        gate = jnp.dot(x, wg_ref[...].astype(BF16), preferred_element_type=F32) + bg_ref[...]
        up = jnp.dot(x, wu_ref[...].astype(BF16), preferred_element_type=F32) + bu_ref[...]
        gate = jnp.minimum(gate, SWIGLU_LIMIT)
        up = jnp.clip(up, -SWIGLU_LIMIT, SWIGLU_LIMIT)
        o_ref[...] = ((up + 1.0) * (gate * jax.nn.sigmoid(gate * SWIGLU_ALPHA))).astype(o_ref.dtype)

    @pl.when(jnp.logical_not(used))
    def _():
        o_ref[...] = jnp.zeros_like(o_ref)


def _moe_down_body(it_ref, ct_ref, be_ref, nu_ref, a_ref, wd_ref, bd_ref, o_ref):
    used = it_ref[pl.program_id(0)] < nu_ref[0]

    @pl.when(used)
    def _():
        y = jnp.dot(a_ref[...], wd_ref[...].astype(BF16), preferred_element_type=F32) + bd_ref[...]
        o_ref[...] = y.astype(o_ref.dtype)

    @pl.when(jnp.logical_not(used))
    def _():
        o_ref[...] = jnp.zeros_like(o_ref)


def _expert_major_steps(blk_e, n_blk, n_col):
    col = jnp.repeat(jnp.arange(n_col, dtype=jnp.int32), n_blk)
    blk = jnp.tile(jnp.arange(n_blk, dtype=jnp.int32), n_col)
    experts = jnp.arange(N_EXPERTS, dtype=jnp.int32)
    first = jnp.sum((blk_e[None, :] < experts[:, None]).astype(jnp.int32), axis=1)
    count = jnp.sum((blk_e[None, :] == experts[:, None]).astype(jnp.int32), axis=1)
    e = blk_e[blk]
    pos = n_col * first[e] + col * count[e] + (blk - first[e])
    zeros = jnp.zeros((n_col * n_blk,), jnp.int32)
    return zeros.at[pos].set(blk), zeros.at[pos].set(col)


def _moe_experts(xs, blk_e, n_used, layer, w_gu, b_gu, w_down, b_down):
    n_rows, d = xs.shape
    depth, ne, _, f2 = w_gu.shape
    fdim = f2 // 2
    tm, tf, tn = MOE_ROW_BLOCK, min(MOE_F_TILE, fdim), min(MOE_OUT_TILE, d)
    nf, nn = fdim // tf, d // tn
    n_blk = n_rows // tm
    bgu = b_gu.reshape(depth, ne, 1, f2)

    it, ct = _expert_major_steps(blk_e, n_blk, nf)
    act = pl.pallas_call(
        _moe_up_body,
        grid_spec=pltpu.PrefetchScalarGridSpec(
            num_scalar_prefetch=4,
            grid=(nf * n_blk,),
            in_specs=[
                pl.BlockSpec((tm, d), lambda t, it, ct, be, nu: (jnp.minimum(it[t], nu[0] - 1), 0)),
                pl.BlockSpec((None, None, d, tf), lambda t, it, ct, be, nu: (layer, be[it[t]], 0, ct[t])),
                pl.BlockSpec((None, None, d, tf), lambda t, it, ct, be, nu: (layer, be[it[t]], 0, nf + ct[t])),
                pl.BlockSpec((None, None, 1, tf), lambda t, it, ct, be, nu: (layer, be[it[t]], 0, ct[t])),
                pl.BlockSpec((None, None, 1, tf), lambda t, it, ct, be, nu: (layer, be[it[t]], 0, nf + ct[t])),
            ],
            out_specs=pl.BlockSpec((tm, tf), lambda t, it, ct, be, nu: (it[t], ct[t])),
        ),
        out_shape=jax.ShapeDtypeStruct((n_rows, fdim), BF16),
        compiler_params=_params("arbitrary"),
        name="moe_experts_up",
    )(it, ct, blk_e, n_used, xs, w_gu, w_gu, bgu, bgu)

    it2, ct2 = _expert_major_steps(blk_e, n_blk, nn)
    return pl.pallas_call(
        _moe_down_body,
        grid_spec=pltpu.PrefetchScalarGridSpec(
            num_scalar_prefetch=4,
            grid=(nn * n_blk,),
            in_specs=[
                pl.BlockSpec((tm, fdim), lambda t, it, ct, be, nu: (jnp.minimum(it[t], nu[0] - 1), 0)),
                pl.BlockSpec((None, None, fdim, tn), lambda t, it, ct, be, nu: (layer, be[it[t]], 0, ct[t])),
                pl.BlockSpec((None, None, 1, tn), lambda t, it, ct, be, nu: (layer, be[it[t]], 0, ct[t])),
            ],
            out_specs=pl.BlockSpec((tm, tn), lambda t, it, ct, be, nu: (it[t], ct[t])),
        ),
        out_shape=jax.ShapeDtypeStruct((n_rows, d), BF16),
        compiler_params=_params("arbitrary"),
        name="moe_experts_down",
    )(it2, ct2, blk_e, n_used, act, w_down, b_down.reshape(depth, ne, 1, d))


def _combine_ln_body(x_ref, y_ref, gate_ref, g_ref, b_ref, o_ref, ob_ref, *, alpha):
    ffn = y_ref[0].astype(F32) * gate_ref[:, 0:1]
    for kk in range(1, TOP_K):
        ffn = ffn + y_ref[kk].astype(F32) * gate_ref[:, kk:kk + 1]
    z = alpha * x_ref[...] + ffn
    mu = jnp.mean(z, -1, keepdims=True)
    zc = z - mu
    var = jnp.mean(zc * zc, -1, keepdims=True)
    y = zc * lax.rsqrt(var + LN_EPS) * g_ref[...] + b_ref[...]
    o_ref[...] = y
    ob_ref[...] = y.astype(BF16)


def _combine_ln(x, y4, gates, g, b, alpha, tm=256):
    n, d = x.shape
    row = pl.BlockSpec((tm, d), lambda i: (i, 0))
    vec = pl.BlockSpec((1, d), lambda i: (0, 0))
    return pl.pallas_call(
        functools.partial(_combine_ln_body, alpha=alpha),
        grid=(n // tm,),
        in_specs=[row, pl.BlockSpec((TOP_K, tm, d), lambda i: (0, i, 0)),
                  pl.BlockSpec((tm, LANES), lambda i: (i, 0)), vec, vec],
        out_specs=[row, row],
        out_shape=[jax.ShapeDtypeStruct((n, d), F32), jax.ShapeDtypeStruct((n, d), BF16)],
        compiler_params=_params("parallel"),
        name="moe_combine_layer_norm",
    )(x, y4, gates, g.reshape(1, d), b.reshape(1, d))


def _moe_ffn(xb, layer, router_w, router_b, w_gu, b_gu, w_down, b_down):
    n, d = xb.shape
    tm = MOE_ROW_BLOCK
    top_i, gates = _router(xb, router_w, router_b)
    flat_e = top_i.reshape(-1)
    onehot = (flat_e[:, None] == jnp.arange(N_EXPERTS, dtype=jnp.int32)[None, :]).astype(jnp.int32)
    csum = jnp.cumsum(onehot, axis=0)
    rank = jnp.take_along_axis(csum, flat_e[:, None], axis=1)[:, 0] - 1
    counts = csum[-1]
    padded = ((counts + tm - 1) // tm) * tm
    pad_end = jnp.cumsum(padded)
    pad_start = pad_end - padded
    dest = pad_start[flat_e] + rank
    n_rows = -(-(n * TOP_K) // tm) * tm + N_EXPERTS * tm
    n_blk = n_rows // tm
    blk_start = jnp.arange(n_blk, dtype=jnp.int32) * tm
    blk_e = jnp.minimum(jnp.sum((pad_end[None, :] <= blk_start[:, None]).astype(jnp.int32), axis=1),
                        N_EXPERTS - 1).astype(jnp.int32)
    n_used = (pad_end[-1] // tm).astype(jnp.int32).reshape(1)
    flat_tok = jnp.arange(n * TOP_K, dtype=jnp.int32) // TOP_K
    row_tok = (jnp.arange(n_rows, dtype=jnp.int32) % n).at[dest].set(flat_tok)
    xs = jnp.take(xb, row_tok, axis=0, mode='clip')
    ys = _moe_experts(xs, blk_e, n_used, layer, w_gu, b_gu, w_down, b_down)
    dest_kmajor = dest.reshape(n, TOP_K).T.reshape(-1)
    y4 = jnp.take(ys, dest_kmajor, axis=0, mode='clip').reshape(TOP_K, n, d)
    return y4, gates


def _cmp_mlp_body(x_ref, pe_ref, w1_ref, b1_ref, w2_ref, b2_ref, o_ref):
    h = _bdot(x_ref[...] + pe_ref[...], w1_ref[...]) + b1_ref[...]
    h = jax.nn.gelu(h)
    o_ref[...] = (_bdot(h, w2_ref[...]) + b2_ref[...]).astype(o_ref.dtype)


def _cmp_mlp(flat, pe, w1, b1, w2, b2, tm=256):
    m, kd = flat.shape
    hid = w1.shape[1]
    dk = w2.shape[1]
    full = lambda shp: pl.BlockSpec(shp, lambda i: (0, 0))
    return pl.pallas_call(
        _cmp_mlp_body,
        grid=(m // tm,),
        in_specs=[pl.BlockSpec((tm, kd), lambda i: (i, 0)), full((1, kd)), full((kd, hid)), full((1, hid)),
                  full((hid, dk)), full((1, dk))],
        out_specs=pl.BlockSpec((tm, dk), lambda i: (i, 0)),
        out_shape=jax.ShapeDtypeStruct((m, dk), BF16),
        compiler_params=_params("parallel"),
        name="nsa_compress_mlp",
    )(flat, pe.reshape(1, kd), w1.astype(BF16), b1.reshape(1, hid), w2.astype(BF16), b2.reshape(1, dk))


def _group_rows(q_ref, hpg):
    dk = NSA_HEAD_DIM
    return jnp.concatenate([q_ref[:, h * dk:(h + 1) * dk] for h in range(hpg)], axis=0)


def _softmax_rows(s, mask):
    s = jnp.where(mask, s, NEG_BIG)
    m = jnp.max(s, axis=-1, keepdims=True)
    e = jnp.where(mask, jnp.exp2(s - m), 0.0)
    den = jnp.sum(e, axis=-1, keepdims=True)
    return e / jnp.where(den > 0, den, 1.0)


def _exp2_rows(s_rows, bias, cols):
    s_h = [s_rows[:, cj] + bias[:, cj] for cj in cols]
    mx = s_h[0]
    for s_hj in s_h[1:]:
        mx = jnp.maximum(mx, s_hj)
    m = jnp.broadcast_to(jnp.max(mx, axis=-1, keepdims=True), mx.shape)
    return jnp.concatenate([jnp.exp2(s_hj - m).astype(BF16) for s_hj in s_h], axis=1)


def _nsa_cmp_body(q_ref, kc_ref, vc_ref, ov_ref, *rest, hpg, n_c, n_sel, qb0):
    o_ref, sel_ref = rest[-2:]
    qb = pl.program_id(2) + qb0
    t0 = qb * Q_BLOCK
    ncols = kc_ref.shape[0]
    n_s = ov_ref.shape[1]
    dk = NSA_HEAD_DIM
    tq_c = t0 + lax.broadcasted_iota(jnp.int32, (Q_BLOCK, ncols), 0)
    cid = lax.broadcasted_iota(jnp.int32, (Q_BLOCK, ncols), 1)
    bias = jnp.where((cid * CMP_STRIDE + (CMP_LEN - 1) <= tq_c) & (cid < n_c), 0.0, NEG_BIG)
    t_row = t0 + lax.broadcasted_iota(jnp.int32, (Q_BLOCK, LANES), 0)
    row_live = t_row >= CMP_LEN - 1
    cols = [slice(j * LANES, (j + 1) * LANES) for j in range(ncols // LANES)]
    k_tile = kc_ref[...]
    v_aug = jnp.concatenate([vc_ref[...], jnp.ones((ncols, LANES), BF16)], axis=1)
    ov = ov_ref[...]
    hg = min(CMP_HEADS_PER_DOT, hpg)
    imp = jnp.zeros((Q_BLOCK, n_s), F32)
    for g0 in range(0, hpg, hg):
        qg = jnp.concatenate([q_ref[:, h * dk:(h + 1) * dk] for h in range(g0, g0 + hg)], axis=0)
        s_g = _bdot_nt(qg, k_tile)
        e = jnp.concatenate([_exp2_rows(s_g[hl * Q_BLOCK:(hl + 1) * Q_BLOCK], bias, cols) for hl in range(hg)], axis=0)
        od = jnp.dot(e, v_aug, preferred_element_type=F32)
        ih = jnp.dot(e, ov, preferred_element_type=F32)
        for hl in range(hg):
            rows = slice(hl * Q_BLOCK, (hl + 1) * Q_BLOCK)
            inv = jnp.where(row_live, 1.0 / od[rows, dk:], 0.0)
            o_ref[:, (g0 + hl) * dk:(g0 + hl + 1) * dk] = (od[rows, :dk] * inv).astype(o_ref.dtype)
            inv_s = inv[:, :n_s] if n_s <= LANES else jnp.concatenate([inv] * (n_s // LANES), axis=1)
            imp = imp + ih[rows] * inv_s
    imp_t = imp.T
    tq = t0 + lax.broadcasted_iota(jnp.int32, (n_s, Q_BLOCK), 1)
    sid_i = lax.broadcasted_iota(jnp.int32, (n_s, Q_BLOCK), 0)
    cur = tq // SEL_BLOCK
    forced = (sid_i == 0) | (sid_i == cur) | (sid_i == cur - 1)
    score = jnp.where(sid_i * SEL_BLOCK <= tq, jnp.where(forced, 1e30, imp_t), -1.0)
    sid = sid_i.astype(F32)
    sel = jnp.zeros((n_s, Q_BLOCK), F32)
    for _ in range(n_sel):
        m = jnp.max(score, axis=0, keepdims=True)
        first = jnp.min(jnp.where(score == m, sid, float(n_s)), axis=0, keepdims=True)
        hit = sid == first
        sel = jnp.where(hit, 1.0, sel)
        score = jnp.where(hit, -2.0, score)
    sel_ref[...] = sel.T.astype(BF16)


def _nsa_win_body(q_ref, *refs, hpg, nwb):
    k_refs, v_refs, o_ref = refs[:nwb], refs[nwb:2 * nwb], refs[2 * nwb]
    qb = pl.program_id(2)
    t0 = qb * Q_BLOCK
    dk = NSA_HEAD_DIM
    nk = nwb * Q_BLOCK
    kcat = jnp.concatenate([r[...] for r in k_refs], axis=0)
    v_aug = jnp.concatenate([r[...] for r in v_refs], axis=0)
    v_aug = jnp.concatenate([v_aug, jnp.ones((nk, LANES), BF16)], axis=1)
    t = t0 + lax.broadcasted_iota(jnp.int32, (Q_BLOCK, nk), 0)
    kpos = t0 - WINDOW + lax.broadcasted_iota(jnp.int32, (Q_BLOCK, nk), 1)
    bias = jnp.where((kpos <= t) & (kpos > t - WINDOW) & (kpos >= 0), 0.0, NEG_BIG)
    cols = [slice(j * LANES, (j + 1) * LANES) for j in range(nk // LANES)]
    hg = min(CMP_HEADS_PER_DOT, hpg)
    for g0 in range(0, hpg, hg):
        qg = jnp.concatenate([q_ref[:, h * dk:(h + 1) * dk] for h in range(g0, g0 + hg)], axis=0)
        s_g = _bdot_nt(qg, kcat)
        e = jnp.concatenate([_exp2_rows(s_g[hl * Q_BLOCK:(hl + 1) * Q_BLOCK], bias, cols) for hl in range(hg)], axis=0)
        od = jnp.dot(e, v_aug, preferred_element_type=F32)
        for hl in range(hg):
            rows = slice(hl * Q_BLOCK, (hl + 1) * Q_BLOCK)
            o_ref[:, (g0 + hl) * dk:(g0 + hl + 1) * dk] = (od[rows, :dk] / od[rows, dk:]).astype(o_ref.dtype)


def _nsa_sel_body(qb_ref, kb_ref, q_ref, k_ref, v_ref, sel_ref, pick_ref, mk_ref, o_ref, m_ref, acc_ref, *, hpg):
    step = pl.program_id(2)
    qb = qb_ref[step]
    kb = kb_ref[step]
    qn = q_ref.shape[0]
    t0 = qb * qn
    tk = k_ref.shape[0]
    dk = NSA_HEAD_DIM

    @pl.when(kb == 0)
    def _():
        m_ref[...] = jnp.full_like(m_ref, NEG_BIG)
        acc_ref[...] = jnp.zeros_like(acc_ref)

    unpicked = 1.0 - jnp.dot(sel_ref[...], pick_ref[...], preferred_element_type=F32)
    q_mask = unpicked.astype(BF16)
    cols = [slice(j * LANES, (j + 1) * LANES) for j in range(tk // LANES)]
    k_ext = jnp.concatenate([k_ref[...], mk_ref[...]], axis=1)
    v_aug = jnp.concatenate([v_ref[...], jnp.ones((tk, LANES), BF16)], axis=1)
    hg = SEL_HEADS_PER_DOT
    last = kb == (t0 + qn - 1) // tk

    def scores(g0):
        qg = jnp.concatenate([jnp.concatenate([q_ref[:, h * dk:(h + 1) * dk], q_mask], axis=1)
                              for h in range(g0, g0 + hg)], axis=0)
        return _bdot_nt(qg, k_ext)

    def process(causal_bias):
        s_next = scores(0)
        for g0 in range(0, hpg, hg):
            s_g = s_next
            if g0 + hg < hpg:
                s_next = scores(g0 + hg)
            p_rows, alphas = [], []
            for hl in range(hg):
                rows = slice((g0 + hl) * qn, (g0 + hl + 1) * qn)
                s_h = [s_g[hl * qn:(hl + 1) * qn, cj] for cj in cols]
                if causal_bias is not None:
                    s_h = [s_hj + causal_bias[:, cj] for s_hj, cj in zip(s_h, cols)]
                mx = s_h[0]
                for s_hj in s_h[1:]:
                    mx = jnp.maximum(mx, s_hj)
                m_old = m_ref[rows, :]
                m_new = jnp.maximum(m_old, jnp.broadcast_to(jnp.max(mx, axis=-1, keepdims=True), m_old.shape))
                m_ref[rows, :] = m_new
                alphas.append(jnp.exp2(m_old - m_new))
                p_rows.append(jnp.concatenate([jnp.exp2(s_hj - m_new).astype(BF16) for s_hj in s_h], axis=1))
            pv = jnp.dot(jnp.concatenate(p_rows, axis=0), v_aug, preferred_element_type=F32)
            alpha = jnp.concatenate(alphas, axis=0)
            grows = slice(g0 * qn, (g0 + hg) * qn)
            acc_ref[grows, :dk] = alpha * acc_ref[grows, :dk] + pv[:, :dk]
            acc_ref[grows, dk:] = alpha * acc_ref[grows, dk:] + pv[:, dk:]

    @pl.when(jnp.logical_not(last))
    def _():
        process(None)

    @pl.when(last)
    def _():
        tq = t0 + lax.broadcasted_iota(jnp.int32, (qn, tk), 0)
        kpos = kb * tk + lax.broadcasted_iota(jnp.int32, (qn, tk), 1)
        process(jnp.where(kpos <= tq, 0.0, NEG_BIG))
        den = acc_ref[:, dk:]
        o = acc_ref[:, :dk] / jnp.where(den > 0, den, 1.0)
        for h in range(hpg):
            o_ref[:, h * dk:(h + 1) * dk] = o[h * qn:(h + 1) * qn].astype(o_ref.dtype)


def _nsa_attention(q, kvb, k_cmp, v_cmp, batch, n_c):
    n, hd = q.shape
    dk, g = NSA_HEAD_DIM, NSA_KV_GROUPS
    hpg = hd // dk // g
    t = n // batch
    nqb = t // Q_BLOCK
    n_s = t // SEL_BLOCK
    n_sel = min(N_SEL, n_s)
    ncp = k_cmp.shape[2]
    gw = hpg * dk

    c_lo = np.arange(ncp) * CMP_STRIDE
    s_lo = np.arange(n_s) * SEL_BLOCK
    overlap = ((c_lo[:, None] < s_lo[None, :] + SEL_BLOCK) & (c_lo[:, None] + CMP_LEN > s_lo[None, :])
               & (np.arange(ncp)[:, None] < n_c))
    overlap = jnp.asarray(overlap, BF16)

    qspec = pl.BlockSpec((Q_BLOCK, gw), lambda b, gi, qb: (b * nqb + qb, gi))
    seg_qb = CMP_SEG_COLS * CMP_STRIDE // Q_BLOCK
    o_c = sel = None
    for qb0 in range(0, nqb, seg_qb):
        nq = min(seg_qb, nqb - qb0)
        ncols = min(ncp, -(-((qb0 + nq) * Q_BLOCK // CMP_STRIDE) // LANES) * LANES)
        oq = pl.BlockSpec((Q_BLOCK, gw), lambda b, gi, qb, qb0=qb0: (b * nqb + qb0 + qb, gi))
        in_specs = [oq,
                    pl.BlockSpec((None, None, ncols, dk), lambda b, gi, qb: (b, gi, 0, 0)),
                    pl.BlockSpec((None, None, ncols, dk), lambda b, gi, qb: (b, gi, 0, 0)),
                    pl.BlockSpec((ncols, n_s), lambda b, gi, qb: (0, 0))]
        args = [q, k_cmp, v_cmp, overlap]
        aliases = {}
        if o_c is not None:
            in_specs += [pl.BlockSpec(memory_space=pl.ANY), pl.BlockSpec(memory_space=pl.ANY)]
            args += [o_c, sel]
            aliases = {4: 0, 5: 1}
        o_c, sel = pl.pallas_call(
            functools.partial(_nsa_cmp_body, hpg=hpg, n_c=n_c, n_sel=n_sel, qb0=qb0),
            grid=(batch, g, nq),
            in_specs=in_specs,
            out_specs=[oq, pl.BlockSpec((None, None, Q_BLOCK, n_s), lambda b, gi, qb, qb0=qb0: (b, gi, qb0 + qb, 0))],
            out_shape=[jax.ShapeDtypeStruct((n, hd), BF16), jax.ShapeDtypeStruct((batch, g, t, n_s), BF16)],
            input_output_aliases=aliases,
            compiler_params=_params("parallel", "parallel", "parallel"),
            name="nsa_compressed_select",
        )(*args)

    nwb = WINDOW // Q_BLOCK + 1
    kcol, vcol = 4 * g, 5 * g

    def kv_spec(col, j):
        return pl.BlockSpec((Q_BLOCK, dk),
                            lambda b, gi, qb: (b * nqb + jnp.maximum(qb - (nwb - 1) + j, 0), col + gi))

    o_w = pl.pallas_call(
        functools.partial(_nsa_win_body, hpg=hpg, nwb=nwb),
        grid=(batch, g, nqb),
        in_specs=[qspec] + [kv_spec(kcol, j) for j in range(nwb)] + [kv_spec(vcol, j) for j in range(nwb)],
        out_specs=qspec,
        out_shape=jax.ShapeDtypeStruct((n, hd), BF16),
        compiler_params=_params("parallel", "parallel", "parallel"),
        name="nsa_window",
    )(q, *([kvb] * (2 * nwb)))

    tk = min(SEL_KEY_TILE, t)
    nkb = t // tk
    qn = min(SEL_Q_TILE, t)
    nqt = t // qn
    steps = [(qb, kb) for qb in range(nqt) for kb in range((qb * qn + qn - 1) // tk + 1)]
    qb_tab = jnp.asarray([s_[0] for s_ in steps], jnp.int32)
    kb_tab = jnp.asarray([s_[1] for s_ in steps], jnp.int32)
    spt = tk // SEL_BLOCK
    assert spt <= LANES
    pick = np.zeros((nkb, n_s, LANES), np.float32)
    for kb_ in range(nkb):
        pick[kb_, kb_ * spt + np.arange(spt), np.arange(spt)] = 1.0
    mk = np.zeros((tk, LANES), np.float32)
    mk[np.arange(tk), np.arange(tk) // SEL_BLOCK] = -(2.0 ** 100)
    pick, mk = jnp.asarray(pick, BF16), jnp.asarray(mk, BF16)
    kscol, vscol = 2 * g, 3 * g
    grid_spec = pltpu.PrefetchScalarGridSpec(
        num_scalar_prefetch=2,
        grid=(batch, g, len(steps)),
        in_specs=[
            pl.BlockSpec((qn, gw), lambda b, gi, s_, qt, kt: (b * nqt + qt[s_], gi)),
            pl.BlockSpec((tk, dk), lambda b, gi, s_, qt, kt: (b * nkb + kt[s_], kscol + gi)),
            pl.BlockSpec((tk, dk), lambda b, gi, s_, qt, kt: (b * nkb + kt[s_], vscol + gi)),
            pl.BlockSpec((None, None, qn, n_s), lambda b, gi, s_, qt, kt: (b, gi, qt[s_], 0)),
            pl.BlockSpec((None, n_s, LANES), lambda b, gi, s_, qt, kt: (kt[s_], 0, 0)),
            pl.BlockSpec((tk, LANES), lambda b, gi, s_, qt, kt: (0, 0)),
        ],
        out_specs=pl.BlockSpec((qn, gw), lambda b, gi, s_, qt, kt: (b * nqt + qt[s_], gi)),
        scratch_shapes=[pltpu.VMEM((hpg * qn, LANES), F32), pltpu.VMEM((hpg * qn, dk + LANES), F32)],
    )
    o_s = pl.pallas_call(
        functools.partial(_nsa_sel_body, hpg=hpg),
        grid_spec=grid_spec,
        out_shape=jax.ShapeDtypeStruct((n, hd), BF16),
        compiler_params=_params("parallel", "parallel", "arbitrary"),
        name="nsa_selected",
    )(qb_tab, kb_tab, q, kvb, kvb, sel, pick, mk)
    return o_c, o_s, o_w


def _nsa_shared_kv(xb, batch, w_kv, cmp_pe, cmp_w1, cmp_b1, cmp_w2, cmp_b2):
    n, d = xb.shape
    g, dk = NSA_KV_GROUPS, NSA_HEAD_DIM
    t = n // batch
    kv = _matmul(xb, w_kv.astype(BF16), tn=768)
    n_c = t // CMP_STRIDE - 1
    ncp = -(-n_c // LANES) * LANES
    rows = batch * n_c * g
    rows_p = -(-rows // 256) * 256
    outs = []
    for i in range(2):
        z = kv[:, i * g * dk:(i + 1) * g * dk].reshape(batch, t // CMP_STRIDE, CMP_STRIDE, g, dk)
        blk = jnp.concatenate([z[:, :-1], z[:, 1:]], axis=2)
        flat = blk.transpose(0, 1, 3, 2, 4).reshape(rows, CMP_LEN * dk)
        flat = jnp.pad(flat, ((0, rows_p - rows), (0, 0)))
        pe = jnp.broadcast_to(cmp_pe[i][:, None, :], (CMP_LEN, 1, dk)).reshape(CMP_LEN * dk)
        c = _cmp_mlp(flat, pe, cmp_w1[i], cmp_b1[i], cmp_w2[i], cmp_b2[i])[:rows]
        c = c.reshape(batch, n_c, g, dk).transpose(0, 2, 1, 3)
        outs.append(jnp.pad(c, ((0, 0), (0, 0), (0, ncp - n_c), (0, 0))))
    return kv.astype(BF16), outs[0], outs[1], n_c


def _gate_combine_body(oc_ref, os_ref, ow_ref, gl_ref, ex_ref, o_ref):
    gates = jax.nn.sigmoid(gl_ref[...])
    acc = None
    for i, r in enumerate((oc_ref, os_ref, ow_ref)):
        term = _dot_split_const(gates, ex_ref[i]) * r[...].astype(F32)
        acc = term if acc is None else acc + term
    o_ref[...] = acc.astype(BF16)


def _gate_combine(o_c, o_s, o_w, glog, nh, tm=256):
    n, hd = o_c.shape
    dk = hd // nh
    ex = np.zeros((3, LANES, hd), np.float32)
    for i in range(3):
        for h in range(nh):
            ex[i, i * nh + h, h * dk:(h + 1) * dk] = 1.0
    row = pl.BlockSpec((tm, hd), lambda i: (i, 0))
    return pl.pallas_call(
        _gate_combine_body,
        grid=(n // tm,),
        in_specs=[row, row, row, pl.BlockSpec((tm, LANES), lambda i: (i, 0)),
                  pl.BlockSpec((3, LANES, hd), lambda i: (0, 0, 0))],
        out_specs=row,
        out_shape=jax.ShapeDtypeStruct((n, hd), BF16),
        compiler_params=_params("parallel"),
        name="nsa_gate_combine",
    )(o_c, o_s, o_w, glog, jnp.asarray(ex, BF16))


def _nsa_layer(xb, batch, shared, w_in, b_gate, w_o):
    kvb, k_cmp, v_cmp, n_c = shared
    n, d = xb.shape
    hd = w_o.shape[0]
    nh = hd // NSA_HEAD_DIM
    q = _matmul(xb, w_in[:, :hd].astype(BF16), out_dtype=BF16,
                out_scale=NSA_HEAD_DIM ** -0.5 * math.log2(math.e))
    wg = jnp.pad(w_in[:, hd:], ((0, 0), (0, LANES - 3 * nh))).astype(BF16)
    bg = jnp.pad(b_gate, (0, LANES - 3 * nh))
    glog = _matmul(xb, wg, bias=bg)
    o_c, o_s, o_w = _nsa_attention(q, kvb, k_cmp, v_cmp, batch, n_c)
    o = _gate_combine(o_c, o_s, o_w, glog, nh)
    return o, w_o.astype(BF16)


def kernel(x, ln_g, ln_b, rw_mu, rw_w_rkv, rw_w0, rw_w1, rw_w2, rw_a0, rw_a1, rw_a2, rw_g1, rw_g2, rw_k_k, rw_k_a, rw_r_k, rw_lnx_g, rw_lnx_b, rw_w_o, nsa_w_kv, nsa_cmp_pe, nsa_cmp_w1, nsa_cmp_b1, nsa_cmp_w2, nsa_cmp_b2, nsa_w_in, nsa_b_gate, nsa_w_o, moe_router_w, moe_router_b, moe_w_gu, moe_b_gu, moe_w_down, moe_b_down):
    n_seq, t, d = x.shape
    depth = ln_g.shape[0]
    n_a = rw_mu.shape[0]
    alpha = (2 * depth) ** 0.25
    if n_seq > 1:
        args = (ln_g, ln_b, rw_mu, rw_w_rkv, rw_w0, rw_w1, rw_w2, rw_a0, rw_a1, rw_a2, rw_g1, rw_g2, rw_k_k,
                rw_k_a, rw_r_k, rw_lnx_g, rw_lnx_b, rw_w_o, nsa_w_kv, nsa_cmp_pe, nsa_cmp_w1, nsa_cmp_b1,
                nsa_cmp_w2, nsa_cmp_b2, nsa_w_in, nsa_b_gate, nsa_w_o, moe_router_w, moe_router_b, moe_w_gu,
                moe_b_gu, moe_w_down, moe_b_down)
        return jnp.concatenate([kernel(x[b:b + 1], *args) for b in range(n_seq)], axis=0)
    batch = 1
    h = x.reshape(batch * t, d)
    hb = None
    shared = None
    for layer in range(depth):
        if layer < n_a:
            i = layer
            mixed, w_o = _rwkv_time_mix(h, batch, rw_mu[i], rw_w_rkv[i], rw_w0[i], rw_w1[i], rw_w2[i], rw_a0[i],
                                        rw_a1[i], rw_a2[i], rw_g1[i], rw_g2[i], rw_k_k[i], rw_k_a[i], rw_r_k[i],
                                        rw_lnx_g[i], rw_lnx_b[i], rw_w_o[i])
        else:
            if shared is None:
                if hb is None:
                    hb = h.astype(BF16)
                shared = _nsa_shared_kv(hb, batch, nsa_w_kv, nsa_cmp_pe, nsa_cmp_w1, nsa_cmp_b1, nsa_cmp_w2,
                                        nsa_cmp_b2)
            j = layer - n_a
            mixed, w_o = _nsa_layer(hb, batch, shared, nsa_w_in[j], nsa_b_gate[j], nsa_w_o[j])
        h, hb = _proj_ln(mixed, w_o, h, ln_g[layer, 0], ln_b[layer, 0], alpha)
        y4, gates = _moe_ffn(hb, layer, moe_router_w[layer], moe_router_b[layer], moe_w_gu, moe_b_gu, moe_w_down,
                             moe_b_down)
        h, hb = _combine_ln(h, y4, gates, ln_g[layer, 1], ln_b[layer, 1], alpha)
    return h.reshape(batch, t, d)
```
